```python
import jax, jax.numpy as jnp
from jax import lax
import numpy as np

D_MODEL = 1024
BATCH = 4
SEQ = 4096
DEPTH = 1
DEC_BATCH = 128
DEC_SEQ = 4
PAST_LEN = 8192
PAGE_SIZE = 128

EPS = 1e-6
GLA_HEADS = 4
GLA_DK = D_MODEL // 2 // GLA_HEADS
GLA_DV = D_MODEL // GLA_HEADS
GLA_RANK = 16
GLA_TAU = 16.0
GLA_CHUNK = 64
GLA_SCALE = GLA_DK ** -0.5
DSW_GROUPS = ((128, 1), (512, 4), (2048, 16))
DSW_N_GROUPS = len(DSW_GROUPS)
DSW_HEADS = 4
DSW_HEAD_DIM = 64
DSW_SCALE = DSW_HEAD_DIM ** -0.5
BAND_BLOCK = 128
ROPE_THETA = 10000.0
D_FF = -(-8 * D_MODEL // (3 * 256)) * 256
GLA_QK_W = GLA_HEADS * GLA_DK
GLA_V_W = GLA_HEADS * GLA_DV
DSW_W = DSW_N_GROUPS * DSW_HEADS * DSW_HEAD_DIM
DSW_OUT_W = DSW_HEADS * DSW_HEAD_DIM
IN_SIZES = (GLA_QK_W, GLA_QK_W, GLA_V_W, GLA_V_W, GLA_RANK, DSW_W, DSW_W, DSW_W, D_MODEL, D_MODEL)
IN_W = sum(IN_SIZES)
IN_OFFSETS = tuple(int(o) for o in np.cumsum(IN_SIZES)[:-1])

kernel_name = "gla_dilated_window_hybrid_adaln_step"


def _rmsnorm(x, g):
    xf = x.astype(jnp.float32)
    y = xf * lax.rsqrt(jnp.mean(xf * xf, axis=-1, keepdims=True) + EPS)
    return (y * g.astype(jnp.float32)).astype(x.dtype)


def _rope(x, pos):
    half = x.shape[-1] // 2
    inv = ROPE_THETA ** (-jnp.arange(half, dtype=jnp.float32) / half)
    ang = pos[:, None] * inv[None, :]
    cos, sin = jnp.cos(ang)[:, None, :], jnp.sin(ang)[:, None, :]
    xf = x.astype(jnp.float32)
    x1, x2 = xf[..., :half], xf[..., half:]
    return jnp.concatenate([x1 * cos - x2 * sin, x2 * cos + x1 * sin], axis=-1).astype(x.dtype)


def _masked_softmax(s, valid):
    s = jnp.where(valid, s, -jnp.inf)
    m = jnp.max(s, axis=-1, keepdims=True)
    e = jnp.exp(s - m)
    den = jnp.sum(e, axis=-1, keepdims=True)
    return e / den, (m + jnp.log(den))[..., 0]


def _gla(q, k, v, log_a, S0):
    B, T, H, dk = q.shape
    dv = v.shape[-1]
    C = min(GLA_CHUNK, T)
    n = T // C
    f = lambda a: a.astype(jnp.float32).reshape(B, n, C, H, a.shape[-1]).transpose(0, 1, 3, 2, 4)
    qc, kc, vc, lc = f(q) * GLA_SCALE, f(k), f(v), f(log_a)
    b = jnp.cumsum(lc, axis=3)
    b_last = b[:, :, :, -1:, :]
    qg = qc * jnp.exp(b)
    kd = kc * jnp.exp(-b)
    kl = kc * jnp.exp(b_last - b)
    causal = jnp.tril(jnp.ones((C, C), jnp.float32))
    att = jnp.einsum('bnhcd,bnhsd->bnhcs', qg, kd) * causal
    intra = jnp.einsum('bnhcs,bnhsv->bnhcv', att, vc)
    dec = jnp.exp(b_last[:, :, :, 0, :])

    def step(S, inp):
        qg_c, kl_c, v_c, dec_c = inp
        inter = jnp.einsum('bhcd,bhdv->bhcv', qg_c, S)
        S = dec_c[..., None] * S + jnp.einsum('bhcd,bhcv->bhdv', kl_c, v_c)
        return S, inter

    xs = (jnp.moveaxis(qg, 1, 0), jnp.moveaxis(kl, 1, 0), jnp.moveaxis(vc, 1, 0), jnp.moveaxis(dec, 1, 0))
    S_fin, inter = lax.scan(step, S0.astype(jnp.float32), xs)
    o = intra + jnp.moveaxis(inter, 0, 1)
    o = o.transpose(0, 1, 3, 2, 4).reshape(B, T, H, dv)
    return o, S_fin


def _band_attn(q, k, v, max_dist):
    N, L, H, hd = q.shape
    blk = BAND_BLOCK
    nb = -(-L // blk)
    pad = nb * blk - L
    blocks = lambda a: jnp.pad(a, ((0, 0), (0, pad), (0, 0), (0, 0))).reshape(N, nb, blk, H, hd)
    qb, kb, vb = blocks(q), blocks(k), blocks(v)

    def with_prev(a):
        prev = jnp.concatenate([jnp.zeros_like(a[:, :1]), a[:, :-1]], axis=1)
        return jnp.concatenate([prev, a], axis=2)

    kw, vw = with_prev(kb), with_prev(vb)
    s = jnp.einsum('nbqhd,nbkhd->nbhqk', qb, kw).astype(jnp.float32) * DSW_SCALE
    qi = jnp.arange(blk)[:, None] + blk
    ki = jnp.arange(2 * blk)[None, :]
    dist = qi - ki
    kabs = jnp.arange(nb)[:, None, None] * blk - blk + ki[None]
    valid = (dist >= 0) & (dist <= max_dist) & (kabs >= 0)
    p, lse = _masked_softmax(s, valid[None, :, None])
    o = jnp.einsum('nbhqk,nbkhd->nbqhd', p.astype(v.dtype), vw).reshape(N, nb * blk, H, hd)[:, :L]
    lse = lse.transpose(0, 1, 3, 2).reshape(N, nb * blk, H)[:, :L]
    return o, lse


def _dilated_band(q, k, v, dil, n_keys):
    B, T, H, hd = q.shape
    L = T // dil
    fold = lambda a: a.reshape(B, L, dil, H, hd).transpose(0, 2, 1, 3, 4).reshape(B * dil, L, H, hd)
    o, lse = _band_attn(fold(q), fold(k), fold(v), n_keys)
    o = o.reshape(B, dil, L, H, hd).transpose(0, 2, 1, 3, 4).reshape(B, T, H, hd)
    lse = lse.reshape(B, dil, L, H).transpose(0, 2, 1, 3).reshape(B, T, H)
    return o, lse


def _combine(outs, lses):
    w = jax.nn.softmax(jnp.stack(lses, axis=0), axis=0)[..., None]
    return jnp.sum(w * jnp.stack(outs, axis=0).astype(jnp.float32), axis=0)


def _dsw_prompt(q, k, v):
    T = q.shape[1]
    outs, lses, new = [], [], []
    for g, (win, dil) in enumerate(DSW_GROUPS):
        o, lse = _dilated_band(q[:, :, g], k[:, :, g], v[:, :, g], dil, win // dil)
        outs.append(o)
        lses.append(lse)
        keep = min(win, T)
        new.append(jnp.stack([k[:, T - keep:, g], v[:, T - keep:, g]], axis=2))
    return _combine(outs, lses), tuple(new)


def _dsw_sample(q, k, v, caches):
    S = q.shape[1]
    outs, lses, new = [], [], []
    for g, (win, dil) in enumerate(DSW_GROUPS):
        cache = caches[g]
        Wb = cache.shape[1]
        kc = jnp.concatenate([cache[:, :, 0].astype(k.dtype), k[:, :, g]], axis=1)
        vc = jnp.concatenate([cache[:, :, 1].astype(v.dtype), v[:, :, g]], axis=1)
        n_keys = win // dil
        idx = Wb + jnp.arange(S)[:, None] - jnp.arange(n_keys + 1)[None, :] * dil
        valid = idx >= 0
        idxc = jnp.maximum(idx, 0)
        kg, vg = kc[:, idxc], vc[:, idxc]
        s = jnp.einsum('bshd,bskhd->bhsk', q[:, :, g], kg).astype(jnp.float32) * DSW_SCALE
        p, lse = _masked_softmax(s, valid[None, None])
        outs.append(jnp.einsum('bhsk,bskhd->bshd', p.astype(vg.dtype), vg))
        lses.append(lse.transpose(0, 2, 1))
        keep = min(win, Wb + S)
        new.append(jnp.stack([kc[:, Wb + S - keep:], vc[:, Wb + S - keep:]], axis=2))
    return _combine(outs, lses), tuple(new)


def _layer(x, c, pos, S0, caches, norm1_g, norm2_g, w_ada, b_ada, w_in, b_in, w_alpha2, b_alpha2,
           gla_norm_g, w_proj_a, w_proj_b, w_out, w_up, w_down):
    B, T, D = x.shape
    mod = (jax.nn.silu(c) @ w_ada + b_ada).reshape(B, 6, 1, D)
    sh1, sc1, g1, sh2, sc2, g2 = (mod[:, i] for i in range(6))
    h = _rmsnorm(x, norm1_g) * (1 + sc1) + sh1
    proj = h @ w_in + b_in
    gq, gk, gv, gr, glr, dq, dk, dv, ga, gb = jnp.split(proj, IN_OFFSETS, axis=-1)
    heads = lambda a, n: a.reshape(B, T, GLA_HEADS, n)
    log_a = jax.nn.log_sigmoid((glr @ w_alpha2 + b_alpha2).astype(jnp.float32)) / GLA_TAU
    o_a, S_new = _gla(heads(gq, GLA_DK), heads(gk, GLA_DK), heads(gv, GLA_DV), heads(log_a, GLA_DK), S0)
    o_a = (_rmsnorm(o_a, gla_norm_g).reshape(B, T, GLA_V_W) * jax.nn.silu(gr.astype(jnp.float32))).astype(x.dtype)
    dsw = lambda a: a.reshape(B, T, DSW_N_GROUPS * DSW_HEADS, DSW_HEAD_DIM)
    q = _rope(dsw(dq), pos).reshape(B, T, DSW_N_GROUPS, DSW_HEADS, DSW_HEAD_DIM)
    k = _rope(dsw(dk), pos).reshape(B, T, DSW_N_GROUPS, DSW_HEADS, DSW_HEAD_DIM)
    v = dv.reshape(B, T, DSW_N_GROUPS, DSW_HEADS, DSW_HEAD_DIM)
    if caches is None:
        o_b, new_kv = _dsw_prompt(q, k, v)
    else:
        o_b, new_kv = _dsw_sample(q, k, v, caches)
    o_b = o_b.reshape(B, T, DSW_OUT_W).astype(x.dtype)
    merged = jax.nn.sigmoid(ga) * (o_a @ w_proj_a) + jax.nn.sigmoid(gb) * (o_b @ w_proj_b)
    x = x + g1 * (merged @ w_out)
    h2 = _rmsnorm(x, norm2_g) * (1 + sc2) + sh2
    u1, u2 = jnp.split(h2 @ w_up, 2, axis=-1)
    x = x + g2 * ((jax.nn.silu(u1) * u2) @ w_down)
    return x, S_new, new_kv


def setup_inputs(seed: int = 0) -> dict:
    key = jax.random.key(seed)
    ks = jax.random.split(key, 24)
    nrm = lambda k, shape, scale: jax.random.normal(k, shape, jnp.float32) * scale
    D = D_MODEL
    kv_shape = lambda win: (DEPTH, DEC_BATCH, min(win, PAST_LEN), 2, DSW_HEADS, DSW_HEAD_DIM)
    return {
        "x_prompt": nrm(ks[0], (BATCH, SEQ, D), 1.0),
        "x_sample": nrm(ks[1], (DEC_BATCH, DEC_SEQ, D), 1.0),
        "state_gla": nrm(ks[2], (DEPTH, DEC_BATCH, GLA_HEADS, GLA_DK, GLA_DV), 1.0),
        "cache_kv_w128": nrm(ks[3], kv_shape(DSW_GROUPS[0][0]), 1.0),
        "cache_kv_w512": nrm(ks[4], kv_shape(DSW_GROUPS[1][0]), 1.0),
        "cache_kv_w2048": nrm(ks[5], kv_shape(DSW_GROUPS[2][0]), 1.0),
        "c_prompt": nrm(ks[6], (BATCH, D), 1.0),
        "c_sample": nrm(ks[7], (DEC_BATCH, D), 1.0),
        "norm1_g": 1.0 + nrm(ks[8], (DEPTH, D), 0.05),
        "norm2_g": 1.0 + nrm(ks[9], (DEPTH, D), 0.05),
        "w_ada": nrm(ks[10], (DEPTH, D, 6 * D), 0.5 * D ** -0.5),
        "b_ada": nrm(ks[11], (DEPTH, 6 * D), 0.02),
        "w_in": nrm(ks[12], (DEPTH, D, IN_W), D ** -0.5),
        "b_in": nrm(ks[13], (DEPTH, IN_W), 0.02),
        "w_alpha2": nrm(ks[14], (DEPTH, GLA_RANK, GLA_QK_W), GLA_RANK ** -0.5),
        "b_alpha2": nrm(ks[15], (DEPTH, GLA_QK_W), 0.1),
        "gla_norm_g": 1.0 + nrm(ks[16], (DEPTH, GLA_DV), 0.05),
        "w_proj_a": nrm(ks[17], (DEPTH, GLA_V_W, D), GLA_V_W ** -0.5),
        "w_proj_b": nrm(ks[18], (DEPTH, DSW_OUT_W, D), DSW_OUT_W ** -0.5),
        "w_out": nrm(ks[19], (DEPTH, D, D), D ** -0.5),
        "w_up": nrm(ks[20], (DEPTH, D, 2 * D_FF), D ** -0.5),
        "w_down": nrm(ks[21], (DEPTH, D_FF, D), D_FF ** -0.5),
        "normf_g": 1.0 + nrm(ks[22], (D,), 0.05),
    }


def reference(x_prompt, x_sample, state_gla, cache_kv_w128, cache_kv_w512, cache_kv_w2048, c_prompt, c_sample,
              norm1_g, norm2_g, w_ada, b_ada, w_in, b_in, w_alpha2, b_alpha2, gla_norm_g, w_proj_a, w_proj_b,
              w_out, w_up, w_down, normf_g):
    pos_p = jnp.arange(x_prompt.shape[1], dtype=jnp.float32)
    pos_s = PAST_LEN + jnp.arange(x_sample.shape[1], dtype=jnp.float32)
    S0_p = jnp.zeros((x_prompt.shape[0], GLA_HEADS, GLA_DK, GLA_DV), jnp.float32)
    xp, xs = x_prompt, x_sample
    sp_list, kvp_list, ss_list, kvs_list = [], [], [], []
    for l in range(DEPTH):
        lw = (norm1_g[l], norm2_g[l], w_ada[l], b_ada[l], w_in[l], b_in[l], w_alpha2[l], b_alpha2[l],
              gla_norm_g[l], w_proj_a[l], w_proj_b[l], w_out[l], w_up[l], w_down[l])
        xp, s_p, kv_p = _layer(xp, c_prompt, pos_p, S0_p, None, *lw)
        xs, s_s, kv_s = _layer(xs, c_sample, pos_s, state_gla[l],
                               (cache_kv_w128[l], cache_kv_w512[l], cache_kv_w2048[l]), *lw)
        sp_list.append(s_p)
        kvp_list.append(kv_p)
        ss_list.append(s_s)
        kvs_list.append(kv_s)
    y_prompt = _rmsnorm(xp, normf_g)
    y_sample = _rmsnorm(xs, normf_g)
    state_gla_p = jnp.stack(sp_list).astype(x_prompt.dtype)
    kv128_p = jnp.stack([kv[0] for kv in kvp_list])
    kv512_p = jnp.stack([kv[1] for kv in kvp_list])
    kv2048_p = jnp.stack([kv[2] for kv in kvp_list])
    state_gla_s = jnp.stack(ss_list).astype(state_gla.dtype)
    kv128_s = jnp.stack([kv[0] for kv in kvs_list])
    kv512_s = jnp.stack([kv[1] for kv in kvs_list])
    kv2048_s = jnp.stack([kv[2] for kv in kvs_list])
    return (y_prompt, y_sample, state_gla_p, kv128_p, kv512_p, kv2048_p, state_gla_s, kv128_s, kv512_s, kv2048_s)
```

```python
import functools

import jax
import jax.numpy as jnp
from jax import lax
from jax.experimental import pallas as pl
from jax.experimental.pallas import tpu as pltpu

F32 = jnp.float32
BF16 = jnp.bfloat16

EPS = 1e-6
GLA_HEADS = 4
GLA_DK = 128
GLA_DV = 256
GLA_RANK = 16
GLA_TAU = 16.0
GLA_CHUNK = 64
GLA_SCALE = GLA_DK ** -0.5
DSW_GROUPS = ((128, 1), (512, 4), (2048, 16))
DSW_HEADS = 4
DSW_HEAD_DIM = 64
DSW_SCALE = DSW_HEAD_DIM ** -0.5
DSW_GW = DSW_HEADS * DSW_HEAD_DIM
BAND = 128
ROPE_THETA = 10000.0
PAST_LEN = 8192
LANES = 128
VMEM_LIMIT = 56 * 1024 * 1024


def _nn(a, b):
    return jnp.dot(a, b, preferred_element_type=F32)


def _nt(a, b):
    return lax.dot_general(a, b, (((1,), (1,)), ((), ())), preferred_element_type=F32)


def _tn(a, b):
    return lax.dot_general(a, b, (((0,), (0,)), ((), ())), preferred_element_type=F32)


def _split3(x):
    hi = x.astype(BF16)
    r1 = x - hi.astype(F32)
    mid = r1.astype(BF16)
    lo = (r1 - mid.astype(F32)).astype(BF16)
    return hi, mid, lo


def _iota(shape, dim):
    return lax.broadcasted_iota(jnp.int32, shape, dim)


def _rms(x, g):
    return x * lax.rsqrt(jnp.mean(x * x, axis=-1, keepdims=True) + EPS) * g


def _params(n_parallel, n_arbitrary=0):
    sem = ("parallel",) * n_parallel + ("arbitrary",) * n_arbitrary
    return pltpu.CompilerParams(dimension_semantics=sem, vmem_limit_bytes=VMEM_LIMIT)


def _resident(shape):
    nd = len(shape)
    return pl.BlockSpec(shape, lambda *_: (0,) * nd, pipeline_mode=pl.Buffered(1))


def _ada_kernel(c_ref, w_ref, b_ref, o_ref):
    c = c_ref[...]
    a = (c * jax.nn.sigmoid(c)).astype(BF16)
    o_ref[...] = _nn(a, w_ref[...].astype(BF16)) + b_ref[...]


def _ada(c_all, w_ada, b_ada):
    n, d = c_all.shape
    ncol = w_ada.shape[1]
    tn = 1536
    return pl.pallas_call(
        _ada_kernel,
        grid=(ncol // tn,),
        in_specs=[pl.BlockSpec((n, d), lambda j: (0, 0)),
                  pl.BlockSpec((d, tn), lambda j: (0, j)),
                  pl.BlockSpec((1, tn), lambda j: (0, j))],
        out_specs=pl.BlockSpec((n, tn), lambda j: (0, j)),
        out_shape=jax.ShapeDtypeStruct((n, ncol), F32),
        compiler_params=_params(1),
        name="ada_mod",
    )(c_all, w_ada, b_ada.reshape(1, ncol))


def _rope(x, cosf, sins, first_half):
    rot = jnp.where(first_half, pltpu.roll(x, LANES - 32, 1), pltpu.roll(x, 32, 1))
    return x * cosf + rot * sins


def _inproj_kernel(x_ref, sc_ref, sh_ref, g_ref, cos_ref, sin_ref,
                   wa_ref, ba_ref, wg_ref, bg_ref, w2_ref, b2_ref, wb_ref, bb_ref, wc_ref, bc_ref,
                   gq_ref, gk_ref, gv_ref, gr_ref, la_ref, qb_ref, kb_ref, vb_ref, k32_ref, v32_ref,
                   ga_ref, gb_ref):
    x = x_ref[...]
    h = (_rms(x, g_ref[...]) * (1.0 + sc_ref[...]) + sh_ref[...]).astype(BF16)

    gq_ref[...] = _nn(h, wa_ref[:, 0:512]) + ba_ref[:, 0:512]
    gk_ref[...] = _nn(h, wa_ref[:, 512:1024]) + ba_ref[:, 512:1024]
    gv_ref[...] = _nn(h, wa_ref[:, 1024:2048]) + ba_ref[:, 1024:2048]
    gr_ref[...] = _nn(h, wa_ref[:, 2048:3072]) + ba_ref[:, 2048:3072]

    glr = (_nn(h, wg_ref[...]) + bg_ref[...]).astype(BF16)
    z = _nn(glr, w2_ref[...]) + b2_ref[...]
    la_ref[...] = jax.nn.log_sigmoid(z) * (1.0 / GLA_TAU)

    cosf = cos_ref[...]
    sins = sin_ref[...]
    first_half = (_iota(cosf.shape, 1) % DSW_HEAD_DIM) < (DSW_HEAD_DIM // 2)
    n_chunks = 3 * DSW_GW // LANES
    for c in range(n_chunks):
        cols = slice(c * LANES, (c + 1) * LANES)
        q = _nn(h, wb_ref[:, c * LANES:(c + 1) * LANES]) + bb_ref[:, cols]
        qb_ref[:, cols] = (_rope(q, cosf, sins, first_half) * DSW_SCALE).astype(BF16)
        k = _nn(h, wb_ref[:, 768 + c * LANES:768 + (c + 1) * LANES]) + bb_ref[:, 768 + c * LANES:768 + (c + 1) * LANES]
        k = _rope(k, cosf, sins, first_half)
        k32_ref[:, cols] = k
        kb_ref[:, cols] = k.astype(BF16)
    v = _nn(h, wb_ref[:, 1536:2304]) + bb_ref[:, 1536:2304]
    v32_ref[...] = v
    vb_ref[...] = v.astype(BF16)

    ga_ref[...] = _nn(h, wc_ref[:, 0:1024]) + bc_ref[:, 0:1024]
    gb_ref[...] = _nn(h, wc_ref[:, 1024:2048]) + bc_ref[:, 1024:2048]


def _mod_spec(arr, tm):
    if arr.shape[1] == 1:
        return pl.BlockSpec((None, 1, arr.shape[2]), lambda b, i: (b, 0, 0))
    return pl.BlockSpec((None, tm, arr.shape[2]), lambda b, i: (b, i, 0))


def _inproj(x, sc, sh, g, cos_t, sin_t, w, tm):
    nb, t, d = x.shape
    tok = lambda n: pl.BlockSpec((None, tm, n), lambda b, i: (b, i, 0))
    out_cols = (512, 512, 1024, 1024, 512, 768, 768, 768, 768, 768, 1024, 1024)
    out_dt = (F32, F32, F32, F32, F32, BF16, BF16, BF16, F32, F32, F32, F32)
    weights = (w["wa"], w["ba"], w["wg"], w["bg"], w["w2"], w["b2"], w["wb"], w["bb"], w["wc"], w["bc"])
    return pl.pallas_call(
        _inproj_kernel,
        grid=(nb, t // tm),
        in_specs=[tok(d), _mod_spec(sc, tm), _mod_spec(sh, tm), _resident((1, d)),
                  pl.BlockSpec((tm, LANES), lambda b, i: (i, 0)),
                  pl.BlockSpec((tm, LANES), lambda b, i: (i, 0))]
                 + [_resident(a.shape) for a in weights],
        out_specs=[tok(n) for n in out_cols],
        out_shape=[jax.ShapeDtypeStruct((nb, t, n), dt) for n, dt in zip(out_cols, out_dt)],
        compiler_params=_params(2),
        name="inproj",
    )(x, sc, sh, g, cos_t, sin_t, *weights)


def _gla_local(gq_ref, gk_ref, la_ref, chunk):
    la = la_ref[...]
    tt = la.shape[0]
    r = _iota((tt, tt), 0)
    c = _iota((tt, tt), 1)
    same = (r // chunk) == (c // chunk)
    tri = jnp.where(same & (c <= r), 1.0, 0.0).astype(BF16)
    ones = jnp.where(same, 1.0, 0.0).astype(BF16)
    hi, mid, lo = _split3(la)
    b = _nn(tri, hi) + _nn(tri, mid) + _nn(tri, lo)
    bl = _nn(ones, hi) + _nn(ones, mid) + _nn(ones, lo)
    qg = (gq_ref[...] * GLA_SCALE * jnp.exp(b)).astype(BF16)
    kd = (gk_ref[...] * jnp.exp(-b)).astype(BF16)
    kl = (gk_ref[...] * jnp.exp(bl - b)).astype(BF16)
    causal = same & (c <= r)
    return qg, kd, kl, jnp.exp(bl), causal


def _gla_finish(o, gr, g):
    return (_rms(o, g) * (gr * jax.nn.sigmoid(gr))).astype(BF16)


def _gla_prompt_kernel(gq_ref, gk_ref, gv_ref, la_ref, gr_ref, g_ref, o_ref, st_ref):
    @pl.when(pl.program_id(1) == 0)
    def _():
        st_ref[...] = jnp.zeros_like(st_ref)

    qg, kd, kl, dec, causal = _gla_local(gq_ref, gk_ref, la_ref, GLA_CHUNK)
    tt = qg.shape[0]
    for h in range(GLA_HEADS):
        kc = slice(h * GLA_DK, (h + 1) * GLA_DK)
        vc = slice(h * GLA_DV, (h + 1) * GLA_DV)
        v = gv_ref[:, vc].astype(BF16)
        att = jnp.where(causal, _nt(qg[:, kc], kd[:, kc]), 0.0).astype(BF16)
        intra = _nn(att, v)
        st = st_ref[h]
        inter = []
        for ci in range(tt // GLA_CHUNK):
            rows = slice(ci * GLA_CHUNK, (ci + 1) * GLA_CHUNK)
            inter.append(_nt(qg[rows, kc], st.astype(BF16)))
            st = dec[ci * GLA_CHUNK:ci * GLA_CHUNK + 1, kc] * st + _tn(v[rows], kl[rows, kc])
        st_ref[h] = st
        o = intra + jnp.concatenate(inter, axis=0)
        o_ref[:, vc] = _gla_finish(o, gr_ref[:, vc], g_ref[...])


def _gla_prompt(gq, gk, gv, la, gr, g, tt):
    nb, t, _ = gq.shape
    tok = lambda n: pl.BlockSpec((None, tt, n), lambda b, i: (b, i, 0))
    return pl.pallas_call(
        _gla_prompt_kernel,
        grid=(nb, t // tt),
        in_specs=[tok(512), tok(512), tok(1024), tok(512), tok(1024), _resident((1, GLA_DV))],
        out_specs=[tok(1024),
                   pl.BlockSpec((None, GLA_HEADS, GLA_DV, GLA_DK), lambda b, i: (b, 0, 0, 0))],
        out_shape=[jax.ShapeDtypeStruct((nb, t, 1024), BF16),
                   jax.ShapeDtypeStruct((nb, GLA_HEADS, GLA_DV, GLA_DK), F32)],
        compiler_params=_params(1, 1),
        name="gla_prompt",
    )(gq, gk, gv, la, gr, g)


def _gla_sample_kernel(gq_ref, gk_ref, gv_ref, la_ref, gr_ref, g_ref, s_ref, o_ref, so_ref, *, seq):
    qg, kd, kl, dec, causal = _gla_local(gq_ref, gk_ref, la_ref, seq)
    rows_total = qg.shape[0]
    per8 = 8 // seq
    row8 = _iota((8, 1), 0)
    for h in range(GLA_HEADS):
        kc = slice(h * GLA_DK, (h + 1) * GLA_DK)
        vc = slice(h * GLA_DV, (h + 1) * GLA_DV)
        v = gv_ref[:, vc].astype(BF16)
        att = jnp.where(causal, _nt(qg[:, kc], kd[:, kc]), 0.0).astype(BF16)
        intra = _nn(att, v)
        inter = []
        for p in range(rows_total // 8):
            rows = slice(p * 8, (p + 1) * 8)
            d_hi, d_mid, d_lo = _split3(dec[rows, kc])
            inter_p = jnp.zeros((8, GLA_DV), F32)
            for j in range(per8):
                b = p * per8 + j
                r0 = j * seq
                s0 = s_ref[b, h]
                mine = (row8 >= r0) & (row8 < r0 + seq)
                inter_p = jnp.where(mine, _nn(qg[rows, kc], s0.astype(BF16)), inter_p)
                dl = jnp.where(row8 == r0, d_hi, jnp.where(row8 == r0 + 1, d_mid,
                               jnp.where(row8 == r0 + 2, d_lo, jnp.zeros_like(d_lo))))
                e = jnp.where((row8 >= r0) & (row8 < r0 + 3), 1.0, 0.0).astype(BF16)
                dec_b = _tn(dl, jnp.broadcast_to(e, (8, GLA_DV)))
                upd = _tn(jnp.where(mine, kl[rows, kc], jnp.zeros_like(kl[rows, kc])), v[rows])
                so_ref[b, h] = dec_b * s0 + upd
            inter.append(inter_p)
        o = intra + jnp.concatenate(inter, axis=0)
        o_ref[:, vc] = _gla_finish(o, gr_ref[:, vc], g_ref[...])


def _gla_sample(gq, gk, gv, la, gr, g, s0, seq, bb):
    n_seq = s0.shape[0]
    rows = bb * seq
    tok = lambda n: pl.BlockSpec((None, rows, n), lambda i: (0, i, 0))
    st = pl.BlockSpec((bb, GLA_HEADS, GLA_DK, GLA_DV), lambda i: (i, 0, 0, 0))
    return pl.pallas_call(
        functools.partial(_gla_sample_kernel, seq=seq),
        grid=(n_seq // bb,),
        in_specs=[tok(512), tok(512), tok(1024), tok(512), tok(1024), _resident((1, GLA_DV)), st],
        out_specs=[tok(1024), st],
        out_shape=[jax.ShapeDtypeStruct((1, n_seq * seq, 1024), BF16),
                   jax.ShapeDtypeStruct(s0.shape, F32)],
        compiler_params=_params(1),
        name="gla_sample",
    )(gq, gk, gv, la, gr, g, s0)


def _dsw_prompt_kernel(q_ref, kp_ref, kc_ref, vp_ref, vc_ref, o_ref, lse_ref):
    qb = q_ref.shape[0]
    first_key = jnp.where(pl.program_id(2) == 0, BAND, 0)
    qi = _iota((BAND, 2 * BAND), 0) + BAND
    ki = _iota((BAND, 2 * BAND), 1)
    band = (qi - ki >= 0) & (qi - ki <= BAND)
    lane = _iota((BAND, LANES), 1)
    for s in range(qb // BAND):
        rows = slice(s * BAND, (s + 1) * BAND)
        if s == 0:
            valid = band & (ki >= first_key)
        else:
            valid = band
        for hp in range(DSW_GW // LANES):
            cols = slice(hp * LANES, (hp + 1) * LANES)
            qp = q_ref[rows, cols]
            if s == 0:
                kcat = jnp.concatenate([kp_ref[:, cols], kc_ref[0:BAND, cols]], axis=0)
                vcat = jnp.concatenate([vp_ref[:, cols], vc_ref[0:BAND, cols]], axis=0)
            else:
                kcat = kc_ref[(s - 1) * BAND:(s + 1) * BAND, cols]
                vcat = vc_ref[(s - 1) * BAND:(s + 1) * BAND, cols]
            outs, lses = [], []
            for hh in range(LANES // DSW_HEAD_DIM):
                in_head = (lane // DSW_HEAD_DIM) == hh
                sc = _nt(jnp.where(in_head, qp, jnp.zeros_like(qp)), kcat)
                sc = jnp.where(valid, sc, -jnp.inf)
                m = jnp.max(sc, axis=-1, keepdims=True)
                e = jnp.exp(sc - m)
                den = jnp.sum(e, axis=-1, keepdims=True)
                outs.append(_nn((e / den).astype(BF16), vcat))
                lses.append(m + jnp.log(den))
            first = lane < DSW_HEAD_DIM
            o_ref[rows, cols] = jnp.where(first, outs[0], outs[1])
            lse_ref[rows, cols] = jnp.where(first, lses[0], lses[1])


def _dsw_prompt(qb, kb, vb, g, dil):
    nb, t, _ = qb.shape
    seq_len = t // dil
    tq = min(512, seq_len)
    sub = tq // BAND
    view = lambda a: a.reshape(nb, seq_len, dil * 3 * DSW_GW)
    cur = pl.BlockSpec((None, tq, DSW_GW), lambda b, r, j: (b, j, r * 3 + g))
    prev = pl.BlockSpec((None, BAND, DSW_GW), lambda b, r, j: (b, jnp.maximum(j * sub - 1, 0), r * 3 + g))
    out = pl.BlockSpec((None, tq, DSW_GW), lambda b, r, j: (b, j, r))
    o, lse = pl.pallas_call(
        _dsw_prompt_kernel,
        grid=(nb, dil, seq_len // tq),
        in_specs=[cur, prev, cur, prev, cur],
        out_specs=[out, out],
        out_shape=[jax.ShapeDtypeStruct((nb, seq_len, dil * DSW_GW), F32)] * 2,
        compiler_params=_params(3),
        name=f"dsw_prompt_d{dil}",
    )(view(qb), view(kb), view(kb), view(vb), view(vb))
    return o.reshape(nb, t, DSW_GW), lse.reshape(nb, t, DSW_GW)


def _dsw_sample_kernel(q_ref, kn_ref, vn_ref, c_ref, o_ref, lse_ref, co_ref, *, seq, dil):
    kv_w = 2 * DSW_GW
    per8 = 8 // seq
    n_rows = DSW_HEADS * 8
    lane = _iota((8, DSW_GW), 1)
    q8 = q_ref[...]
    qrows = jnp.concatenate([jnp.where((lane // DSW_HEAD_DIM) == h, q8, jnp.zeros_like(q8))
                             for h in range(DSW_HEADS)], axis=0).astype(BF16)
    knb = kn_ref[...].astype(BF16)
    vnb = vn_ref[...].astype(BF16)
    r = _iota((n_rows, 1), 0)
    r_step = r % seq
    r_seq = (r % 8) // seq
    n_cache = c_ref.shape[1]

    if dil == 1:
        col = _iota((n_rows, n_cache), 1)
        comb = jnp.full((n_rows, n_cache), -jnp.inf, F32)
        for j in range(per8):
            sc = _nt(qrows, c_ref[j, :, 0:DSW_GW].astype(BF16))
            comb = jnp.where((r_seq == j) & (col >= r_step), sc, comb)
        pad = jnp.zeros((n_cache - 8, DSW_GW), BF16)
        knb_t = jnp.concatenate([knb, pad], axis=0)
        vnb_t = jnp.concatenate([vnb, pad], axis=0)
        new_ok = (col < 8) & ((col // seq) == r_seq) & ((col % seq) <= r_step)
        scn = jnp.where(new_ok, _nt(qrows, knb_t), -jnp.inf)
        m = jnp.maximum(jnp.max(comb, axis=-1, keepdims=True), jnp.max(scn, axis=-1, keepdims=True))
        e = jnp.exp(comb - m)
        en = jnp.exp(scn - m)
        den = jnp.sum(e, axis=-1, keepdims=True) + jnp.sum(en, axis=-1, keepdims=True)
        p = (e / den).astype(BF16)
        acc = _nn((en / den).astype(BF16), vnb_t)
        for j in range(per8):
            pj = jnp.where(r_seq == j, p, jnp.zeros_like(p))
            acc = acc + _nn(pj, c_ref[j, :, DSW_GW:kv_w].astype(BF16))
    else:
        comb = jnp.zeros((n_rows, n_cache), F32)
        for j in range(per8):
            for s in range(seq):
                sc = _nt(qrows, c_ref[j, :, s * kv_w:s * kv_w + DSW_GW].astype(BF16))
                comb = jnp.where((r_seq == j) & (r_step == s), sc, comb)
        kn_rows = jnp.concatenate([knb.astype(F32)] * DSW_HEADS, axis=0)
        vn_rows = jnp.concatenate([vnb.astype(F32)] * DSW_HEADS, axis=0)
        own = jnp.sum(qrows.astype(F32) * kn_rows, axis=-1, keepdims=True)
        m = jnp.maximum(jnp.max(comb, axis=-1, keepdims=True), own)
        e = jnp.exp(comb - m)
        e_own = jnp.exp(own - m)
        den = jnp.sum(e, axis=-1, keepdims=True) + e_own
        p = (e / den).astype(BF16)
        acc = (e_own / den).astype(BF16).astype(F32) * vn_rows
        for j in range(per8):
            for s in range(seq):
                pj = jnp.where((r_seq == j) & (r_step == s), p, jnp.zeros_like(p))
                acc = acc + _nn(pj, c_ref[j, :, s * kv_w + DSW_GW:(s + 1) * kv_w].astype(BF16))
    lse = m + jnp.log(den)

    o8 = jnp.zeros((8, DSW_GW), F32)
    l8 = jnp.zeros((8, DSW_GW), F32)
    for h in range(DSW_HEADS):
        in_head = (lane // DSW_HEAD_DIM) == h
        o8 = jnp.where(in_head, acc[h * 8:(h + 1) * 8], o8)
        l8 = jnp.where(in_head, lse[h * 8:(h + 1) * 8], l8)
    o_ref[...] = o8
    lse_ref[...] = l8

    n_view = c_ref.shape[1]
    row_w = c_ref.shape[2]
    tok_per_row = row_w // kv_w
    for j in range(per8):
        if tok_per_row == 1:
            co_ref[j, 0:n_view - seq, :] = c_ref[j, seq:n_view, :]
            co_ref[j, n_view - seq:n_view, 0:DSW_GW] = kn_ref[j * seq:(j + 1) * seq, :]
            co_ref[j, n_view - seq:n_view, DSW_GW:kv_w] = vn_ref[j * seq:(j + 1) * seq, :]
        else:
            keep = row_w - seq * kv_w
            if keep:
                co_ref[j, :, 0:keep] = c_ref[j, :, seq * kv_w:row_w]
            co_ref[j, 0:n_view - 1, keep:row_w] = c_ref[j, 1:n_view, 0:seq * kv_w]
            for s in range(seq):
                base = keep + s * kv_w
                co_ref[j, n_view - 1:n_view, base:base + DSW_GW] = kn_ref[j * seq + s:j * seq + s + 1, :]
                co_ref[j, n_view - 1:n_view, base + DSW_GW:base + kv_w] = vn_ref[j * seq + s:j * seq + s + 1, :]


def _dsw_sample(qb, k32, v32, cache, g, dil, seq):
    n_seq, win = cache.shape[0], cache.shape[1]
    kv_w = 2 * DSW_GW
    per8 = 8 // seq
    tok_per_row = 1 if dil == 1 else max(dil, seq)
    view = cache.reshape(n_seq, win // tok_per_row, tok_per_row * kv_w)
    tok = pl.BlockSpec((None, 8, DSW_GW), lambda i: (0, i, g))
    tok_out = pl.BlockSpec((None, 8, DSW_GW), lambda i: (0, i, 0))
    cspec = pl.BlockSpec((per8,) + view.shape[1:], lambda i: (i, 0, 0))
    o, lse, new = pl.pallas_call(
        functools.partial(_dsw_sample_kernel, seq=seq, dil=dil),
        grid=(n_seq // per8,),
        in_specs=[tok, tok, tok, cspec],
        out_specs=[tok_out, tok_out, cspec],
        out_shape=[jax.ShapeDtypeStruct((1, n_seq * seq, DSW_GW), F32)] * 2
                  + [jax.ShapeDtypeStruct(view.shape, F32)],
        compiler_params=_params(1),
        name=f"dsw_sample_d{dil}",
    )(qb, k32, v32, view)
    return o, lse, new.reshape(cache.shape)


def _merge_kernel(oa_ref, o0_ref, o1_ref, o2_ref, l0_ref, l1_ref, l2_ref, ga_ref, gb_ref, x_ref,
                  g1_ref, sc_ref, sh_ref, n2_ref, wpa_ref, wpb_ref, wo_ref, x1_ref, h2_ref):
    l0, l1, l2 = l0_ref[...], l1_ref[...], l2_ref[...]
    m = jnp.maximum(jnp.maximum(l0, l1), l2)
    w0, w1, w2 = jnp.exp(l0 - m), jnp.exp(l1 - m), jnp.exp(l2 - m)
    den = w0 + w1 + w2
    ob = (w0 / den) * o0_ref[...] + (w1 / den) * o1_ref[...] + (w2 / den) * o2_ref[...]
    merged = (jax.nn.sigmoid(ga_ref[...]) * _nn(oa_ref[...], wpa_ref[...])
              + jax.nn.sigmoid(gb_ref[...]) * _nn(ob.astype(BF16), wpb_ref[...]))
    x1 = x_ref[...] + g1_ref[...] * _nn(merged.astype(BF16), wo_ref[...])
    x1_ref[...] = x1
    h2_ref[...] = (_rms(x1, n2_ref[...]) * (1.0 + sc_ref[...]) + sh_ref[...]).astype(BF16)


def _merge(oa, og, lg, ga, gb, x, g1, sc2, sh2, n2, wpa, wpb, wo, tm):
    nb, t, d = x.shape
    tok = lambda n: pl.BlockSpec((None, tm, n), lambda b, i: (b, i, 0))
    return pl.pallas_call(
        _merge_kernel,
        grid=(nb, t // tm),
        in_specs=[tok(1024)] + [tok(DSW_GW)] * 6 + [tok(d), tok(d), tok(d),
                  _mod_spec(g1, tm), _mod_spec(sc2, tm), _mod_spec(sh2, tm), _resident((1, d)),
                  _resident(wpa.shape), _resident(wpb.shape), _resident(wo.shape)],
        out_specs=[tok(d), tok(d)],
        out_shape=[jax.ShapeDtypeStruct((nb, t, d), F32), jax.ShapeDtypeStruct((nb, t, d), BF16)],
        compiler_params=_params(2),
        name="merge_outproj",
    )(oa, *og, *lg, ga, gb, x, g1, sc2, sh2, n2, wpa, wpb, wo)


def _ffn_kernel(h_ref, x_ref, g2_ref, nf_ref, wu_ref, wd_ref, y_ref, *, final_norm, n_split):
    h = h_ref[...]
    d_ff = wd_ref.shape[0]
    step = d_ff // n_split
    acc = None
    for j in range(n_split):
        u1 = _nn(h, wu_ref[:, j * step:(j + 1) * step])
        u2 = _nn(h, wu_ref[:, d_ff + j * step:d_ff + (j + 1) * step])
        a = (u1 * jax.nn.sigmoid(u1) * u2).astype(BF16)
        part = _nn(a, wd_ref[j * step:(j + 1) * step, :])
        acc = part if acc is None else acc + part
    x2 = x_ref[...] + g2_ref[...] * acc
    y_ref[...] = _rms(x2, nf_ref[...]) if final_norm else x2


def _ffn(h2, x1, g2, nf, wu, wd, tm, final_norm):
    nb, t, d = x1.shape
    tok = lambda n: pl.BlockSpec((None, tm, n), lambda b, i: (b, i, 0))
    return pl.pallas_call(
        functools.partial(_ffn_kernel, final_norm=final_norm, n_split=2),
        grid=(nb, t // tm),
        in_specs=[tok(d), tok(d), _mod_spec(g2, tm), _resident((1, d)),
                  _resident(wu.shape), _resident(wd.shape)],
        out_specs=tok(d),
        out_shape=jax.ShapeDtypeStruct((nb, t, d), F32),
        compiler_params=_params(2),
        name="ffn",
    )(h2, x1, g2, nf, wu, wd)


def _rope_tables(pos):
    half = DSW_HEAD_DIM // 2
    inv = ROPE_THETA ** (-jnp.arange(half, dtype=F32) / half)
    ang = pos[:, None] * inv[None, :]
    cos, sin = jnp.cos(ang), jnp.sin(ang)
    reps = LANES // half
    cosf = jnp.tile(cos, (1, reps))
    sign = jnp.tile(jnp.concatenate([-jnp.ones((half,), F32), jnp.ones((half,), F32)]), LANES // DSW_HEAD_DIM)
    return cosf, jnp.tile(sin, (1, reps)) * sign[None, :]


def _layer_weights(w_in, b_in, w_alpha2, b_alpha2, w_proj_a, w_proj_b, w_out, w_up, w_down):
    bf = lambda a: a.astype(BF16)
    row = lambda a: a.reshape(1, -1)
    o_glr, o_dq, o_ga = 3072, 3088, 5392
    pad_r = LANES - GLA_RANK
    return dict(
        wa=bf(w_in[:, :o_glr]), ba=row(b_in[:o_glr]),
        wg=bf(jnp.pad(w_in[:, o_glr:o_dq], ((0, 0), (0, pad_r)))), bg=row(jnp.pad(b_in[o_glr:o_dq], (0, pad_r))),
        w2=bf(jnp.pad(w_alpha2, ((0, pad_r), (0, 0)))), b2=row(b_alpha2),
        wb=bf(w_in[:, o_dq:o_ga]), bb=row(b_in[o_dq:o_ga]),
        wc=bf(w_in[:, o_ga:]), bc=row(b_in[o_ga:]),
        wpa=bf(w_proj_a), wpb=bf(w_proj_b), wo=bf(w_out), wu=bf(w_up), wd=bf(w_down))


def _kv_stack(k32, v32, g, keep):
    nb, t, _ = k32.shape
    cols = slice(g * DSW_GW, (g + 1) * DSW_GW)
    k = k32[:, t - keep:, cols].reshape(nb, keep, DSW_HEADS, DSW_HEAD_DIM)
    v = v32[:, t - keep:, cols].reshape(nb, keep, DSW_HEADS, DSW_HEAD_DIM)
    return jnp.stack([k, v], axis=2)


def kernel(x_prompt, x_sample, state_gla, cache_kv_w128, cache_kv_w512, cache_kv_w2048, c_prompt, c_sample,
           norm1_g, norm2_g, w_ada, b_ada, w_in, b_in, w_alpha2, b_alpha2, gla_norm_g, w_proj_a, w_proj_b,
           w_out, w_up, w_down, normf_g):
    depth = w_ada.shape[0]
    nb, t, d = x_prompt.shape
    n_seq, seq, _ = x_sample.shape
    assert 8 % seq == 0 and seq >= 3, "sample kernels pack whole sequences into 8-row groups"
    past = PAST_LEN
    caches = (cache_kv_w128, cache_kv_w512, cache_kv_w2048)

    cos_p, sin_p = _rope_tables(jnp.arange(t, dtype=F32))
    cos_s, sin_s = _rope_tables(jnp.tile(past + jnp.arange(seq, dtype=F32), n_seq))

    n_c = nb + n_seq
    pad_c = (-n_c) % 8
    c_all = jnp.pad(jnp.concatenate([c_prompt, c_sample], axis=0), ((0, pad_c), (0, 0)))

    xp = x_prompt
    xs = x_sample.reshape(1, n_seq * seq, d)
    row = lambda a: a.reshape(1, -1)
    sp_l, kvp_l, ss_l, kvs_l = [], [], [], []
    for l in range(depth):
        w = _layer_weights(w_in[l], b_in[l], w_alpha2[l], b_alpha2[l], w_proj_a[l], w_proj_b[l],
                           w_out[l], w_up[l], w_down[l])
        mod = _ada(c_all, w_ada[l], b_ada[l])
        mod_p = [mod[:nb, i * d:(i + 1) * d].reshape(nb, 1, d) for i in range(6)]
        mod_s = [jnp.repeat(mod[nb:nb + n_seq, i * d:(i + 1) * d], seq, axis=0).reshape(1, n_seq * seq, d)
                 for i in range(6)]
        last = l == depth - 1

        sh1, sc1, g1, sh2, sc2, g2 = mod_p
        (gq, gk, gv, gr, la, qb, kb, vb, k32, v32, ga, gb) = _inproj(
            xp, sc1, sh1, row(norm1_g[l]), cos_p, sin_p, w, tm=256)
        oa, st = _gla_prompt(gq, gk, gv, la, gr, row(gla_norm_g[l]), tt=256)
        og, lg = zip(*[_dsw_prompt(qb, kb, vb, g, dil) for g, (_, dil) in enumerate(DSW_GROUPS)])
        x1, h2 = _merge(oa, og, lg, ga, gb, xp, g1, sc2, sh2, row(norm2_g[l]), w["wpa"], w["wpb"], w["wo"], tm=512)
        xp = _ffn(h2, x1, g2, row(normf_g), w["wu"], w["wd"], tm=512, final_norm=last)
        sp_l.append(jnp.swapaxes(st, 2, 3))
        kvp_l.append(tuple(_kv_stack(k32, v32, g, min(win, t)) for g, (win, _) in enumerate(DSW_GROUPS)))

        sh1, sc1, g1, sh2, sc2, g2 = mod_s
        (gq, gk, gv, gr, la, qb, kb, vb, k32, v32, ga, gb) = _inproj(
            xs, sc1, sh1, row(norm1_g[l]), cos_s, sin_s, w, tm=256)
        oa, s_new = _gla_sample(gq, gk, gv, la, gr, row(gla_norm_g[l]), state_gla[l], seq, bb=8)
        qf = qb.astype(F32)
        og, lg, new_kv = zip(*[_dsw_sample(qf, k32, v32, caches[g][l], g, dil, seq)
                               for g, (_, dil) in enumerate(DSW_GROUPS)])
        x1, h2 = _merge(oa, og, lg, ga, gb, xs, g1, sc2, sh2, row(norm2_g[l]), w["wpa"], w["wpb"], w["wo"], tm=256)
        xs = _ffn(h2, x1, g2, row(normf_g), w["wu"], w["wd"], tm=256, final_norm=last)
        ss_l.append(s_new)
        kvs_l.append(new_kv)

    y_prompt = xp
    y_sample = xs.reshape(n_seq, seq, d)
    stack = lambda items: jnp.stack(list(items))
    return (y_prompt, y_sample, stack(sp_l),
            stack(kv[0] for kv in kvp_l), stack(kv[1] for kv in kvp_l), stack(kv[2] for kv in kvp_l),
            stack(ss_l),
            stack(kv[0] for kv in kvs_l), stack(kv[1] for kv in kvs_l), stack(kv[2] for kv in kvs_l))
```

```python
import functools

import jax
import jax.numpy as jnp
from jax import lax
from jax.experimental import pallas as pl
from jax.experimental.pallas import tpu as pltpu

F32 = jnp.float32
BF16 = jnp.bfloat16

EPS = 1e-6
GLA_HEADS = 4
GLA_DK = 128
GLA_DV = 256
GLA_RANK = 16
GLA_TAU = 16.0
GLA_CHUNK = 64
GLA_SCALE = GLA_DK ** -0.5
DSW_GROUPS = ((128, 1), (512, 4), (2048, 16))
DSW_HEADS = 4
DSW_HEAD_DIM = 64
DSW_SCALE = DSW_HEAD_DIM ** -0.5
DSW_GW = DSW_HEADS * DSW_HEAD_DIM
BAND = 128
ROPE_THETA = 10000.0
PAST_LEN = 8192
LANES = 128
VMEM_LIMIT = 56 * 1024 * 1024


def _nn(a, b):
    return jnp.dot(a, b, preferred_element_type=F32)


def _nt(a, b):
    return lax.dot_general(a, b, (((1,), (1,)), ((), ())), preferred_element_type=F32)


def _tn(a, b):
    return lax.dot_general(a, b, (((0,), (0,)), ((), ())), preferred_element_type=F32)


def _split3(x):
    hi = x.astype(BF16)
    r1 = x - hi.astype(F32)
    mid = r1.astype(BF16)
    lo = (r1 - mid.astype(F32)).astype(BF16)
    return hi, mid, lo


def _iota(shape, dim):
    return lax.broadcasted_iota(jnp.int32, shape, dim)


def _rms(x, g):
    return x * lax.rsqrt(jnp.mean(x * x, axis=-1, keepdims=True) + EPS) * g


def _params(n_parallel, n_arbitrary=0):
    sem = ("parallel",) * n_parallel + ("arbitrary",) * n_arbitrary
    return pltpu.CompilerParams(dimension_semantics=sem, vmem_limit_bytes=VMEM_LIMIT)


def _resident(shape):
    nd = len(shape)
    return pl.BlockSpec(shape, lambda *_: (0,) * nd, pipeline_mode=pl.Buffered(1))


def _ada_kernel(c_ref, w_ref, b_ref, o_ref):
    c = c_ref[...]
    a = (c * jax.nn.sigmoid(c)).astype(BF16)
    o_ref[...] = _nn(a, w_ref[...].astype(BF16)) + b_ref[...]


def _ada(c_all, w_ada, b_ada):
    n, d = c_all.shape
    ncol = w_ada.shape[1]
    tn = 1536
    return pl.pallas_call(
        _ada_kernel,
        grid=(ncol // tn,),
        in_specs=[pl.BlockSpec((n, d), lambda j: (0, 0)),
                  pl.BlockSpec((d, tn), lambda j: (0, j)),
                  pl.BlockSpec((1, tn), lambda j: (0, j))],
        out_specs=pl.BlockSpec((n, tn), lambda j: (0, j)),
        out_shape=jax.ShapeDtypeStruct((n, ncol), F32),
        compiler_params=_params(1),
        name="ada_mod",
    )(c_all, w_ada, b_ada.reshape(1, ncol))


def _rope(x, cosf, sins, first_half):
    rot = jnp.where(first_half, pltpu.roll(x, LANES - 32, 1), pltpu.roll(x, 32, 1))
    return x * cosf + rot * sins


def _inproj_kernel(x_ref, sc_ref, sh_ref, g_ref, cos_ref, sin_ref,
                   wa_ref, ba_ref, wg_ref, bg_ref, w2_ref, b2_ref, wb_ref, bb_ref, wc_ref, bc_ref,
                   gq_ref, gk_ref, gv_ref, gr_ref, la_ref, ga_ref, gb_ref, k32_ref, v32_ref, *rest, fold):
    x = x_ref[...]
    h = (_rms(x, g_ref[...]) * (1.0 + sc_ref[...]) + sh_ref[...]).astype(BF16)

    gq_ref[...] = _nn(h, wa_ref[:, 0:512]) + ba_ref[:, 0:512]
    gk_ref[...] = _nn(h, wa_ref[:, 512:1024]) + ba_ref[:, 512:1024]
    gv_ref[...] = _nn(h, wa_ref[:, 1024:2048]) + ba_ref[:, 1024:2048]
    gr_ref[...] = _nn(h, wa_ref[:, 2048:3072]) + ba_ref[:, 2048:3072]

    glr = (_nn(h, wg_ref[...]) + bg_ref[...]).astype(BF16)
    z = _nn(glr, w2_ref[...]) + b2_ref[...]
    la_ref[...] = jax.nn.log_sigmoid(z) * (1.0 / GLA_TAU)

    cosf = cos_ref[...]
    sins = sin_ref[...]
    first_half = (_iota(cosf.shape, 1) % DSW_HEAD_DIM) < (DSW_HEAD_DIM // 2)
    tm = x.shape[0]
    per_group = DSW_GW // LANES
    width = 3 * DSW_GW

    def proj(off, c):
        cols = slice(off + c * LANES, off + (c + 1) * LANES)
        return _nn(h, wb_ref[:, cols]) + bb_ref[:, cols]

    for c in range(3 * per_group):
        cols = slice(c * LANES, (c + 1) * LANES)
        q = _rope(proj(0, c), cosf, sins, first_half) * DSW_SCALE
        k = _rope(proj(width, c), cosf, sins, first_half)
        v = proj(2 * width, c)
        k32_ref[:, cols] = k
        v32_ref[:, cols] = v
        if not fold:
            rest[0][:, cols] = q
            continue
        g, slab = divmod(c, per_group)
        dil = DSW_GROUPS[g][1]
        scratch = rest[-1]
        lanes = slice(slab * LANES, (slab + 1) * LANES)
        for which, val in enumerate((q, k, v)):
            out_ref = rest[3 * g + which]
            if dil == 1:
                out_ref[:, lanes] = val.astype(BF16)
            else:
                slot = 3 * c + which
                scratch[slot] = val
                for r in range(dil):
                    out_ref[r, :, lanes] = scratch[slot, pl.ds(r, tm // dil, stride=dil), :].astype(BF16)

    ga_ref[...] = _nn(h, wc_ref[:, 0:1024]) + bc_ref[:, 0:1024]
    gb_ref[...] = _nn(h, wc_ref[:, 1024:2048]) + bc_ref[:, 1024:2048]


def _mod_spec(arr, tm):
    if arr.shape[1] == 1:
        return pl.BlockSpec((None, 1, arr.shape[2]), lambda b, i: (b, 0, 0))
    return pl.BlockSpec((None, tm, arr.shape[2]), lambda b, i: (b, i, 0))


def _inproj(x, sc, sh, g, cos_t, sin_t, w, tm, fold):
    nb, t, d = x.shape
    tok = lambda n: pl.BlockSpec((None, tm, n), lambda b, i: (b, i, 0))
    out_cols = (512, 512, 1024, 1024, 512, 1024, 1024, 768, 768)
    out_specs = [tok(n) for n in out_cols]
    out_shape = [jax.ShapeDtypeStruct((nb, t, n), F32) for n in out_cols]
    scratch = []
    if fold:
        for _, dil in DSW_GROUPS:
            for _ in range(3):
                if dil == 1:
                    out_specs.append(tok(DSW_GW))
                    out_shape.append(jax.ShapeDtypeStruct((nb, t, DSW_GW), BF16))
                else:
                    out_specs.append(pl.BlockSpec((None, dil, tm // dil, DSW_GW), lambda b, i: (b, 0, i, 0)))
                    out_shape.append(jax.ShapeDtypeStruct((nb, dil, t // dil, DSW_GW), BF16))
        scratch = [pltpu.VMEM((9 * DSW_GW // LANES, tm, LANES), F32)]
    else:
        out_specs.append(tok(3 * DSW_GW))
        out_shape.append(jax.ShapeDtypeStruct((nb, t, 3 * DSW_GW), F32))
    weights = (w["wa"], w["ba"], w["wg"], w["bg"], w["w2"], w["b2"], w["wb"], w["bb"], w["wc"], w["bc"])
    return pl.pallas_call(
        functools.partial(_inproj_kernel, fold=fold),
        grid=(nb, t // tm),
        in_specs=[tok(d), _mod_spec(sc, tm), _mod_spec(sh, tm), _resident((1, d)),
                  pl.BlockSpec((tm, LANES), lambda b, i: (i, 0)),
                  pl.BlockSpec((tm, LANES), lambda b, i: (i, 0))]
                 + [_resident(a.shape) for a in weights],
        out_specs=out_specs,
        out_shape=out_shape,
        scratch_shapes=scratch,
        compiler_params=_params(2),
        name="inproj",
    )(x, sc, sh, g, cos_t, sin_t, *weights)


def _gla_local(gq_ref, gk_ref, la_ref, chunk):
    la = la_ref[...]
    tt = la.shape[0]
    r = _iota((tt, tt), 0)
    c = _iota((tt, tt), 1)
    same = (r // chunk) == (c // chunk)
    tri = jnp.where(same & (c <= r), 1.0, 0.0).astype(BF16)
    ones = jnp.where(same, 1.0, 0.0).astype(BF16)
    hi, mid, lo = _split3(la)
    b = _nn(tri, hi) + _nn(tri, mid) + _nn(tri, lo)
    bl = _nn(ones, hi) + _nn(ones, mid) + _nn(ones, lo)
    qg = (gq_ref[...] * GLA_SCALE * jnp.exp(b)).astype(BF16)
    kd = (gk_ref[...] * jnp.exp(-b)).astype(BF16)
    kl = (gk_ref[...] * jnp.exp(bl - b)).astype(BF16)
    causal = same & (c <= r)
    return qg, kd, kl, jnp.exp(bl), causal


def _gla_finish(o, gr, g):
    return (_rms(o, g) * (gr * jax.nn.sigmoid(gr))).astype(BF16)


def _gla_prompt_kernel(gq_ref, gk_ref, gv_ref, la_ref, gr_ref, g_ref, o_ref, st_ref):
    @pl.when(pl.program_id(1) == 0)
    def _():
        st_ref[...] = jnp.zeros_like(st_ref)

    qg, kd, kl, dec, causal = _gla_local(gq_ref, gk_ref, la_ref, GLA_CHUNK)
    tt = qg.shape[0]
    for h in range(GLA_HEADS):
        kc = slice(h * GLA_DK, (h + 1) * GLA_DK)
        vc = slice(h * GLA_DV, (h + 1) * GLA_DV)
        v = gv_ref[:, vc].astype(BF16)
        att = jnp.where(causal, _nt(qg[:, kc], kd[:, kc]), 0.0).astype(BF16)
        intra = _nn(att, v)
        st = st_ref[h]
        inter = []
        for ci in range(tt // GLA_CHUNK):
            rows = slice(ci * GLA_CHUNK, (ci + 1) * GLA_CHUNK)
            inter.append(_nt(qg[rows, kc], st.astype(BF16)))
            st = dec[ci * GLA_CHUNK:ci * GLA_CHUNK + 1, kc] * st + _tn(v[rows], kl[rows, kc])
        st_ref[h] = st
        o = intra + jnp.concatenate(inter, axis=0)
        o_ref[:, vc] = _gla_finish(o, gr_ref[:, vc], g_ref[...])


def _gla_prompt(gq, gk, gv, la, gr, g, tt):
    nb, t, _ = gq.shape
    tok = lambda n: pl.BlockSpec((None, tt, n), lambda b, i: (b, i, 0))
    return pl.pallas_call(
        _gla_prompt_kernel,
        grid=(nb, t // tt),
        in_specs=[tok(512), tok(512), tok(1024), tok(512), tok(1024), _resident((1, GLA_DV))],
        out_specs=[tok(1024),
                   pl.BlockSpec((None, GLA_HEADS, GLA_DV, GLA_DK), lambda b, i: (b, 0, 0, 0))],
        out_shape=[jax.ShapeDtypeStruct((nb, t, 1024), BF16),
                   jax.ShapeDtypeStruct((nb, GLA_HEADS, GLA_DV, GLA_DK), F32)],
        compiler_params=_params(1, 1),
        name="gla_prompt",
    )(gq, gk, gv, la, gr, g)


def _gla_sample_kernel(gq_ref, gk_ref, gv_ref, la_ref, gr_ref, g_ref, s_ref, o_ref, so_ref, *, seq):
    qg, kd, kl, dec, causal = _gla_local(gq_ref, gk_ref, la_ref, seq)
    rows_total = qg.shape[0]
    per8 = 8 // seq
    row8 = _iota((8, 1), 0)
    for h in range(GLA_HEADS):
        kc = slice(h * GLA_DK, (h + 1) * GLA_DK)
        vc = slice(h * GLA_DV, (h + 1) * GLA_DV)
        v = gv_ref[:, vc].astype(BF16)
        att = jnp.where(causal, _nt(qg[:, kc], kd[:, kc]), 0.0).astype(BF16)
        intra = _nn(att, v)
        inter = []
        for p in range(rows_total // 8):
            rows = slice(p * 8, (p + 1) * 8)
            d_hi, d_mid, d_lo = _split3(dec[rows, kc])
            inter_p = jnp.zeros((8, GLA_DV), F32)
            for j in range(per8):
                b = p * per8 + j
                r0 = j * seq
                s0 = s_ref[b, h]
                mine = (row8 >= r0) & (row8 < r0 + seq)
                inter_p = jnp.where(mine, _nn(qg[rows, kc], s0.astype(BF16)), inter_p)
                dl = jnp.where(row8 == r0, d_hi, jnp.where(row8 == r0 + 1, d_mid,
                               jnp.where(row8 == r0 + 2, d_lo, jnp.zeros_like(d_lo))))
                e = jnp.where((row8 >= r0) & (row8 < r0 + 3), 1.0, 0.0).astype(BF16)
                dec_b = _tn(dl, jnp.broadcast_to(e, (8, GLA_DV)))
                upd = _tn(jnp.where(mine, kl[rows, kc], jnp.zeros_like(kl[rows, kc])), v[rows])
                so_ref[b, h] = dec_b * s0 + upd
            inter.append(inter_p)
        o = intra + jnp.concatenate(inter, axis=0)
        o_ref[:, vc] = _gla_finish(o, gr_ref[:, vc], g_ref[...])


def _gla_sample(gq, gk, gv, la, gr, g, s0, seq, bb):
    n_seq = s0.shape[0]
    rows = bb * seq
    tok = lambda n: pl.BlockSpec((None, rows, n), lambda i: (0, i, 0))
    st = pl.BlockSpec((bb, GLA_HEADS, GLA_DK, GLA_DV), lambda i: (i, 0, 0, 0))
    return pl.pallas_call(
        functools.partial(_gla_sample_kernel, seq=seq),
        grid=(n_seq // bb,),
        in_specs=[tok(512), tok(512), tok(1024), tok(512), tok(1024), _resident((1, GLA_DV)), st],
        out_specs=[tok(1024), st],
        out_shape=[jax.ShapeDtypeStruct((1, n_seq * seq, 1024), BF16),
                   jax.ShapeDtypeStruct(s0.shape, F32)],
        compiler_params=_params(1),
        name="gla_sample",
    )(gq, gk, gv, la, gr, g, s0)


def _dsw_prompt_kernel(q_ref, kp_ref, kc_ref, vp_ref, vc_ref, o_ref, lse_ref, *, dil):
    qb = q_ref.shape[0]
    res = pl.program_id(2)
    first_key = jnp.where(pl.program_id(1) == 0, BAND, 0)
    qi = _iota((BAND, 2 * BAND), 0) + BAND
    ki = _iota((BAND, 2 * BAND), 1)
    band = (qi - ki >= 0) & (qi - ki <= BAND)
    lane = _iota((BAND, LANES), 1)
    for s in range(qb // BAND):
        rows = slice(s * BAND, (s + 1) * BAND)
        if s == 0:
            valid = band & (ki >= first_key)
        else:
            valid = band
        if dil == 1:
            tok_rows = rows
        else:
            tok_rows = pl.ds(s * BAND * dil + res, BAND, stride=dil)
        for hp in range(DSW_GW // LANES):
            cols = slice(hp * LANES, (hp + 1) * LANES)
            qp = q_ref[rows, cols]
            if s == 0:
                kcat = jnp.concatenate([kp_ref[:, cols], kc_ref[0:BAND, cols]], axis=0)
                vcat = jnp.concatenate([vp_ref[:, cols], vc_ref[0:BAND, cols]], axis=0)
            else:
                kcat = kc_ref[(s - 1) * BAND:(s + 1) * BAND, cols]
                vcat = vc_ref[(s - 1) * BAND:(s + 1) * BAND, cols]
            outs, lses = [], []
            for hh in range(LANES // DSW_HEAD_DIM):
                in_head = (lane // DSW_HEAD_DIM) == hh
                sc = _nt(jnp.where(in_head, qp, jnp.zeros_like(qp)), kcat)
                sc = jnp.where(valid, sc, -jnp.inf)
                m = jnp.max(sc, axis=-1, keepdims=True)
                e = jnp.exp(sc - m)
                den = jnp.sum(e, axis=-1, keepdims=True)
                outs.append(_nn((e / den).astype(BF16), vcat))
                lses.append(m + jnp.log(den))
            first = lane < DSW_HEAD_DIM
            o_ref[hp, tok_rows, :] = jnp.where(first, outs[0], outs[1])
            lse_ref[hp, tok_rows, :] = jnp.where(first, lses[0], jnp.broadcast_to(lses[1], (BAND, LANES)))


def _dsw_prompt(q, k, v, dil):
    nb = q.shape[0]
    seq_len = q.shape[-2]
    t = seq_len * dil
    tq = min(512, seq_len)
    sub = tq // BAND
    if dil == 1:
        cur = pl.BlockSpec((None, tq, DSW_GW), lambda b, j, r: (b, j, 0))
        prev = pl.BlockSpec((None, BAND, DSW_GW), lambda b, j, r: (b, jnp.maximum(j * sub - 1, 0), 0))
    else:
        cur = pl.BlockSpec((None, None, tq, DSW_GW), lambda b, j, r: (b, r, j, 0))
        prev = pl.BlockSpec((None, None, BAND, DSW_GW), lambda b, j, r: (b, r, jnp.maximum(j * sub - 1, 0), 0))
    n_slab = DSW_GW // LANES
    out = pl.BlockSpec((None, n_slab, tq * dil, LANES), lambda b, j, r: (b, 0, j, 0))
    return pl.pallas_call(
        functools.partial(_dsw_prompt_kernel, dil=dil),
        grid=(nb, seq_len // tq, dil),
        in_specs=[cur, prev, cur, prev, cur],
        out_specs=[out, out],
        out_shape=[jax.ShapeDtypeStruct((nb, n_slab, t, LANES), F32)] * 2,
        compiler_params=_params(2, 1),
        name=f"dsw_prompt_d{dil}",
    )(q, k, k, v, v)


def _dsw_sample_kernel(q_ref, kn_ref, vn_ref, c_ref, o_ref, lse_ref, co_ref, *, seq, dil):
    per8 = 8 // seq
    win = c_ref.shape[2]
    n_rows = 2 * 8
    lane = _iota((8, LANES), 1)
    row8 = _iota((8, 1), 0)
    r = _iota((n_rows, 1), 0)
    r_step = r % seq
    r_seq = (r % 8) // seq
    key = _iota((n_rows, win), 1)
    cache_ok = ((key % dil) == (r_step % dil)) & (key >= r_step)
    c128 = _iota((n_rows, LANES), 1)
    new_ok = ((c128 < 8) & ((c128 // seq) == r_seq) & ((c128 % seq) <= r_step)
              & (((r_step - c128 % seq) % dil) == 0))
    pad = jnp.zeros((LANES - 8, LANES), BF16)
    first = lane < DSW_HEAD_DIM

    for hp in range(DSW_GW // LANES):
        lanes = slice(hp * LANES, (hp + 1) * LANES)
        q8 = q_ref[:, lanes]
        qrows = jnp.concatenate([jnp.where(first, q8, 0.0), jnp.where(first, 0.0, q8)], axis=0).astype(BF16)
        kn_t = jnp.concatenate([kn_ref[:, lanes].astype(BF16), pad], axis=0)
        vn_t = jnp.concatenate([vn_ref[:, lanes].astype(BF16), pad], axis=0)
        scn = jnp.where(new_ok, _nt(qrows, kn_t), -jnp.inf)
        m_new = jnp.max(scn, axis=-1, keepdims=True)
        o_p = jnp.zeros((8, LANES), F32)
        l_p = jnp.zeros((8, LANES), F32)
        for j in range(per8):
            kt = c_ref[j, hp * LANES:(hp + 1) * LANES, :].astype(BF16)
            vt = c_ref[j, DSW_GW + hp * LANES:DSW_GW + (hp + 1) * LANES, :].astype(BF16)
            sc = jnp.where(cache_ok, _nn(qrows, kt), -jnp.inf)
            m = jnp.maximum(jnp.max(sc, axis=-1, keepdims=True), m_new)
            e = jnp.exp(sc - m)
            en = jnp.exp(scn - m)
            den = jnp.sum(e, axis=-1, keepdims=True) + jnp.sum(en, axis=-1, keepdims=True)
            o = _nt((e / den).astype(BF16), vt) + _nn((en / den).astype(BF16), vn_t)
            lse = jnp.broadcast_to(m + jnp.log(den), (n_rows, LANES))
            mine = (row8 // seq) == j
            o_p = jnp.where(mine, jnp.where(first, o[0:8], o[8:16]), o_p)
            l_p = jnp.where(mine, jnp.where(first, lse[0:8], lse[8:16]), l_p)
        o_ref[:, lanes] = o_p
        lse_ref[:, lanes] = l_p

    lane_sq = _iota((LANES, LANES), 1)
    p_row = _iota((8, LANES), 0)
    for j in range(per8):
        place = jnp.where(((p_row // seq) == j) & (lane == LANES - seq + p_row % seq), 1.0, 0.0).astype(BF16)
        for blk in range(2 * DSW_GW // LANES):
            rows = slice(blk * LANES, (blk + 1) * LANES)
            src = kn_ref if blk < DSW_GW // LANES else vn_ref
            blk_lanes = slice((blk % (DSW_GW // LANES)) * LANES, (blk % (DSW_GW // LANES) + 1) * LANES)
            hi, mid, lo = _split3(src[:, blk_lanes])
            new_cols = _tn(hi, place) + _tn(mid, place) + _tn(lo, place)
            rolled = pltpu.roll(c_ref[j, rows, :], win - seq, 1)
            if win > LANES:
                co_ref[j, rows, 0:win - LANES] = rolled[:, 0:win - LANES]
            co_ref[j, rows, win - LANES:win] = jnp.where(lane_sq < LANES - seq, rolled[:, win - LANES:win], new_cols)


def _dsw_sample(q32, k32, v32, cache, g, dil, seq):
    n_seq, win = cache.shape[0], cache.shape[1]
    per8 = 8 // seq
    view = jnp.transpose(cache, (0, 2, 3, 4, 1)).reshape(n_seq, 2 * DSW_GW, win)
    tok = pl.BlockSpec((None, 8, DSW_GW), lambda i: (0, i, g))
    tok_out = pl.BlockSpec((None, 8, DSW_GW), lambda i: (0, i, 0))
    cspec = pl.BlockSpec((per8, 2 * DSW_GW, win), lambda i: (i, 0, 0))
    o, lse, new = pl.pallas_call(
        functools.partial(_dsw_sample_kernel, seq=seq, dil=dil),
        grid=(n_seq // per8,),
        in_specs=[tok, tok, tok, cspec],
        out_specs=[tok_out, tok_out, cspec],
        out_shape=[jax.ShapeDtypeStruct((1, n_seq * seq, DSW_GW), F32)] * 2
                  + [jax.ShapeDtypeStruct(view.shape, F32)],
        compiler_params=_params(1),
        name=f"dsw_sample_d{dil}",
    )(q32, k32, v32, view)
    new = jnp.transpose(new.reshape(n_seq, 2, DSW_HEADS, DSW_HEAD_DIM, win), (0, 4, 1, 2, 3))
    n_slab = DSW_GW // LANES
    slabs = lambda a: jnp.transpose(a.reshape(1, n_seq * seq, n_slab, LANES), (0, 2, 1, 3))
    return slabs(o), slabs(lse), new


def _merge_kernel(oa_ref, o0_ref, o1_ref, o2_ref, l0_ref, l1_ref, l2_ref, ga_ref, gb_ref, x_ref,
                  g1_ref, sc_ref, sh_ref, n2_ref, wpa_ref, wpb_ref, wo_ref, x1_ref, h2_ref):
    ob = []
    for slab in range(DSW_GW // LANES):
        l0, l1, l2 = l0_ref[slab], l1_ref[slab], l2_ref[slab]
        m = jnp.maximum(jnp.maximum(l0, l1), l2)
        w0, w1, w2 = jnp.exp(l0 - m), jnp.exp(l1 - m), jnp.exp(l2 - m)
        den = w0 + w1 + w2
        ob.append((w0 / den) * o0_ref[slab] + (w1 / den) * o1_ref[slab] + (w2 / den) * o2_ref[slab])
    ob = jnp.concatenate(ob, axis=1).astype(BF16)
    merged = (jax.nn.sigmoid(ga_ref[...]) * _nn(oa_ref[...], wpa_ref[...])
              + jax.nn.sigmoid(gb_ref[...]) * _nn(ob, wpb_ref[...]))
    x1 = x_ref[...] + g1_ref[...] * _nn(merged.astype(BF16), wo_ref[...])
    x1_ref[...] = x1
    h2_ref[...] = (_rms(x1, n2_ref[...]) * (1.0 + sc_ref[...]) + sh_ref[...]).astype(BF16)


def _merge(oa, og, lg, ga, gb, x, g1, sc2, sh2, n2, wpa, wpb, wo, tm):
    nb, t, d = x.shape
    tok = lambda n: pl.BlockSpec((None, tm, n), lambda b, i: (b, i, 0))
    slab = pl.BlockSpec((None, DSW_GW // LANES, tm, LANES), lambda b, i: (b, 0, i, 0))
    return pl.pallas_call(
        _merge_kernel,
        grid=(nb, t // tm),
        in_specs=[tok(1024)] + [slab] * 6 + [tok(d), tok(d), tok(d),
                  _mod_spec(g1, tm), _mod_spec(sc2, tm), _mod_spec(sh2, tm), _resident((1, d)),
                  _resident(wpa.shape), _resident(wpb.shape), _resident(wo.shape)],
        out_specs=[tok(d), tok(d)],
        out_shape=[jax.ShapeDtypeStruct((nb, t, d), F32), jax.ShapeDtypeStruct((nb, t, d), BF16)],
        compiler_params=_params(2),
        name="merge_outproj",
    )(oa, *og, *lg, ga, gb, x, g1, sc2, sh2, n2, wpa, wpb, wo)


def _ffn_kernel(h_ref, x_ref, g2_ref, nf_ref, wu_ref, wd_ref, y_ref, *, final_norm, n_split):
    h = h_ref[...]
    d_ff = wd_ref.shape[0]
    step = d_ff // n_split
    acc = None
    for j in range(n_split):
        u1 = _nn(h, wu_ref[:, j * step:(j + 1) * step])
        u2 = _nn(h, wu_ref[:, d_ff + j * step:d_ff + (j + 1) * step])
        a = (u1 * jax.nn.sigmoid(u1) * u2).astype(BF16)
        part = _nn(a, wd_ref[j * step:(j + 1) * step, :])
        acc = part if acc is None else acc + part
    x2 = x_ref[...] + g2_ref[...] * acc
    y_ref[...] = _rms(x2, nf_ref[...]) if final_norm else x2


def _ffn(h2, x1, g2, nf, wu, wd, tm, final_norm):
    nb, t, d = x1.shape
    tok = lambda n: pl.BlockSpec((None, tm, n), lambda b, i: (b, i, 0))
    return pl.pallas_call(
        functools.partial(_ffn_kernel, final_norm=final_norm, n_split=2),
        grid=(nb, t // tm),
        in_specs=[tok(d), tok(d), _mod_spec(g2, tm), _resident((1, d)),
                  _resident(wu.shape), _resident(wd.shape)],
        out_specs=tok(d),
        out_shape=jax.ShapeDtypeStruct((nb, t, d), F32),
        compiler_params=_params(2),
        name="ffn",
    )(h2, x1, g2, nf, wu, wd)


def _rope_tables(pos):
    half = DSW_HEAD_DIM // 2
    inv = ROPE_THETA ** (-jnp.arange(half, dtype=F32) / half)
    ang = pos[:, None] * inv[None, :]
    cos, sin = jnp.cos(ang), jnp.sin(ang)
    reps = LANES // half
    cosf = jnp.tile(cos, (1, reps))
    sign = jnp.tile(jnp.concatenate([-jnp.ones((half,), F32), jnp.ones((half,), F32)]), LANES // DSW_HEAD_DIM)
    return cosf, jnp.tile(sin, (1, reps)) * sign[None, :]


def _layer_weights(w_in, b_in, w_alpha2, b_alpha2, w_proj_a, w_proj_b, w_out, w_up, w_down):
    bf = lambda a: a.astype(BF16)
    row = lambda a: a.reshape(1, -1)
    o_glr, o_dq, o_ga = 3072, 3088, 5392
    pad_r = LANES - GLA_RANK
    return dict(
        wa=bf(w_in[:, :o_glr]), ba=row(b_in[:o_glr]),
        wg=bf(jnp.pad(w_in[:, o_glr:o_dq], ((0, 0), (0, pad_r)))), bg=row(jnp.pad(b_in[o_glr:o_dq], (0, pad_r))),
        w2=bf(jnp.pad(w_alpha2, ((0, pad_r), (0, 0)))), b2=row(b_alpha2),
        wb=bf(w_in[:, o_dq:o_ga]), bb=row(b_in[o_dq:o_ga]),
        wc=bf(w_in[:, o_ga:]), bc=row(b_in[o_ga:]),
        wpa=bf(w_proj_a), wpb=bf(w_proj_b), wo=bf(w_out), wu=bf(w_up), wd=bf(w_down))


def _kv_stack(k32, v32, g, keep):
    nb, t, _ = k32.shape
    cols = slice(g * DSW_GW, (g + 1) * DSW_GW)
    k = k32[:, t - keep:, cols].reshape(nb, keep, DSW_HEADS, DSW_HEAD_DIM)
    v = v32[:, t - keep:, cols].reshape(nb, keep, DSW_HEADS, DSW_HEAD_DIM)
    return jnp.stack([k, v], axis=2)


def kernel(x_prompt, x_sample, state_gla, cache_kv_w128, cache_kv_w512, cache_kv_w2048, c_prompt, c_sample,
           norm1_g, norm2_g, w_ada, b_ada, w_in, b_in, w_alpha2, b_alpha2, gla_norm_g, w_proj_a, w_proj_b,
           w_out, w_up, w_down, normf_g):
    depth = w_ada.shape[0]
    nb, t, d = x_prompt.shape
    n_seq, seq, _ = x_sample.shape
    assert 8 % seq == 0 and seq >= 3, "sample kernels pack whole sequences into 8-row groups"
    past = PAST_LEN
    caches = (cache_kv_w128, cache_kv_w512, cache_kv_w2048)

    cos_p, sin_p = _rope_tables(jnp.arange(t, dtype=F32))
    cos_s, sin_s = _rope_tables(jnp.tile(past + jnp.arange(seq, dtype=F32), n_seq))

    n_c = nb + n_seq
    pad_c = (-n_c) % 8
    c_all = jnp.pad(jnp.concatenate([c_prompt, c_sample], axis=0), ((0, pad_c), (0, 0)))

    xp = x_prompt
    xs = x_sample.reshape(1, n_seq * seq, d)
    row = lambda a: a.reshape(1, -1)
    sp_l, kvp_l, ss_l, kvs_l = [], [], [], []
    for l in range(depth):
        w = _layer_weights(w_in[l], b_in[l], w_alpha2[l], b_alpha2[l], w_proj_a[l], w_proj_b[l],
                           w_out[l], w_up[l], w_down[l])
        mod = _ada(c_all, w_ada[l], b_ada[l])
        mod_p = [mod[:nb, i * d:(i + 1) * d].reshape(nb, 1, d) for i in range(6)]
        mod_s = [jnp.repeat(mod[nb:nb + n_seq, i * d:(i + 1) * d], seq, axis=0).reshape(1, n_seq * seq, d)
                 for i in range(6)]
        last = l == depth - 1

        sh1, sc1, g1, sh2, sc2, g2 = mod_p
        gq, gk, gv, gr, la, ga, gb, k32, v32, *qkv = _inproj(
            xp, sc1, sh1, row(norm1_g[l]), cos_p, sin_p, w, tm=256, fold=True)
        oa, st = _gla_prompt(gq, gk, gv, la, gr, row(gla_norm_g[l]), tt=256)
        og, lg = zip(*[_dsw_prompt(*qkv[3 * g:3 * g + 3], dil) for g, (_, dil) in enumerate(DSW_GROUPS)])
        x1, h2 = _merge(oa, og, lg, ga, gb, xp, g1, sc2, sh2, row(norm2_g[l]), w["wpa"], w["wpb"], w["wo"], tm=512)
        xp = _ffn(h2, x1, g2, row(normf_g), w["wu"], w["wd"], tm=512, final_norm=last)
        sp_l.append(jnp.swapaxes(st, 2, 3))
        kvp_l.append(tuple(_kv_stack(k32, v32, g, min(win, t)) for g, (win, _) in enumerate(DSW_GROUPS)))

        sh1, sc1, g1, sh2, sc2, g2 = mod_s
        gq, gk, gv, gr, la, ga, gb, k32, v32, q32 = _inproj(
            xs, sc1, sh1, row(norm1_g[l]), cos_s, sin_s, w, tm=256, fold=False)
        oa, s_new = _gla_sample(gq, gk, gv, la, gr, row(gla_norm_g[l]), state_gla[l], seq, bb=8)
        og, lg, new_kv = zip(*[_dsw_sample(q32, k32, v32, caches[g][l], g, dil, seq)
                               for g, (_, dil) in enumerate(DSW_GROUPS)])
        x1, h2 = _merge(oa, og, lg, ga, gb, xs, g1, sc2, sh2, row(norm2_g[l]), w["wpa"], w["wpb"], w["wo"], tm=256)
        xs = _ffn(h2, x1, g2, row(normf_g), w["wu"], w["wd"], tm=256, final_norm=last)
        ss_l.append(s_new)
        kvs_l.append(new_kv)

    y_prompt = xp
    y_sample = xs.reshape(n_seq, seq, d)
    stack = lambda items: jnp.stack(list(items))
    return (y_prompt, y_sample, stack(sp_l),
            stack(kv[0] for kv in kvp_l), stack(kv[1] for kv in kvp_l), stack(kv[2] for kv in kvp_l),
            stack(ss_l),
            stack(kv[0] for kv in kvs_l), stack(kv[1] for kv in kvs_l), stack(kv[2] for kv in kvs_l))
```

```python
import functools

import jax
import jax.numpy as jnp
from jax import lax
from jax.experimental import pallas as pl
from jax.experimental.pallas import tpu as pltpu

F32 = jnp.float32
BF16 = jnp.bfloat16

EPS = 1e-6
GLA_HEADS = 4
GLA_DK = 128
GLA_DV = 256
GLA_RANK = 16
GLA_TAU = 16.0
GLA_CHUNK = 64
GLA_SCALE = GLA_DK ** -0.5
DSW_GROUPS = ((128, 1), (512, 4), (2048, 16))
DSW_HEADS = 4
DSW_HEAD_DIM = 64
DSW_SCALE = DSW_HEAD_DIM ** -0.5
DSW_GW = DSW_HEADS * DSW_HEAD_DIM
BAND = 128
ROPE_THETA = 10000.0
PAST_LEN = 8192
LANES = 128
VMEM_LIMIT = 56 * 1024 * 1024


def _nn(a, b):
    return jnp.dot(a, b, preferred_element_type=F32)


def _nt(a, b):
    return lax.dot_general(a, b, (((1,), (1,)), ((), ())), preferred_element_type=F32)


def _tn(a, b):
    return lax.dot_general(a, b, (((0,), (0,)), ((), ())), preferred_element_type=F32)


def _split3(x):
    hi = x.astype(BF16)
    r1 = x - hi.astype(F32)
    mid = r1.astype(BF16)
    lo = (r1 - mid.astype(F32)).astype(BF16)
    return hi, mid, lo


def _iota(shape, dim):
    return lax.broadcasted_iota(jnp.int32, shape, dim)


def _rms(x, g):
    return x * lax.rsqrt(jnp.mean(x * x, axis=-1, keepdims=True) + EPS) * g


def _params(n_parallel, n_arbitrary=0):
    sem = ("parallel",) * n_parallel + ("arbitrary",) * n_arbitrary
    return pltpu.CompilerParams(dimension_semantics=sem, vmem_limit_bytes=VMEM_LIMIT)


def _resident(shape):
    nd = len(shape)
    return pl.BlockSpec(shape, lambda *_: (0,) * nd, pipeline_mode=pl.Buffered(1))


def _ada_kernel(c_ref, w_ref, b_ref, o_ref):
    c = c_ref[...]
    a = (c * jax.nn.sigmoid(c)).astype(BF16)
    o_ref[...] = _nn(a, w_ref[...].astype(BF16)) + b_ref[...]


def _ada(c_all, w_ada, b_ada):
    n, d = c_all.shape
    ncol = w_ada.shape[1]
    tn = 1536
    return pl.pallas_call(
        _ada_kernel,
        grid=(ncol // tn,),
        in_specs=[pl.BlockSpec((n, d), lambda j: (0, 0)),
                  pl.BlockSpec((d, tn), lambda j: (0, j)),
                  pl.BlockSpec((1, tn), lambda j: (0, j))],
        out_specs=pl.BlockSpec((n, tn), lambda j: (0, j)),
        out_shape=jax.ShapeDtypeStruct((n, ncol), F32),
        compiler_params=_params(1),
        name="ada_mod",
    )(c_all, w_ada, b_ada.reshape(1, ncol))


def _rope(x, cosf, sins, first_half):
    rot = jnp.where(first_half, pltpu.roll(x, LANES - 32, 1), pltpu.roll(x, 32, 1))
    return x * cosf + rot * sins


def _inproj_kernel(x_ref, sc_ref, sh_ref, g_ref, cos_ref, sin_ref,
                   wa_ref, ba_ref, wg_ref, bg_ref, w2_ref, b2_ref, wb_ref, bb_ref, wc_ref, bc_ref,
                   gq_ref, gk_ref, gv_ref, gr_ref, la_ref, ga_ref, gb_ref, k32_ref, v32_ref, *rest, fold):
    x = x_ref[...]
    h = (_rms(x, g_ref[...]) * (1.0 + sc_ref[...]) + sh_ref[...]).astype(BF16)

    gq_ref[...] = (_nn(h, wa_ref[:, 0:512]) + ba_ref[:, 0:512]).astype(gq_ref.dtype)
    gk_ref[...] = (_nn(h, wa_ref[:, 512:1024]) + ba_ref[:, 512:1024]).astype(gk_ref.dtype)
    gv_ref[...] = (_nn(h, wa_ref[:, 1024:2048]) + ba_ref[:, 1024:2048]).astype(gv_ref.dtype)
    gr_ref[...] = (_nn(h, wa_ref[:, 2048:3072]) + ba_ref[:, 2048:3072]).astype(gr_ref.dtype)

    glr = (_nn(h, wg_ref[...]) + bg_ref[...]).astype(BF16)
    z = _nn(glr, w2_ref[...]) + b2_ref[...]
    la_ref[...] = jax.nn.log_sigmoid(z) * (1.0 / GLA_TAU)

    cosf = cos_ref[...]
    sins = sin_ref[...]
    first_half = (_iota(cosf.shape, 1) % DSW_HEAD_DIM) < (DSW_HEAD_DIM // 2)
    tm = x.shape[0]
    per_group = DSW_GW // LANES
    width = 3 * DSW_GW

    def proj(off, g):
        cols = slice(off + g * DSW_GW, off + (g + 1) * DSW_GW)
        full = _nn(h, wb_ref[:, cols]) + bb_ref[:, cols]
        return [full[:, s * LANES:(s + 1) * LANES] for s in range(per_group)]

    for g, (_, dil) in enumerate(DSW_GROUPS):
        qs = [_rope(a, cosf, sins, first_half) * DSW_SCALE for a in proj(0, g)]
        ks = [_rope(a, cosf, sins, first_half) for a in proj(width, g)]
        vs = proj(2 * width, g)
        for slab in range(per_group):
            cols = slice(g * DSW_GW + slab * LANES, g * DSW_GW + (slab + 1) * LANES)
            lanes = slice(slab * LANES, (slab + 1) * LANES)
            k32_ref[:, cols] = ks[slab]
            v32_ref[:, cols] = vs[slab]
            if not fold:
                rest[0][:, cols] = qs[slab]
                continue
            scratch = rest[-1]
            for which, val in enumerate((qs[slab], ks[slab], vs[slab])):
                out_ref = rest[3 * g + which]
                if dil == 1:
                    out_ref[:, lanes] = val.astype(BF16)
                else:
                    scratch[which] = val
                    for r in range(dil):
                        out_ref[r, :, lanes] = scratch[which, pl.ds(r, tm // dil, stride=dil), :].astype(BF16)

    ga_ref[...] = (_nn(h, wc_ref[:, 0:1024]) + bc_ref[:, 0:1024]).astype(ga_ref.dtype)
    gb_ref[...] = (_nn(h, wc_ref[:, 1024:2048]) + bc_ref[:, 1024:2048]).astype(gb_ref.dtype)


def _mod_spec(arr, tm):
    if arr.shape[1] == 1:
        return pl.BlockSpec((None, 1, arr.shape[2]), lambda b, i: (b, 0, 0))
    return pl.BlockSpec((None, tm, arr.shape[2]), lambda b, i: (b, i, 0))


def _inproj(x, sc, sh, g, cos_t, sin_t, w, tm, fold):
    nb, t, d = x.shape
    tok = lambda n: pl.BlockSpec((None, tm, n), lambda b, i: (b, i, 0))
    out_cols = (512, 512, 1024, 1024, 512, 1024, 1024, 768, 768)
    out_dt = (BF16, BF16, BF16, BF16, F32, BF16, BF16, F32, F32)
    out_specs = [tok(n) for n in out_cols]
    out_shape = [jax.ShapeDtypeStruct((nb, t, n), dt) for n, dt in zip(out_cols, out_dt)]
    scratch = []
    if fold:
        for _, dil in DSW_GROUPS:
            for _ in range(3):
                if dil == 1:
                    out_specs.append(tok(DSW_GW))
                    out_shape.append(jax.ShapeDtypeStruct((nb, t, DSW_GW), BF16))
                else:
                    out_specs.append(pl.BlockSpec((None, dil, tm // dil, DSW_GW), lambda b, i: (b, 0, i, 0)))
                    out_shape.append(jax.ShapeDtypeStruct((nb, dil, t // dil, DSW_GW), BF16))
        scratch = [pltpu.VMEM((3, tm, LANES), F32)]
    else:
        out_specs.append(tok(3 * DSW_GW))
        out_shape.append(jax.ShapeDtypeStruct((nb, t, 3 * DSW_GW), F32))
    weights = (w["wa"], w["ba"], w["wg"], w["bg"], w["w2"], w["b2"], w["wb"], w["bb"], w["wc"], w["bc"])
    return pl.pallas_call(
        functools.partial(_inproj_kernel, fold=fold),
        grid=(nb, t // tm),
        in_specs=[tok(d), _mod_spec(sc, tm), _mod_spec(sh, tm), _resident((1, d)),
                  pl.BlockSpec((tm, LANES), lambda b, i: (i, 0)),
                  pl.BlockSpec((tm, LANES), lambda b, i: (i, 0))]
                 + [_resident(a.shape) for a in weights],
        out_specs=out_specs,
        out_shape=out_shape,
        scratch_shapes=scratch,
        compiler_params=_params(2),
        name="inproj",
    )(x, sc, sh, g, cos_t, sin_t, *weights)


def _gla_local(gq_ref, gk_ref, la_ref, chunk):
    la = la_ref[...]
    tt = la.shape[0]
    r = _iota((tt, tt), 0)
    c = _iota((tt, tt), 1)
    same = (r // chunk) == (c // chunk)
    tri = jnp.where(same & (c <= r), 1.0, 0.0).astype(BF16)
    ones = jnp.where(same, 1.0, 0.0).astype(BF16)
    hi, mid, lo = _split3(la)
    b = _nn(tri, hi) + _nn(tri, mid) + _nn(tri, lo)
    bl = _nn(ones, hi) + _nn(ones, mid) + _nn(ones, lo)
    gq = gq_ref[...].astype(F32)
    gk = gk_ref[...].astype(F32)
    qg = (gq * GLA_SCALE * jnp.exp(b)).astype(BF16)
    kd = (gk * jnp.exp(-b)).astype(BF16)
    kl = (gk * jnp.exp(bl - b)).astype(BF16)
    causal = same & (c <= r)
    return qg, kd, kl, jnp.exp(bl), causal


def _gla_finish(o, gr, g):
    return (_rms(o, g) * (gr * jax.nn.sigmoid(gr))).astype(BF16)


def _gla_prompt_kernel(gq_ref, gk_ref, gv_ref, la_ref, gr_ref, g_ref, o_ref, st_ref):
    @pl.when(pl.program_id(1) == 0)
    def _():
        st_ref[...] = jnp.zeros_like(st_ref)

    qg, kd, kl, dec, causal = _gla_local(gq_ref, gk_ref, la_ref, GLA_CHUNK)
    tt = qg.shape[0]
    for h in range(GLA_HEADS):
        kc = slice(h * GLA_DK, (h + 1) * GLA_DK)
        vc = slice(h * GLA_DV, (h + 1) * GLA_DV)
        v = gv_ref[:, vc].astype(BF16)
        att = jnp.where(causal, _nt(qg[:, kc], kd[:, kc]), 0.0).astype(BF16)
        intra = _nn(att, v)
        st = st_ref[h]
        inter = []
        for ci in range(tt // GLA_CHUNK):
            rows = slice(ci * GLA_CHUNK, (ci + 1) * GLA_CHUNK)
            inter.append(_nt(qg[rows, kc], st.astype(BF16)))
            st = dec[ci * GLA_CHUNK:ci * GLA_CHUNK + 1, kc] * st + _tn(v[rows], kl[rows, kc])
        st_ref[h] = st
        o = intra + jnp.concatenate(inter, axis=0)
        o_ref[:, vc] = _gla_finish(o, gr_ref[:, vc].astype(F32), g_ref[...])


def _gla_prompt(gq, gk, gv, la, gr, g, tt):
    nb, t, _ = gq.shape
    tok = lambda n: pl.BlockSpec((None, tt, n), lambda b, i: (b, i, 0))
    return pl.pallas_call(
        _gla_prompt_kernel,
        grid=(nb, t // tt),
        in_specs=[tok(512), tok(512), tok(1024), tok(512), tok(1024), _resident((1, GLA_DV))],
        out_specs=[tok(1024),
                   pl.BlockSpec((None, GLA_HEADS, GLA_DV, GLA_DK), lambda b, i: (b, 0, 0, 0))],
        out_shape=[jax.ShapeDtypeStruct((nb, t, 1024), BF16),
                   jax.ShapeDtypeStruct((nb, GLA_HEADS, GLA_DV, GLA_DK), F32)],
        compiler_params=_params(1, 1),
        name="gla_prompt",
    )(gq, gk, gv, la, gr, g)


def _gla_sample_kernel(gq_ref, gk_ref, gv_ref, la_ref, gr_ref, g_ref, s_ref, o_ref, so_ref, *, seq):
    qg, kd, kl, dec, causal = _gla_local(gq_ref, gk_ref, la_ref, seq)
    rows_total = qg.shape[0]
    per8 = 8 // seq
    row8 = _iota((8, 1), 0)
    for h in range(GLA_HEADS):
        kc = slice(h * GLA_DK, (h + 1) * GLA_DK)
        vc = slice(h * GLA_DV, (h + 1) * GLA_DV)
        v = gv_ref[:, vc].astype(BF16)
        att = jnp.where(causal, _nt(qg[:, kc], kd[:, kc]), 0.0).astype(BF16)
        intra = _nn(att, v)
        inter = []
        for p in range(rows_total // 8):
            rows = slice(p * 8, (p + 1) * 8)
            d_hi, d_mid, d_lo = _split3(dec[rows, kc])
            inter_p = jnp.zeros((8, GLA_DV), F32)
            for j in range(per8):
                b = p * per8 + j
                r0 = j * seq
                s0 = s_ref[b, h]
                mine = (row8 >= r0) & (row8 < r0 + seq)
                inter_p = jnp.where(mine, _nn(qg[rows, kc], s0.astype(BF16)), inter_p)
                dl = jnp.where(row8 == r0, d_hi, jnp.where(row8 == r0 + 1, d_mid,
                               jnp.where(row8 == r0 + 2, d_lo, jnp.zeros_like(d_lo))))
                e = jnp.where((row8 >= r0) & (row8 < r0 + 3), 1.0, 0.0).astype(BF16)
                dec_b = _tn(dl, jnp.broadcast_to(e, (8, GLA_DV)))
                upd = _tn(jnp.where(mine, kl[rows, kc], jnp.zeros_like(kl[rows, kc])), v[rows])
                so_ref[b, h] = dec_b * s0 + upd
            inter.append(inter_p)
        o = intra + jnp.concatenate(inter, axis=0)
        o_ref[:, vc] = _gla_finish(o, gr_ref[:, vc].astype(F32), g_ref[...])


def _gla_sample(gq, gk, gv, la, gr, g, s0, seq, bb):
    n_seq = s0.shape[0]
    rows = bb * seq
    tok = lambda n: pl.BlockSpec((None, rows, n), lambda i: (0, i, 0))
    st = pl.BlockSpec((bb, GLA_HEADS, GLA_DK, GLA_DV), lambda i: (i, 0, 0, 0))
    return pl.pallas_call(
        functools.partial(_gla_sample_kernel, seq=seq),
        grid=(n_seq // bb,),
        in_specs=[tok(512), tok(512), tok(1024), tok(512), tok(1024), _resident((1, GLA_DV)), st],
        out_specs=[tok(1024), st],
        out_shape=[jax.ShapeDtypeStruct((1, n_seq * seq, 1024), BF16),
                   jax.ShapeDtypeStruct(s0.shape, F32)],
        compiler_params=_params(1),
        name="gla_sample",
    )(gq, gk, gv, la, gr, g, s0)


def _dsw_prompt_kernel(q_ref, kp_ref, kc_ref, vp_ref, vc_ref, o_ref, lse_ref, *, dil):
    qb = q_ref.shape[0]
    res = pl.program_id(2)
    first_key = jnp.where(pl.program_id(1) == 0, BAND, 0)
    qi = _iota((BAND, 2 * BAND), 0) + BAND
    ki = _iota((BAND, 2 * BAND), 1)
    band = (qi - ki >= 0) & (qi - ki <= BAND)
    lane = _iota((BAND, LANES), 1)
    for s in range(qb // BAND):
        rows = slice(s * BAND, (s + 1) * BAND)
        if s == 0:
            valid = band & (ki >= first_key)
        else:
            valid = band
        if dil == 1:
            tok_rows = rows
        else:
            tok_rows = pl.ds(s * BAND * dil + res, BAND, stride=dil)
        for hp in range(DSW_GW // LANES):
            cols = slice(hp * LANES, (hp + 1) * LANES)
            qp = q_ref[rows, cols]
            if s == 0:
                kcat = jnp.concatenate([kp_ref[:, cols], kc_ref[0:BAND, cols]], axis=0)
                vcat = jnp.concatenate([vp_ref[:, cols], vc_ref[0:BAND, cols]], axis=0)
            else:
                kcat = kc_ref[(s - 1) * BAND:(s + 1) * BAND, cols]
                vcat = vc_ref[(s - 1) * BAND:(s + 1) * BAND, cols]
            outs, lses = [], []
            for hh in range(LANES // DSW_HEAD_DIM):
                in_head = (lane // DSW_HEAD_DIM) == hh
                sc = _nt(jnp.where(in_head, qp, jnp.zeros_like(qp)), kcat)
                sc = jnp.where(valid, sc, -jnp.inf)
                m = jnp.max(sc, axis=-1, keepdims=True)
                e = jnp.exp(sc - m)
                den = jnp.sum(e, axis=-1, keepdims=True)
                outs.append(_nn((e / den).astype(BF16), vcat))
                lses.append(m + jnp.log(den))
            first = lane < DSW_HEAD_DIM
            o_ref[hp, tok_rows, :] = jnp.where(first, outs[0], outs[1])
            lse_ref[hp, tok_rows, :] = jnp.where(first, lses[0], jnp.broadcast_to(lses[1], (BAND, LANES)))


def _dsw_prompt(q, k, v, dil):
    nb = q.shape[0]
    seq_len = q.shape[-2]
    t = seq_len * dil
    tq = min(512, seq_len)
    sub = tq // BAND
    if dil == 1:
        cur = pl.BlockSpec((None, tq, DSW_GW), lambda b, j, r: (b, j, 0))
        prev = pl.BlockSpec((None, BAND, DSW_GW), lambda b, j, r: (b, jnp.maximum(j * sub - 1, 0), 0))
    else:
        cur = pl.BlockSpec((None, None, tq, DSW_GW), lambda b, j, r: (b, r, j, 0))
        prev = pl.BlockSpec((None, None, BAND, DSW_GW), lambda b, j, r: (b, r, jnp.maximum(j * sub - 1, 0), 0))
    n_slab = DSW_GW // LANES
    out = pl.BlockSpec((None, n_slab, tq * dil, LANES), lambda b, j, r: (b, 0, j, 0))
    return pl.pallas_call(
        functools.partial(_dsw_prompt_kernel, dil=dil),
        grid=(nb, seq_len // tq, dil),
        in_specs=[cur, prev, cur, prev, cur],
        out_specs=[out, out],
        out_shape=[jax.ShapeDtypeStruct((nb, n_slab, t, LANES), F32)] * 2,
        compiler_params=_params(2, 1),
        name=f"dsw_prompt_d{dil}",
    )(q, k, k, v, v)


def _dsw_sample_kernel(q_ref, kn_ref, vn_ref, c_ref, o_ref, lse_ref, co_ref, *, seq, dil):
    per8 = 8 // seq
    win = c_ref.shape[2]
    n_rows = DSW_HEADS * 8
    lane = _iota((8, LANES), 1)
    head_of_lane = _iota((8, DSW_GW), 1) // DSW_HEAD_DIM
    row8 = _iota((8, 1), 0)
    r = _iota((n_rows, 1), 0)
    r_step = r % seq
    r_seq = (r % 8) // seq
    key = _iota((n_rows, win), 1)
    cache_ok = ((key % dil) == (r_step % dil)) & (key >= r_step)
    c128 = _iota((n_rows, LANES), 1)
    new_ok = ((c128 < 8) & ((c128 // seq) == r_seq) & ((c128 % seq) <= r_step)
              & (((r_step - c128 % seq) % dil) == 0))
    pad = jnp.zeros((LANES - 8, DSW_GW), BF16)

    def by_head(x):
        out = x[(DSW_HEADS - 1) * 8:DSW_HEADS * 8]
        for h in range(DSW_HEADS - 2, -1, -1):
            out = jnp.where(head_of_lane == h, x[h * 8:(h + 1) * 8], out)
        return out

    lane_sq = _iota((LANES, LANES), 1)
    p_row = _iota((8, LANES), 0)
    for grp in range(q_ref.shape[0] // 8):
        r8 = slice(grp * 8, (grp + 1) * 8)
        q8 = q_ref[r8, :]
        qrows = jnp.concatenate([jnp.where(head_of_lane == h, q8, 0.0) for h in range(DSW_HEADS)],
                                axis=0).astype(BF16)
        kn8 = kn_ref[r8, :]
        vn8 = vn_ref[r8, :]
        kn_t = jnp.concatenate([kn8.astype(BF16), pad], axis=0)
        vn_t = jnp.concatenate([vn8.astype(BF16), pad], axis=0)
        scn = jnp.where(new_ok, _nt(qrows, kn_t), -jnp.inf)
        m_new = jnp.max(scn, axis=-1, keepdims=True)
        o_p = jnp.zeros((8, DSW_GW), F32)
        l_p = jnp.zeros((8, DSW_GW), F32)
        for j in range(per8):
            b = grp * per8 + j
            kt = c_ref[b, 0:DSW_GW, :].astype(BF16)
            vt = c_ref[b, DSW_GW:2 * DSW_GW, :].astype(BF16)
            sc = jnp.where(cache_ok, _nn(qrows, kt), -jnp.inf)
            m = jnp.maximum(jnp.max(sc, axis=-1, keepdims=True), m_new)
            e = jnp.exp(sc - m)
            en = jnp.exp(scn - m)
            den = jnp.sum(e, axis=-1, keepdims=True) + jnp.sum(en, axis=-1, keepdims=True)
            o = _nt((e / den).astype(BF16), vt) + _nn((en / den).astype(BF16), vn_t)
            lse = jnp.broadcast_to(m + jnp.log(den), (n_rows, DSW_GW))
            mine = (row8 // seq) == j
            o_p = jnp.where(mine, by_head(o), o_p)
            l_p = jnp.where(mine, by_head(lse), l_p)
        o_ref[r8, :] = o_p
        lse_ref[r8, :] = l_p

        hi, mid, lo = _split3(jnp.concatenate([kn8, vn8], axis=1))
        for j in range(per8):
            b = grp * per8 + j
            place = jnp.where(((p_row // seq) == j) & (lane == LANES - seq + p_row % seq), 1.0, 0.0).astype(BF16)
            new_cols = _tn(hi, place) + _tn(mid, place) + _tn(lo, place)
            for blk in range(2 * DSW_GW // LANES):
                rows = slice(blk * LANES, (blk + 1) * LANES)
                rolled = pltpu.roll(c_ref[b, rows, :], win - seq, 1)
                if win > LANES:
                    co_ref[b, rows, 0:win - LANES] = rolled[:, 0:win - LANES]
                co_ref[b, rows, win - LANES:win] = jnp.where(lane_sq < LANES - seq, rolled[:, win - LANES:win],
                                                             new_cols[rows])


def _dsw_sample(q32, k32, v32, cache, g, dil, seq):
    n_seq, win = cache.shape[0], cache.shape[1]
    per8 = 8 // seq
    groups = max(1, min(8, 4 * 512 // win))
    n_blk = per8 * groups
    view = jnp.transpose(cache, (0, 2, 3, 4, 1)).reshape(n_seq, 2 * DSW_GW, win)
    tok = pl.BlockSpec((None, 8 * groups, DSW_GW), lambda i: (0, i, g))
    tok_out = pl.BlockSpec((None, 8 * groups, DSW_GW), lambda i: (0, i, 0))
    cspec = pl.BlockSpec((n_blk, 2 * DSW_GW, win), lambda i: (i, 0, 0))
    o, lse, new = pl.pallas_call(
        functools.partial(_dsw_sample_kernel, seq=seq, dil=dil),
        grid=(n_seq // n_blk,),
        in_specs=[tok, tok, tok, cspec],
        out_specs=[tok_out, tok_out, cspec],
        out_shape=[jax.ShapeDtypeStruct((1, n_seq * seq, DSW_GW), F32)] * 2
                  + [jax.ShapeDtypeStruct(view.shape, F32)],
        compiler_params=_params(1),
        name=f"dsw_sample_d{dil}",
    )(q32, k32, v32, view)
    new = jnp.transpose(new.reshape(n_seq, 2, DSW_HEADS, DSW_HEAD_DIM, win), (0, 4, 1, 2, 3))
    n_slab = DSW_GW // LANES
    slabs = lambda a: jnp.transpose(a.reshape(1, n_seq * seq, n_slab, LANES), (0, 2, 1, 3))
    return slabs(o), slabs(lse), new


def _merge_kernel(oa_ref, o0_ref, o1_ref, o2_ref, l0_ref, l1_ref, l2_ref, ga_ref, gb_ref, x_ref,
                  g1_ref, sc_ref, sh_ref, n2_ref, wpa_ref, wpb_ref, wo_ref, x1_ref, h2_ref):
    ob = []
    for slab in range(DSW_GW // LANES):
        l0, l1, l2 = l0_ref[slab], l1_ref[slab], l2_ref[slab]
        m = jnp.maximum(jnp.maximum(l0, l1), l2)
        w0, w1, w2 = jnp.exp(l0 - m), jnp.exp(l1 - m), jnp.exp(l2 - m)
        den = w0 + w1 + w2
        ob.append((w0 / den) * o0_ref[slab] + (w1 / den) * o1_ref[slab] + (w2 / den) * o2_ref[slab])
    ob = jnp.concatenate(ob, axis=1).astype(BF16)
    merged = (jax.nn.sigmoid(ga_ref[...].astype(F32)) * _nn(oa_ref[...], wpa_ref[...])
              + jax.nn.sigmoid(gb_ref[...].astype(F32)) * _nn(ob, wpb_ref[...]))
    x1 = x_ref[...] + g1_ref[...] * _nn(merged.astype(BF16), wo_ref[...])
    x1_ref[...] = x1
    h2_ref[...] = (_rms(x1, n2_ref[...]) * (1.0 + sc_ref[...]) + sh_ref[...]).astype(BF16)


def _merge(oa, og, lg, ga, gb, x, g1, sc2, sh2, n2, wpa, wpb, wo, tm):
    nb, t, d = x.shape
    tok = lambda n: pl.BlockSpec((None, tm, n), lambda b, i: (b, i, 0))
    slab = pl.BlockSpec((None, DSW_GW // LANES, tm, LANES), lambda b, i: (b, 0, i, 0))
    return pl.pallas_call(
        _merge_kernel,
        grid=(nb, t // tm),
        in_specs=[tok(1024)] + [slab] * 6 + [tok(d), tok(d), tok(d),
                  _mod_spec(g1, tm), _mod_spec(sc2, tm), _mod_spec(sh2, tm), _resident((1, d)),
                  _resident(wpa.shape), _resident(wpb.shape), _resident(wo.shape)],
        out_specs=[tok(d), tok(d)],
        out_shape=[jax.ShapeDtypeStruct((nb, t, d), F32), jax.ShapeDtypeStruct((nb, t, d), BF16)],
        compiler_params=_params(2),
        name="merge_outproj",
    )(oa, *og, *lg, ga, gb, x, g1, sc2, sh2, n2, wpa, wpb, wo)


def _ffn_kernel(h_ref, x_ref, g2_ref, nf_ref, wu_ref, wd_ref, y_ref, *, final_norm, n_split):
    h = h_ref[...]
    d_ff = wd_ref.shape[0]
    step = d_ff // n_split
    acc = None
    for j in range(n_split):
        u1 = _nn(h, wu_ref[:, j * step:(j + 1) * step])
        u2 = _nn(h, wu_ref[:, d_ff + j * step:d_ff + (j + 1) * step])
        a = (u1 * jax.nn.sigmoid(u1) * u2).astype(BF16)
        part = _nn(a, wd_ref[j * step:(j + 1) * step, :])
        acc = part if acc is None else acc + part
    x2 = x_ref[...] + g2_ref[...] * acc
    y_ref[...] = _rms(x2, nf_ref[...]) if final_norm else x2


def _ffn(h2, x1, g2, nf, wu, wd, tm, final_norm):
    nb, t, d = x1.shape
    tok = lambda n: pl.BlockSpec((None, tm, n), lambda b, i: (b, i, 0))
    return pl.pallas_call(
        functools.partial(_ffn_kernel, final_norm=final_norm, n_split=2),
        grid=(nb, t // tm),
        in_specs=[tok(d), tok(d), _mod_spec(g2, tm), _resident((1, d)),
                  _resident(wu.shape), _resident(wd.shape)],
        out_specs=tok(d),
        out_shape=jax.ShapeDtypeStruct((nb, t, d), F32),
        compiler_params=_params(2),
        name="ffn",
    )(h2, x1, g2, nf, wu, wd)


def _rope_tables(pos):
    half = DSW_HEAD_DIM // 2
    inv = ROPE_THETA ** (-jnp.arange(half, dtype=F32) / half)
    ang = pos[:, None] * inv[None, :]
    cos, sin = jnp.cos(ang), jnp.sin(ang)
    reps = LANES // half
    cosf = jnp.tile(cos, (1, reps))
    sign = jnp.tile(jnp.concatenate([-jnp.ones((half,), F32), jnp.ones((half,), F32)]), LANES // DSW_HEAD_DIM)
    return cosf, jnp.tile(sin, (1, reps)) * sign[None, :]


def _layer_weights(w_in, b_in, w_alpha2, b_alpha2, w_proj_a, w_proj_b, w_out, w_up, w_down):
    bf = lambda a: a.astype(BF16)
    row = lambda a: a.reshape(1, -1)
    o_glr, o_dq, o_ga = 3072, 3088, 5392
    pad_r = LANES - GLA_RANK
    return dict(
        wa=bf(w_in[:, :o_glr]), ba=row(b_in[:o_glr]),
        wg=bf(jnp.pad(w_in[:, o_glr:o_dq], ((0, 0), (0, pad_r)))), bg=row(jnp.pad(b_in[o_glr:o_dq], (0, pad_r))),
        w2=bf(jnp.pad(w_alpha2, ((0, pad_r), (0, 0)))), b2=row(b_alpha2),
        wb=bf(w_in[:, o_dq:o_ga]), bb=row(b_in[o_dq:o_ga]),
        wc=bf(w_in[:, o_ga:]), bc=row(b_in[o_ga:]),
        wpa=bf(w_proj_a), wpb=bf(w_proj_b), wo=bf(w_out), wu=bf(w_up), wd=bf(w_down))


def _kv_stack(k32, v32, g, keep):
    nb, t, _ = k32.shape
    cols = slice(g * DSW_GW, (g + 1) * DSW_GW)
    k = k32[:, t - keep:, cols].reshape(nb, keep, DSW_HEADS, DSW_HEAD_DIM)
    v = v32[:, t - keep:, cols].reshape(nb, keep, DSW_HEADS, DSW_HEAD_DIM)
    return jnp.stack([k, v], axis=2)


def kernel(x_prompt, x_sample, state_gla, cache_kv_w128, cache_kv_w512, cache_kv_w2048, c_prompt, c_sample,
           norm1_g, norm2_g, w_ada, b_ada, w_in, b_in, w_alpha2, b_alpha2, gla_norm_g, w_proj_a, w_proj_b,
           w_out, w_up, w_down, normf_g):
    depth = w_ada.shape[0]
    nb, t, d = x_prompt.shape
    n_seq, seq, _ = x_sample.shape
    assert 8 % seq == 0 and seq >= 3, "sample kernels pack whole sequences into 8-row groups"
    past = PAST_LEN
    caches = (cache_kv_w128, cache_kv_w512, cache_kv_w2048)

    cos_p, sin_p = _rope_tables(jnp.arange(t, dtype=F32))
    cos_s, sin_s = _rope_tables(jnp.tile(past + jnp.arange(seq, dtype=F32), n_seq))

    n_c = nb + n_seq
    pad_c = (-n_c) % 8
    c_all = jnp.pad(jnp.concatenate([c_prompt, c_sample], axis=0), ((0, pad_c), (0, 0)))

    xp = x_prompt
    xs = x_sample.reshape(1, n_seq * seq, d)
    row = lambda a: a.reshape(1, -1)
    sp_l, kvp_l, ss_l, kvs_l = [], [], [], []
    for l in range(depth):
        w = _layer_weights(w_in[l], b_in[l], w_alpha2[l], b_alpha2[l], w_proj_a[l], w_proj_b[l],
                           w_out[l], w_up[l], w_down[l])
        mod = _ada(c_all, w_ada[l], b_ada[l])
        mod_p = [mod[:nb, i * d:(i + 1) * d].reshape(nb, 1, d) for i in range(6)]
        mod_s = [jnp.repeat(mod[nb:nb + n_seq, i * d:(i + 1) * d], seq, axis=0).reshape(1, n_seq * seq, d)
                 for i in range(6)]
        last = l == depth - 1

        sh1, sc1, g1, sh2, sc2, g2 = mod_p
        gq, gk, gv, gr, la, ga, gb, k32, v32, *qkv = _inproj(
            xp, sc1, sh1, row(norm1_g[l]), cos_p, sin_p, w, tm=512, fold=True)
        oa, st = _gla_prompt(gq, gk, gv, la, gr, row(gla_norm_g[l]), tt=256)
        og, lg = zip(*[_dsw_prompt(*qkv[3 * g:3 * g + 3], dil) for g, (_, dil) in enumerate(DSW_GROUPS)])
        x1, h2 = _merge(oa, og, lg, ga, gb, xp, g1, sc2, sh2, row(norm2_g[l]), w["wpa"], w["wpb"], w["wo"], tm=512)
        xp = _ffn(h2, x1, g2, row(normf_g), w["wu"], w["wd"], tm=512, final_norm=last)
        sp_l.append(jnp.swapaxes(st, 2, 3))
        kvp_l.append(tuple(_kv_stack(k32, v32, g, min(win, t)) for g, (win, _) in enumerate(DSW_GROUPS)))

        sh1, sc1, g1, sh2, sc2, g2 = mod_s
        gq, gk, gv, gr, la, ga, gb, k32, v32, q32 = _inproj(
            xs, sc1, sh1, row(norm1_g[l]), cos_s, sin_s, w, tm=256, fold=False)
        oa, s_new = _gla_sample(gq, gk, gv, la, gr, row(gla_norm_g[l]), state_gla[l], seq, bb=8)
        og, lg, new_kv = zip(*[_dsw_sample(q32, k32, v32, caches[g][l], g, dil, seq)
                               for g, (_, dil) in enumerate(DSW_GROUPS)])
        x1, h2 = _merge(oa, og, lg, ga, gb, xs, g1, sc2, sh2, row(norm2_g[l]), w["wpa"], w["wpb"], w["wo"], tm=256)
        xs = _ffn(h2, x1, g2, row(normf_g), w["wu"], w["wd"], tm=256, final_norm=last)
        ss_l.append(s_new)
        kvs_l.append(new_kv)

    y_prompt = xp
    y_sample = xs.reshape(n_seq, seq, d)
    stack = lambda items: jnp.stack(list(items))
    return (y_prompt, y_sample, stack(sp_l),
            stack(kv[0] for kv in kvp_l), stack(kv[1] for kv in kvp_l), stack(kv[2] for kv in kvp_l),
            stack(ss_l),
            stack(kv[0] for kv in kvs_l), stack(kv[1] for kv in kvs_l), stack(kv[2] for kv in kvs_l))
```

```python
import functools

import jax
import jax.numpy as jnp
from jax import lax
from jax.experimental import pallas as pl
from jax.experimental.pallas import tpu as pltpu

F32 = jnp.float32
BF16 = jnp.bfloat16

EPS = 1e-6
GLA_HEADS = 4
GLA_DK = 128
GLA_DV = 256
GLA_RANK = 16
GLA_TAU = 16.0
GLA_CHUNK = 64
GLA_SCALE = GLA_DK ** -0.5
DSW_GROUPS = ((128, 1), (512, 4), (2048, 16))
DSW_HEADS = 4
DSW_HEAD_DIM = 64
DSW_SCALE = DSW_HEAD_DIM ** -0.5
DSW_GW = DSW_HEADS * DSW_HEAD_DIM
BAND = 128
ROPE_THETA = 10000.0
PAST_LEN = 8192
LANES = 128
VMEM_LIMIT = 56 * 1024 * 1024


def _nn(a, b):
    return jnp.dot(a, b, preferred_element_type=F32)


def _nt(a, b):
    return lax.dot_general(a, b, (((1,), (1,)), ((), ())), preferred_element_type=F32)


def _tn(a, b):
    return lax.dot_general(a, b, (((0,), (0,)), ((), ())), preferred_element_type=F32)


def _split3(x):
    hi = x.astype(BF16)
    r1 = x - hi.astype(F32)
    mid = r1.astype(BF16)
    lo = (r1 - mid.astype(F32)).astype(BF16)
    return hi, mid, lo


def _iota(shape, dim):
    return lax.broadcasted_iota(jnp.int32, shape, dim)


def _rms(x, g):
    return x * lax.rsqrt(jnp.mean(x * x, axis=-1, keepdims=True) + EPS) * g


def _params(n_parallel, n_arbitrary=0):
    sem = ("parallel",) * n_parallel + ("arbitrary",) * n_arbitrary
    return pltpu.CompilerParams(dimension_semantics=sem, vmem_limit_bytes=VMEM_LIMIT)


def _resident(shape):
    nd = len(shape)
    return pl.BlockSpec(shape, lambda *_: (0,) * nd, pipeline_mode=pl.Buffered(1))


def _ada_kernel(c_ref, w_ref, b_ref, o_ref):
    c = c_ref[...]
    a = (c * jax.nn.sigmoid(c)).astype(BF16)
    o_ref[...] = _nn(a, w_ref[...].astype(BF16)) + b_ref[...]


def _ada(c_all, w_ada, b_ada):
    n, d = c_all.shape
    ncol = w_ada.shape[1]
    tn = 1536
    return pl.pallas_call(
        _ada_kernel,
        grid=(ncol // tn,),
        in_specs=[pl.BlockSpec((n, d), lambda j: (0, 0)),
                  pl.BlockSpec((d, tn), lambda j: (0, j)),
                  pl.BlockSpec((1, tn), lambda j: (0, j))],
        out_specs=pl.BlockSpec((n, tn), lambda j: (0, j)),
        out_shape=jax.ShapeDtypeStruct((n, ncol), F32),
        compiler_params=_params(1),
        name="ada_mod",
    )(c_all, w_ada, b_ada.reshape(1, ncol))


def _rope(x, cosf, sins, first_half):
    rot = jnp.where(first_half, pltpu.roll(x, LANES - 32, 1), pltpu.roll(x, 32, 1))
    return x * cosf + rot * sins


def _inproj_kernel(x_ref, sc_ref, sh_ref, g_ref, cos_ref, sin_ref,
                   wa_ref, ba_ref, wg_ref, bg_ref, w2_ref, b2_ref, wb_ref, bb_ref, wc_ref, bc_ref,
                   gq_ref, gk_ref, gv_ref, gr_ref, la_ref, ga_ref, gb_ref, *rest, fold, kv_first_tile):
    x = x_ref[...]
    h = (_rms(x, g_ref[...]) * (1.0 + sc_ref[...]) + sh_ref[...]).astype(BF16)

    gq_ref[...] = (_nn(h, wa_ref[:, 0:512]) + ba_ref[:, 0:512]).astype(gq_ref.dtype)
    gk_ref[...] = (_nn(h, wa_ref[:, 512:1024]) + ba_ref[:, 512:1024]).astype(gk_ref.dtype)
    gv_ref[...] = (_nn(h, wa_ref[:, 1024:2048]) + ba_ref[:, 1024:2048]).astype(gv_ref.dtype)
    gr_ref[...] = (_nn(h, wa_ref[:, 2048:3072]) + ba_ref[:, 2048:3072]).astype(gr_ref.dtype)

    glr = (_nn(h, wg_ref[...]) + bg_ref[...]).astype(BF16)
    z = _nn(glr, w2_ref[...]) + b2_ref[...]
    la_ref[...] = jax.nn.log_sigmoid(z) * (1.0 / GLA_TAU)

    cosf = cos_ref[...]
    sins = sin_ref[...]
    first_half = (_iota(cosf.shape, 1) % DSW_HEAD_DIM) < (DSW_HEAD_DIM // 2)
    tm = x.shape[0]
    per_group = DSW_GW // LANES
    width = 3 * DSW_GW

    def proj(off, g):
        cols = slice(off + g * DSW_GW, off + (g + 1) * DSW_GW)
        full = _nn(h, wb_ref[:, cols]) + bb_ref[:, cols]
        return [full[:, s * LANES:(s + 1) * LANES] for s in range(per_group)]

    for g, (_, dil) in enumerate(DSW_GROUPS):
        qs = [_rope(a, cosf, sins, first_half) * DSW_SCALE for a in proj(0, g)]
        ks = [_rope(a, cosf, sins, first_half) for a in proj(width, g)]
        vs = proj(2 * width, g)
        if fold:
            kvt_ref = rest[3 * len(DSW_GROUPS) + g]

            @pl.when(pl.program_id(1) >= kv_first_tile[g])
            def _():
                kvt_ref[0:DSW_GW, :] = jnp.concatenate(ks, axis=1).T
                kvt_ref[DSW_GW:2 * DSW_GW, :] = jnp.concatenate(vs, axis=1).T

        for slab in range(per_group):
            cols = slice(g * DSW_GW + slab * LANES, g * DSW_GW + (slab + 1) * LANES)
            lanes = slice(slab * LANES, (slab + 1) * LANES)
            if not fold:
                rest[0][:, cols] = ks[slab]
                rest[1][:, cols] = vs[slab]
                rest[2][:, cols] = qs[slab]
                continue
            scratch = rest[-1]
            for which, val in enumerate((qs[slab], ks[slab], vs[slab])):
                out_ref = rest[3 * g + which]
                if dil == 1:
                    out_ref[:, lanes] = val.astype(BF16)
                else:
                    scratch[which] = val
                    for r in range(dil):
                        out_ref[r, :, lanes] = scratch[which, pl.ds(r, tm // dil, stride=dil), :].astype(BF16)

    ga_ref[...] = (_nn(h, wc_ref[:, 0:1024]) + bc_ref[:, 0:1024]).astype(ga_ref.dtype)
    gb_ref[...] = (_nn(h, wc_ref[:, 1024:2048]) + bc_ref[:, 1024:2048]).astype(gb_ref.dtype)


def _mod_spec(arr, tm):
    if arr.shape[1] == 1:
        return pl.BlockSpec((None, 1, arr.shape[2]), lambda b, i: (b, 0, 0))
    return pl.BlockSpec((None, tm, arr.shape[2]), lambda b, i: (b, i, 0))


def _inproj(x, sc, sh, g, cos_t, sin_t, w, tm, fold):
    nb, t, d = x.shape
    tok = lambda n: pl.BlockSpec((None, tm, n), lambda b, i: (b, i, 0))
    out_cols = (512, 512, 1024, 1024, 512, 1024, 1024)
    out_dt = (BF16, BF16, BF16, BF16, F32, BF16, BF16)
    out_specs = [tok(n) for n in out_cols]
    out_shape = [jax.ShapeDtypeStruct((nb, t, n), dt) for n, dt in zip(out_cols, out_dt)]
    scratch = []
    kv_first_tile = ()
    if fold:
        for _, dil in DSW_GROUPS:
            for _ in range(3):
                if dil == 1:
                    out_specs.append(tok(DSW_GW))
                    out_shape.append(jax.ShapeDtypeStruct((nb, t, DSW_GW), BF16))
                else:
                    out_specs.append(pl.BlockSpec((None, dil, tm // dil, DSW_GW), lambda b, i: (b, 0, i, 0)))
                    out_shape.append(jax.ShapeDtypeStruct((nb, dil, t // dil, DSW_GW), BF16))
        for win, _ in DSW_GROUPS:
            width = min(max(win, tm), t)
            first = (t - width) // tm
            kv_first_tile += (first,)
            out_specs.append(pl.BlockSpec((None, 2 * DSW_GW, tm),
                                          lambda b, i, first=first: (b, 0, jnp.maximum(i - first, 0))))
            out_shape.append(jax.ShapeDtypeStruct((nb, 2 * DSW_GW, width), F32))
        scratch = [pltpu.VMEM((3, tm, LANES), F32)]
    else:
        for _ in range(3):
            out_specs.append(tok(3 * DSW_GW))
            out_shape.append(jax.ShapeDtypeStruct((nb, t, 3 * DSW_GW), F32))
    weights = (w["wa"], w["ba"], w["wg"], w["bg"], w["w2"], w["b2"], w["wb"], w["bb"], w["wc"], w["bc"])
    return pl.pallas_call(
        functools.partial(_inproj_kernel, fold=fold, kv_first_tile=kv_first_tile),
        grid=(nb, t // tm),
        in_specs=[tok(d), _mod_spec(sc, tm), _mod_spec(sh, tm), _resident((1, d)),
                  pl.BlockSpec((tm, LANES), lambda b, i: (i, 0)),
                  pl.BlockSpec((tm, LANES), lambda b, i: (i, 0))]
                 + [_resident(a.shape) for a in weights],
        out_specs=out_specs,
        out_shape=out_shape,
        scratch_shapes=scratch,
        compiler_params=_params(1, 1),
        name="inproj",
    )(x, sc, sh, g, cos_t, sin_t, *weights)


def _gla_local(gq_ref, gk_ref, la_ref, chunk):
    la = la_ref[...]
    tt = la.shape[0]
    r = _iota((tt, tt), 0)
    c = _iota((tt, tt), 1)
    same = (r // chunk) == (c // chunk)
    tri = jnp.where(same & (c <= r), 1.0, 0.0).astype(BF16)
    ones = jnp.where(same, 1.0, 0.0).astype(BF16)
    hi, mid, lo = _split3(la)
    b = _nn(tri, hi) + _nn(tri, mid) + _nn(tri, lo)
    bl = _nn(ones, hi) + _nn(ones, mid) + _nn(ones, lo)
    gq = gq_ref[...].astype(F32)
    gk = gk_ref[...].astype(F32)
    qg = (gq * GLA_SCALE * jnp.exp(b)).astype(BF16)
    kd = (gk * jnp.exp(-b)).astype(BF16)
    kl = (gk * jnp.exp(bl - b)).astype(BF16)
    causal = same & (c <= r)
    return qg, kd, kl, jnp.exp(bl), causal


def _gla_finish(o, gr, g):
    return (_rms(o, g) * (gr * jax.nn.sigmoid(gr))).astype(BF16)


def _gla_prompt_body(first_tile, gq_ref, gk_ref, gv_ref, la_ref, gr_ref, g_ref, o_ref, st_ref):
    @pl.when(first_tile)
    def _():
        st_ref[...] = jnp.zeros_like(st_ref)

    qg, kd, kl, dec, causal = _gla_local(gq_ref, gk_ref, la_ref, GLA_CHUNK)
    tt = qg.shape[0]
    for h in range(GLA_HEADS):
        kc = slice(h * GLA_DK, (h + 1) * GLA_DK)
        vc = slice(h * GLA_DV, (h + 1) * GLA_DV)
        v = gv_ref[:, vc].astype(BF16)
        att = jnp.where(causal, _nt(qg[:, kc], kd[:, kc]), 0.0).astype(BF16)
        intra = _nn(att, v)
        st = st_ref[h]
        inter = []
        for ci in range(tt // GLA_CHUNK):
            rows = slice(ci * GLA_CHUNK, (ci + 1) * GLA_CHUNK)
            inter.append(_nt(qg[rows, kc], st.astype(BF16)))
            st = dec[ci * GLA_CHUNK:ci * GLA_CHUNK + 1, kc] * st + _tn(v[rows], kl[rows, kc])
        st_ref[h] = st
        o = intra + jnp.concatenate(inter, axis=0)
        o_ref[:, vc] = _gla_finish(o, gr_ref[:, vc].astype(F32), g_ref[...])


def _gla_prompt_call(gq, gk, gv, la, gr, g, tt):
    nb, t, _ = gq.shape
    tiles = t // tt
    tok = lambda n: pl.BlockSpec((None, tt, n), lambda i: (i // tiles, i % tiles, 0))
    return dict(
        steps=nb * tiles, tiles=tiles, args=(gq, gk, gv, la, gr, g),
        in_specs=[tok(512), tok(512), tok(1024), tok(512), tok(1024), _resident((1, GLA_DV))],
        out_specs=[tok(1024),
                   pl.BlockSpec((None, GLA_HEADS, GLA_DV, GLA_DK), lambda i: (i // tiles, 0, 0, 0))],
        out_shape=[jax.ShapeDtypeStruct((nb, t, 1024), BF16),
                   jax.ShapeDtypeStruct((nb, GLA_HEADS, GLA_DV, GLA_DK), F32)])


def _gla_prompt_kernel(*refs, tiles):
    _gla_prompt_body(pl.program_id(0) % tiles == 0, *refs)


def _gla_prompt(gq, gk, gv, la, gr, g, tt):
    c = _gla_prompt_call(gq, gk, gv, la, gr, g, tt)
    return pl.pallas_call(
        functools.partial(_gla_prompt_kernel, tiles=c["tiles"]),
        grid=(c["steps"],), in_specs=c["in_specs"], out_specs=c["out_specs"], out_shape=c["out_shape"],
        compiler_params=_params(0, 1),
        name="gla_prompt",
    )(*c["args"])


def _gla_prompt_dsw_sample_kernel(*refs, tiles, seq, dil):
    _gla_prompt_body(pl.program_id(0) % tiles == 0, *refs[0:6], *refs[10:12])
    _dsw_sample_kernel(*refs[6:10], *refs[12:15], seq=seq, dil=dil)


def _gla_prompt_dsw_sample(gla, dsw, seq, dil):
    assert gla["steps"] == dsw["steps"]
    return pl.pallas_call(
        functools.partial(_gla_prompt_dsw_sample_kernel, tiles=gla["tiles"], seq=seq, dil=dil),
        grid=(gla["steps"],),
        in_specs=gla["in_specs"] + dsw["in_specs"],
        out_specs=gla["out_specs"] + dsw["out_specs"],
        out_shape=gla["out_shape"] + dsw["out_shape"],
        compiler_params=_params(0, 1),
        name=f"gla_prompt_dsw_sample_d{dil}",
    )(*gla["args"], *dsw["args"])


def _gla_sample_kernel(gq_ref, gk_ref, gv_ref, la_ref, gr_ref, g_ref, s_ref, o_ref, so_ref, *, seq):
    qg, kd, kl, dec, causal = _gla_local(gq_ref, gk_ref, la_ref, seq)
    rows_total = qg.shape[0]
    per8 = 8 // seq
    row8 = _iota((8, 1), 0)
    for h in range(GLA_HEADS):
        kc = slice(h * GLA_DK, (h + 1) * GLA_DK)
        vc = slice(h * GLA_DV, (h + 1) * GLA_DV)
        v = gv_ref[:, vc].astype(BF16)
        att = jnp.where(causal, _nt(qg[:, kc], kd[:, kc]), 0.0).astype(BF16)
        intra = _nn(att, v)
        inter = []
        for p in range(rows_total // 8):
            rows = slice(p * 8, (p + 1) * 8)
            d_hi, d_mid, d_lo = _split3(dec[rows, kc])
            inter_p = jnp.zeros((8, GLA_DV), F32)
            for j in range(per8):
                b = p * per8 + j
                r0 = j * seq
                s0 = s_ref[b, h]
                mine = (row8 >= r0) & (row8 < r0 + seq)
                inter_p = jnp.where(mine, _nn(qg[rows, kc], s0.astype(BF16)), inter_p)
                dl = jnp.where(row8 == r0, d_hi, jnp.where(row8 == r0 + 1, d_mid,
                               jnp.where(row8 == r0 + 2, d_lo, jnp.zeros_like(d_lo))))
                e = jnp.where((row8 >= r0) & (row8 < r0 + 3), 1.0, 0.0).astype(BF16)
                dec_b = _tn(dl, jnp.broadcast_to(e, (8, GLA_DV)))
                upd = _tn(jnp.where(mine, kl[rows, kc], jnp.zeros_like(kl[rows, kc])), v[rows])
                so_ref[b, h] = dec_b * s0 + upd
            inter.append(inter_p)
        o = intra + jnp.concatenate(inter, axis=0)
        o_ref[:, vc] = _gla_finish(o, gr_ref[:, vc].astype(F32), g_ref[...])


def _gla_sample(gq, gk, gv, la, gr, g, s0, seq, bb):
    n_seq = s0.shape[0]
    rows = bb * seq
    tok = lambda n: pl.BlockSpec((None, rows, n), lambda i: (0, i, 0))
    st = pl.BlockSpec((bb, GLA_HEADS, GLA_DK, GLA_DV), lambda i: (i, 0, 0, 0))
    return pl.pallas_call(
        functools.partial(_gla_sample_kernel, seq=seq),
        grid=(n_seq // bb,),
        in_specs=[tok(512), tok(512), tok(1024), tok(512), tok(1024), _resident((1, GLA_DV)), st],
        out_specs=[tok(1024), st],
        out_shape=[jax.ShapeDtypeStruct((1, n_seq * seq, 1024), BF16),
                   jax.ShapeDtypeStruct(s0.shape, F32)],
        compiler_params=_params(1),
        name="gla_sample",
    )(gq, gk, gv, la, gr, g, s0)


def _dsw_prompt_kernel(q_ref, kp_ref, kc_ref, vp_ref, vc_ref, o_ref, lse_ref, *, dil):
    qb = q_ref.shape[0]
    res = pl.program_id(2)
    first_key = jnp.where(pl.program_id(1) == 0, BAND, 0)
    qi = _iota((BAND, 2 * BAND), 0) + BAND
    ki = _iota((BAND, 2 * BAND), 1)
    band = (qi - ki >= 0) & (qi - ki <= BAND)
    lane = _iota((BAND, LANES), 1)
    for s in range(qb // BAND):
        rows = slice(s * BAND, (s + 1) * BAND)
        if s == 0:
            valid = band & (ki >= first_key)
        else:
            valid = band
        if dil == 1:
            tok_rows = rows
        else:
            tok_rows = pl.ds(s * BAND * dil + res, BAND, stride=dil)
        for hp in range(DSW_GW // LANES):
            cols = slice(hp * LANES, (hp + 1) * LANES)
            qp = q_ref[rows, cols]
            if s == 0:
                kcat = jnp.concatenate([kp_ref[:, cols], kc_ref[0:BAND, cols]], axis=0)
                vcat = jnp.concatenate([vp_ref[:, cols], vc_ref[0:BAND, cols]], axis=0)
            else:
                kcat = kc_ref[(s - 1) * BAND:(s + 1) * BAND, cols]
                vcat = vc_ref[(s - 1) * BAND:(s + 1) * BAND, cols]
            outs, lses = [], []
            for hh in range(LANES // DSW_HEAD_DIM):
                in_head = (lane // DSW_HEAD_DIM) == hh
                sc = _nt(jnp.where(in_head, qp, jnp.zeros_like(qp)), kcat)
                sc = jnp.where(valid, sc, -jnp.inf)
                m = jnp.max(sc, axis=-1, keepdims=True)
                e = jnp.exp(sc - m)
                den = jnp.sum(e, axis=-1, keepdims=True)
                outs.append(_nn((e / den).astype(BF16), vcat))
                lses.append(m + jnp.log(den))
            first = lane < DSW_HEAD_DIM
            o_ref[hp, tok_rows, :] = jnp.where(first, outs[0], outs[1])
            lse_ref[hp, tok_rows, :] = jnp.where(first, lses[0], jnp.broadcast_to(lses[1], (BAND, LANES)))


def _dsw_prompt(q, k, v, dil):
    nb = q.shape[0]
    seq_len = q.shape[-2]
    t = seq_len * dil
    tq = min(512, seq_len)
    sub = tq // BAND
    if dil == 1:
        cur = pl.BlockSpec((None, tq, DSW_GW), lambda b, j, r: (b, j, 0))
        prev = pl.BlockSpec((None, BAND, DSW_GW), lambda b, j, r: (b, jnp.maximum(j * sub - 1, 0), 0))
    else:
        cur = pl.BlockSpec((None, None, tq, DSW_GW), lambda b, j, r: (b, r, j, 0))
        prev = pl.BlockSpec((None, None, BAND, DSW_GW), lambda b, j, r: (b, r, jnp.maximum(j * sub - 1, 0), 0))
    n_slab = DSW_GW // LANES
    out = pl.BlockSpec((None, n_slab, tq * dil, LANES), lambda b, j, r: (b, 0, j, 0))
    return pl.pallas_call(
        functools.partial(_dsw_prompt_kernel, dil=dil),
        grid=(nb, seq_len // tq, dil),
        in_specs=[cur, prev, cur, prev, cur],
        out_specs=[out, out],
        out_shape=[jax.ShapeDtypeStruct((nb, n_slab, t, LANES), F32)] * 2,
        compiler_params=_params(2, 1),
        name=f"dsw_prompt_d{dil}",
    )(q, k, k, v, v)


def _dsw_sample_kernel(q_ref, kn_ref, vn_ref, c_ref, o_ref, lse_ref, co_ref, *, seq, dil):
    per8 = 8 // seq
    win = c_ref.shape[2]
    n_rows = DSW_HEADS * 8
    lane = _iota((8, LANES), 1)
    head_of_lane = _iota((8, DSW_GW), 1) // DSW_HEAD_DIM
    row8 = _iota((8, 1), 0)
    r = _iota((n_rows, 1), 0)
    r_step = r % seq
    r_seq = (r % 8) // seq
    key = _iota((n_rows, win), 1)
    cache_ok = ((key % dil) == (r_step % dil)) & (key >= r_step)
    c128 = _iota((n_rows, LANES), 1)
    new_ok = ((c128 < 8) & ((c128 // seq) == r_seq) & ((c128 % seq) <= r_step)
              & (((r_step - c128 % seq) % dil) == 0))
    pad = jnp.zeros((LANES - 8, DSW_GW), BF16)

    def by_head(x):
        out = x[(DSW_HEADS - 1) * 8:DSW_HEADS * 8]
        for h in range(DSW_HEADS - 2, -1, -1):
            out = jnp.where(head_of_lane == h, x[h * 8:(h + 1) * 8], out)
        return out

    lane_sq = _iota((LANES, LANES), 1)
    p_row = _iota((8, LANES), 0)
    for grp in range(q_ref.shape[0] // 8):
        r8 = slice(grp * 8, (grp + 1) * 8)
        q8 = q_ref[r8, :]
        qrows = jnp.concatenate([jnp.where(head_of_lane == h, q8, 0.0) for h in range(DSW_HEADS)],
                                axis=0).astype(BF16)
        kn8 = kn_ref[r8, :]
        vn8 = vn_ref[r8, :]
        kn_t = jnp.concatenate([kn8.astype(BF16), pad], axis=0)
        vn_t = jnp.concatenate([vn8.astype(BF16), pad], axis=0)
        scn = jnp.where(new_ok, _nt(qrows, kn_t), -jnp.inf)
        m_new = jnp.max(scn, axis=-1, keepdims=True)
        o_p = jnp.zeros((8, DSW_GW), F32)
        l_p = jnp.zeros((8, DSW_GW), F32)
        for j in range(per8):
            b = grp * per8 + j
            kt = c_ref[b, 0:DSW_GW, :].astype(BF16)
            vt = c_ref[b, DSW_GW:2 * DSW_GW, :].astype(BF16)
            sc = jnp.where(cache_ok, _nn(qrows, kt), -jnp.inf)
            m = jnp.maximum(jnp.max(sc, axis=-1, keepdims=True), m_new)
            e = jnp.exp(sc - m)
            en = jnp.exp(scn - m)
            den = jnp.sum(e, axis=-1, keepdims=True) + jnp.sum(en, axis=-1, keepdims=True)
            o = _nt((e / den).astype(BF16), vt) + _nn((en / den).astype(BF16), vn_t)
            lse = jnp.broadcast_to(m + jnp.log(den), (n_rows, DSW_GW))
            mine = (row8 // seq) == j
            o_p = jnp.where(mine, by_head(o), o_p)
            l_p = jnp.where(mine, by_head(lse), l_p)
        o_ref[r8, :] = o_p
        lse_ref[r8, :] = l_p

        hi, mid, lo = _split3(jnp.concatenate([kn8, vn8], axis=1))
        for j in range(per8):
            b = grp * per8 + j
            place = jnp.where(((p_row // seq) == j) & (lane == LANES - seq + p_row % seq), 1.0, 0.0).astype(BF16)
            new_cols = _tn(hi, place) + _tn(mid, place) + _tn(lo, place)
            for blk in range(2 * DSW_GW // LANES):
                rows = slice(blk * LANES, (blk + 1) * LANES)
                rolled = pltpu.roll(c_ref[b, rows, :], win - seq, 1)
                if win > LANES:
                    co_ref[b, rows, 0:win - LANES] = rolled[:, 0:win - LANES]
                co_ref[b, rows, win - LANES:win] = jnp.where(lane_sq < LANES - seq, rolled[:, win - LANES:win],
                                                             new_cols[rows])


def _dsw_sample_call(q32, k32, v32, cache, g, seq):
    n_seq, win = cache.shape[0], cache.shape[1]
    per8 = 8 // seq
    groups = max(1, min(8, 4 * 512 // win))
    n_blk = per8 * groups
    view = jnp.transpose(cache, (0, 2, 3, 4, 1)).reshape(n_seq, 2 * DSW_GW, win)
    tok = pl.BlockSpec((None, 8 * groups, DSW_GW), lambda i: (0, i, g))
    tok_out = pl.BlockSpec((None, 8 * groups, DSW_GW), lambda i: (0, i, 0))
    cspec = pl.BlockSpec((n_blk, 2 * DSW_GW, win), lambda i: (i, 0, 0))
    return dict(
        steps=n_seq // n_blk, args=(q32, k32, v32, view),
        in_specs=[tok, tok, tok, cspec],
        out_specs=[tok_out, tok_out, cspec],
        out_shape=[jax.ShapeDtypeStruct((1, n_seq * seq, DSW_GW), F32)] * 2
                  + [jax.ShapeDtypeStruct(view.shape, F32)])


def _dsw_sample_finish(o, lse, new, cache_shape):
    n_seq, win = cache_shape[0], cache_shape[1]
    new = jnp.transpose(new.reshape(n_seq, 2, DSW_HEADS, DSW_HEAD_DIM, win), (0, 4, 1, 2, 3))
    n_slab = DSW_GW // LANES
    slabs = lambda a: jnp.transpose(a.reshape(1, -1, n_slab, LANES), (0, 2, 1, 3))
    return slabs(o), slabs(lse), new


def _dsw_sample(q32, k32, v32, cache, g, dil, seq):
    c = _dsw_sample_call(q32, k32, v32, cache, g, seq)
    o, lse, new = pl.pallas_call(
        functools.partial(_dsw_sample_kernel, seq=seq, dil=dil),
        grid=(c["steps"],), in_specs=c["in_specs"], out_specs=c["out_specs"], out_shape=c["out_shape"],
        compiler_params=_params(1),
        name=f"dsw_sample_d{dil}",
    )(*c["args"])
    return _dsw_sample_finish(o, lse, new, cache.shape)


def _merge_kernel(oa_ref, o0_ref, o1_ref, o2_ref, l0_ref, l1_ref, l2_ref, ga_ref, gb_ref, x_ref,
                  g1_ref, sc_ref, sh_ref, n2_ref, wpa_ref, wpb_ref, wo_ref, x1_ref, h2_ref):
    ob = []
    for slab in range(DSW_GW // LANES):
        l0, l1, l2 = l0_ref[slab], l1_ref[slab], l2_ref[slab]
        m = jnp.maximum(jnp.maximum(l0, l1), l2)
        w0, w1, w2 = jnp.exp(l0 - m), jnp.exp(l1 - m), jnp.exp(l2 - m)
        den = w0 + w1 + w2
        ob.append((w0 / den) * o0_ref[slab] + (w1 / den) * o1_ref[slab] + (w2 / den) * o2_ref[slab])
    ob = jnp.concatenate(ob, axis=1).astype(BF16)
    merged = (jax.nn.sigmoid(ga_ref[...].astype(F32)) * _nn(oa_ref[...], wpa_ref[...])
              + jax.nn.sigmoid(gb_ref[...].astype(F32)) * _nn(ob, wpb_ref[...]))
    x1 = x_ref[...] + g1_ref[...] * _nn(merged.astype(BF16), wo_ref[...])
    x1_ref[...] = x1
    h2_ref[...] = (_rms(x1, n2_ref[...]) * (1.0 + sc_ref[...]) + sh_ref[...]).astype(BF16)


def _merge(oa, og, lg, ga, gb, x, g1, sc2, sh2, n2, wpa, wpb, wo, tm):
    nb, t, d = x.shape
    tok = lambda n: pl.BlockSpec((None, tm, n), lambda b, i: (b, i, 0))
    slab = pl.BlockSpec((None, DSW_GW // LANES, tm, LANES), lambda b, i: (b, 0, i, 0))
    return pl.pallas_call(
        _merge_kernel,
        grid=(nb, t // tm),
        in_specs=[tok(1024)] + [slab] * 6 + [tok(d), tok(d), tok(d),
                  _mod_spec(g1, tm), _mod_spec(sc2, tm), _mod_spec(sh2, tm), _resident((1, d)),
                  _resident(wpa.shape), _resident(wpb.shape), _resident(wo.shape)],
        out_specs=[tok(d), tok(d)],
        out_shape=[jax.ShapeDtypeStruct((nb, t, d), F32), jax.ShapeDtypeStruct((nb, t, d), BF16)],
        compiler_params=_params(2),
        name="merge_outproj",
    )(oa, *og, *lg, ga, gb, x, g1, sc2, sh2, n2, wpa, wpb, wo)


def _ffn_kernel(h_ref, x_ref, g2_ref, nf_ref, wu_ref, wd_ref, y_ref, *, final_norm, n_split):
    h = h_ref[...]
    d_ff = wd_ref.shape[0]
    step = d_ff // n_split
    acc = None
    for j in range(n_split):
        u1 = _nn(h, wu_ref[:, j * step:(j + 1) * step])
        u2 = _nn(h, wu_ref[:, d_ff + j * step:d_ff + (j + 1) * step])
        a = (u1 * jax.nn.sigmoid(u1) * u2).astype(BF16)
        part = _nn(a, wd_ref[j * step:(j + 1) * step, :])
        acc = part if acc is None else acc + part
    x2 = x_ref[...] + g2_ref[...] * acc
    y_ref[...] = _rms(x2, nf_ref[...]) if final_norm else x2


def _ffn(h2, x1, g2, nf, wu, wd, tm, final_norm):
    nb, t, d = x1.shape
    tok = lambda n: pl.BlockSpec((None, tm, n), lambda b, i: (b, i, 0))
    return pl.pallas_call(
        functools.partial(_ffn_kernel, final_norm=final_norm, n_split=11),
        grid=(nb, t // tm),
        in_specs=[tok(d), tok(d), _mod_spec(g2, tm), _resident((1, d)),
                  _resident(wu.shape), _resident(wd.shape)],
        out_specs=tok(d),
        out_shape=jax.ShapeDtypeStruct((nb, t, d), F32),
        compiler_params=_params(2),
        name="ffn",
    )(h2, x1, g2, nf, wu, wd)


def _rope_tables(pos):
    half = DSW_HEAD_DIM // 2
    inv = ROPE_THETA ** (-jnp.arange(half, dtype=F32) / half)
    ang = pos[:, None] * inv[None, :]
    cos, sin = jnp.cos(ang), jnp.sin(ang)
    reps = LANES // half
    cosf = jnp.tile(cos, (1, reps))
    sign = jnp.tile(jnp.concatenate([-jnp.ones((half,), F32), jnp.ones((half,), F32)]), LANES // DSW_HEAD_DIM)
    return cosf, jnp.tile(sin, (1, reps)) * sign[None, :]


def _layer_weights(w_in, b_in, w_alpha2, b_alpha2, w_proj_a, w_proj_b, w_out, w_up, w_down):
    bf = lambda a: a.astype(BF16)
    row = lambda a: a.reshape(1, -1)
    o_glr, o_dq, o_ga = 3072, 3088, 5392
    pad_r = LANES - GLA_RANK
    return dict(
        wa=bf(w_in[:, :o_glr]), ba=row(b_in[:o_glr]),
        wg=bf(jnp.pad(w_in[:, o_glr:o_dq], ((0, 0), (0, pad_r)))), bg=row(jnp.pad(b_in[o_glr:o_dq], (0, pad_r))),
        w2=bf(jnp.pad(w_alpha2, ((0, pad_r), (0, 0)))), b2=row(b_alpha2),
        wb=bf(w_in[:, o_dq:o_ga]), bb=row(b_in[o_dq:o_ga]),
        wc=bf(w_in[:, o_ga:]), bc=row(b_in[o_ga:]),
        wpa=bf(w_proj_a), wpb=bf(w_proj_b), wo=bf(w_out), wu=bf(w_up), wd=bf(w_down))


def _kv_unstack(kvt, keep):
    nb, _, width = kvt.shape
    kv = kvt[:, :, width - keep:].reshape(nb, 2, DSW_HEADS, DSW_HEAD_DIM, keep)
    return jnp.transpose(kv, (0, 4, 1, 2, 3))


def kernel(x_prompt, x_sample, state_gla, cache_kv_w128, cache_kv_w512, cache_kv_w2048, c_prompt, c_sample,
           norm1_g, norm2_g, w_ada, b_ada, w_in, b_in, w_alpha2, b_alpha2, gla_norm_g, w_proj_a, w_proj_b,
           w_out, w_up, w_down, normf_g):
    depth = w_ada.shape[0]
    nb, t, d = x_prompt.shape
    n_seq, seq, _ = x_sample.shape
    assert 8 % seq == 0 and seq >= 3, "sample kernels pack whole sequences into 8-row groups"
    past = PAST_LEN
    caches = (cache_kv_w128, cache_kv_w512, cache_kv_w2048)

    cos_p, sin_p = _rope_tables(jnp.arange(t, dtype=F32))
    cos_s, sin_s = _rope_tables(jnp.tile(past + jnp.arange(seq, dtype=F32), n_seq))

    n_c = nb + n_seq
    pad_c = (-n_c) % 8
    c_all = jnp.pad(jnp.concatenate([c_prompt, c_sample], axis=0), ((0, pad_c), (0, 0)))

    xp = x_prompt
    xs = x_sample.reshape(1, n_seq * seq, d)
    row = lambda a: a.reshape(1, -1)
    sp_l, kvp_l, ss_l, kvs_l = [], [], [], []
    for l in range(depth):
        w = _layer_weights(w_in[l], b_in[l], w_alpha2[l], b_alpha2[l], w_proj_a[l], w_proj_b[l],
                           w_out[l], w_up[l], w_down[l])
        mod = _ada(c_all, w_ada[l], b_ada[l])
        mod_p = [mod[:nb, i * d:(i + 1) * d].reshape(nb, 1, d) for i in range(6)]
        mod_s = [jnp.repeat(mod[nb:nb + n_seq, i * d:(i + 1) * d], seq, axis=0).reshape(1, n_seq * seq, d)
                 for i in range(6)]
        last = l == depth - 1

        sh1_s, sc1_s, g1_s, sh2_s, sc2_s, g2_s = mod_s
        gq_s, gk_s, gv_s, gr_s, la_s, ga_s, gb_s, k32, v32, q32 = _inproj(
            xs, sc1_s, sh1_s, row(norm1_g[l]), cos_s, sin_s, w, tm=256, fold=False)
        sh1, sc1, g1, sh2, sc2, g2 = mod_p
        gq, gk, gv, gr, la, ga, gb, *dsw_p = _inproj(
            xp, sc1, sh1, row(norm1_g[l]), cos_p, sin_p, w, tm=512, fold=True)
        qkv, kvt = dsw_p[:3 * len(DSW_GROUPS)], dsw_p[3 * len(DSW_GROUPS):]

        g_big = len(DSW_GROUPS) - 1
        dil_big = DSW_GROUPS[g_big][1]
        gla_c = _gla_prompt_call(gq, gk, gv, la, gr, row(gla_norm_g[l]), tt=256)
        dsw_c = _dsw_sample_call(q32, k32, v32, caches[g_big][l], g_big, seq)
        sample_dsw = [None] * len(DSW_GROUPS)
        if gla_c["steps"] == dsw_c["steps"]:
            oa, st, o_b, lse_b, new_b = _gla_prompt_dsw_sample(gla_c, dsw_c, seq, dil_big)
            sample_dsw[g_big] = _dsw_sample_finish(o_b, lse_b, new_b, caches[g_big][l].shape)
        else:
            oa, st = _gla_prompt(gq, gk, gv, la, gr, row(gla_norm_g[l]), tt=256)
        for g, (_, dil) in enumerate(DSW_GROUPS):
            if sample_dsw[g] is None:
                sample_dsw[g] = _dsw_sample(q32, k32, v32, caches[g][l], g, dil, seq)
        og_s, lg_s, new_kv = zip(*sample_dsw)
        oa_s, s_new = _gla_sample(gq_s, gk_s, gv_s, la_s, gr_s, row(gla_norm_g[l]), state_gla[l], seq, bb=8)
        og, lg = zip(*[_dsw_prompt(*qkv[3 * g:3 * g + 3], dil) for g, (_, dil) in enumerate(DSW_GROUPS)])

        x1, h2 = _merge(oa, og, lg, ga, gb, xp, g1, sc2, sh2, row(norm2_g[l]), w["wpa"], w["wpb"], w["wo"], tm=512)
        xp = _ffn(h2, x1, g2, row(normf_g), w["wu"], w["wd"], tm=512, final_norm=last)
        x1, h2 = _merge(oa_s, og_s, lg_s, ga_s, gb_s, xs, g1_s, sc2_s, sh2_s, row(norm2_g[l]),
                        w["wpa"], w["wpb"], w["wo"], tm=256)
        xs = _ffn(h2, x1, g2_s, row(normf_g), w["wu"], w["wd"], tm=256, final_norm=last)

        sp_l.append(jnp.swapaxes(st, 2, 3))
        kvp_l.append(tuple(_kv_unstack(kvt[g], min(win, t)) for g, (win, _) in enumerate(DSW_GROUPS)))
        ss_l.append(s_new)
        kvs_l.append(new_kv)

    y_prompt = xp
    y_sample = xs.reshape(n_seq, seq, d)
    stack = lambda items: jnp.stack(list(items))
    return (y_prompt, y_sample, stack(sp_l),
            stack(kv[0] for kv in kvp_l), stack(kv[1] for kv in kvp_l), stack(kv[2] for kv in kvp_l),
            stack(ss_l),
            stack(kv[0] for kv in kvs_l), stack(kv[1] for kv in kvs_l), stack(kv[2] for kv in kvs_l))
```

```python
import functools

import jax
import jax.numpy as jnp
from jax import lax
from jax.experimental import pallas as pl
from jax.experimental.pallas import tpu as pltpu

F32 = jnp.float32
BF16 = jnp.bfloat16

EPS = 1e-6
GLA_HEADS = 4
GLA_DK = 128
GLA_DV = 256
GLA_RANK = 16
GLA_TAU = 16.0
GLA_CHUNK = 64
GLA_SCALE = GLA_DK ** -0.5
DSW_GROUPS = ((128, 1), (512, 4), (2048, 16))
DSW_HEADS = 4
DSW_HEAD_DIM = 64
DSW_SCALE = DSW_HEAD_DIM ** -0.5
DSW_GW = DSW_HEADS * DSW_HEAD_DIM
BAND = 128
ROPE_THETA = 10000.0
PAST_LEN = 8192
LANES = 128
VMEM_LIMIT = 56 * 1024 * 1024


def _nn(a, b):
    return jnp.dot(a, b, preferred_element_type=F32)


def _nt(a, b):
    return lax.dot_general(a, b, (((1,), (1,)), ((), ())), preferred_element_type=F32)


def _tn(a, b):
    return lax.dot_general(a, b, (((0,), (0,)), ((), ())), preferred_element_type=F32)


def _split3(x):
    hi = x.astype(BF16)
    r1 = x - hi.astype(F32)
    mid = r1.astype(BF16)
    lo = (r1 - mid.astype(F32)).astype(BF16)
    return hi, mid, lo


def _iota(shape, dim):
    return lax.broadcasted_iota(jnp.int32, shape, dim)


def _rms(x, g):
    return x * lax.rsqrt(jnp.mean(x * x, axis=-1, keepdims=True) + EPS) * g


def _params(n_parallel, n_arbitrary=0):
    sem = ("parallel",) * n_parallel + ("arbitrary",) * n_arbitrary
    return pltpu.CompilerParams(dimension_semantics=sem, vmem_limit_bytes=VMEM_LIMIT)


def _resident(shape):
    nd = len(shape)
    return pl.BlockSpec(shape, lambda *_: (0,) * nd, pipeline_mode=pl.Buffered(1))


def _ada_kernel(c_ref, w_ref, b_ref, o_ref):
    c = c_ref[...]
    a = (c * jax.nn.sigmoid(c)).astype(BF16)
    o_ref[...] = _nn(a, w_ref[...].astype(BF16)) + b_ref[...]


def _ada(c_all, w_ada, b_ada):
    n, d = c_all.shape
    ncol = w_ada.shape[1]
    tn = 1536
    return pl.pallas_call(
        _ada_kernel,
        grid=(ncol // tn,),
        in_specs=[pl.BlockSpec((n, d), lambda j: (0, 0)),
                  pl.BlockSpec((d, tn), lambda j: (0, j)),
                  pl.BlockSpec((1, tn), lambda j: (0, j))],
        out_specs=pl.BlockSpec((n, tn), lambda j: (0, j)),
        out_shape=jax.ShapeDtypeStruct((n, ncol), F32),
        compiler_params=_params(1),
        name="ada_mod",
    )(c_all, w_ada, b_ada.reshape(1, ncol))


def _rope(x, cosf, sins, first_half):
    rot = jnp.where(first_half, pltpu.roll(x, LANES - 32, 1), pltpu.roll(x, 32, 1))
    return x * cosf + rot * sins


def _inproj_kernel(x_ref, sc_ref, sh_ref, g_ref, cos_ref, sin_ref,
                   wa_ref, ba_ref, wg_ref, bg_ref, w2_ref, b2_ref, wb_ref, bb_ref, wc_ref, bc_ref,
                   gq_ref, gk_ref, gv_ref, gr_ref, la_ref, ga_ref, gb_ref, *rest, fold):
    x = x_ref[...]
    h = (_rms(x, g_ref[...]) * (1.0 + sc_ref[...]) + sh_ref[...]).astype(BF16)

    gq_ref[...] = (_nn(h, wa_ref[:, 0:512]) + ba_ref[:, 0:512]).astype(gq_ref.dtype)
    gk_ref[...] = (_nn(h, wa_ref[:, 512:1024]) + ba_ref[:, 512:1024]).astype(gk_ref.dtype)
    gv_ref[...] = (_nn(h, wa_ref[:, 1024:2048]) + ba_ref[:, 1024:2048]).astype(gv_ref.dtype)
    gr_ref[...] = (_nn(h, wa_ref[:, 2048:3072]) + ba_ref[:, 2048:3072]).astype(gr_ref.dtype)

    glr = (_nn(h, wg_ref[...]) + bg_ref[...]).astype(BF16)
    z = _nn(glr, w2_ref[...]) + b2_ref[...]
    la_ref[...] = jax.nn.log_sigmoid(z) * (1.0 / GLA_TAU)

    cosf = cos_ref[...]
    sins = sin_ref[...]
    first_half = (_iota(cosf.shape, 1) % DSW_HEAD_DIM) < (DSW_HEAD_DIM // 2)
    tm = x.shape[0]
    per_group = DSW_GW // LANES
    width = 3 * DSW_GW

    def proj(off, g):
        cols = slice(off + g * DSW_GW, off + (g + 1) * DSW_GW)
        full = _nn(h, wb_ref[:, cols]) + bb_ref[:, cols]
        return [full[:, s * LANES:(s + 1) * LANES] for s in range(per_group)]

    for g, (_, dil) in enumerate(DSW_GROUPS):
        qs = [_rope(a, cosf, sins, first_half) * DSW_SCALE for a in proj(0, g)]
        ks = [_rope(a, cosf, sins, first_half) for a in proj(width, g)]
        vs = proj(2 * width, g)
        if fold:
            kvt_ref = rest[3 * len(DSW_GROUPS) + g]
            kvt_ref[0:DSW_GW, :] = jnp.concatenate(ks, axis=1).T
            kvt_ref[DSW_GW:2 * DSW_GW, :] = jnp.concatenate(vs, axis=1).T

        for slab in range(per_group):
            cols = slice(g * DSW_GW + slab * LANES, g * DSW_GW + (slab + 1) * LANES)
            lanes = slice(slab * LANES, (slab + 1) * LANES)
            if not fold:
                rest[0][:, cols] = ks[slab]
                rest[1][:, cols] = vs[slab]
                rest[2][:, cols] = qs[slab]
                continue
            scratch = rest[-1]
            for which, val in enumerate((qs[slab], ks[slab], vs[slab])):
                out_ref = rest[3 * g + which]
                if dil == 1:
                    out_ref[:, lanes] = val.astype(BF16)
                else:
                    scratch[which] = val
                    for r in range(dil):
                        out_ref[r, :, lanes] = scratch[which, pl.ds(r, tm // dil, stride=dil), :].astype(BF16)

    ga_ref[...] = (_nn(h, wc_ref[:, 0:1024]) + bc_ref[:, 0:1024]).astype(ga_ref.dtype)
    gb_ref[...] = (_nn(h, wc_ref[:, 1024:2048]) + bc_ref[:, 1024:2048]).astype(gb_ref.dtype)


def _mod_spec(arr, tm):
    if arr.shape[1] == 1:
        return pl.BlockSpec((None, 1, arr.shape[2]), lambda b, i: (b, 0, 0))
    return pl.BlockSpec((None, tm, arr.shape[2]), lambda b, i: (b, i, 0))


def _inproj(x, sc, sh, g, cos_t, sin_t, w, tm, fold):
    nb, t, d = x.shape
    tok = lambda n: pl.BlockSpec((None, tm, n), lambda b, i: (b, i, 0))
    out_cols = (512, 512, 1024, 1024, 512, 1024, 1024)
    out_dt = (BF16, BF16, BF16, BF16, F32, BF16, BF16)
    out_specs = [tok(n) for n in out_cols]
    out_shape = [jax.ShapeDtypeStruct((nb, t, n), dt) for n, dt in zip(out_cols, out_dt)]
    scratch = []
    if fold:
        for _, dil in DSW_GROUPS:
            for _ in range(3):
                if dil == 1:
                    out_specs.append(tok(DSW_GW))
                    out_shape.append(jax.ShapeDtypeStruct((nb, t, DSW_GW), BF16))
                else:
                    out_specs.append(pl.BlockSpec((None, dil, tm // dil, DSW_GW), lambda b, i: (b, 0, i, 0)))
                    out_shape.append(jax.ShapeDtypeStruct((nb, dil, t // dil, DSW_GW), BF16))
        for win, _ in DSW_GROUPS:
            width = min(max(win, tm), t)
            first = (t - width) // tm
            out_specs.append(pl.BlockSpec((None, 2 * DSW_GW, tm),
                                          lambda b, i, first=first: (b, 0, jnp.maximum(i - first, 0))))
            out_shape.append(jax.ShapeDtypeStruct((nb, 2 * DSW_GW, width), F32))
        scratch = [pltpu.VMEM((3, tm, LANES), F32)]
    else:
        for _ in range(3):
            out_specs.append(tok(3 * DSW_GW))
            out_shape.append(jax.ShapeDtypeStruct((nb, t, 3 * DSW_GW), F32))
    weights = (w["wa"], w["ba"], w["wg"], w["bg"], w["w2"], w["b2"], w["wb"], w["bb"], w["wc"], w["bc"])
    return pl.pallas_call(
        functools.partial(_inproj_kernel, fold=fold),
        grid=(nb, t // tm),
        in_specs=[tok(d), _mod_spec(sc, tm), _mod_spec(sh, tm), _resident((1, d)),
                  pl.BlockSpec((tm, LANES), lambda b, i: (i, 0)),
                  pl.BlockSpec((tm, LANES), lambda b, i: (i, 0))]
                 + [_resident(a.shape) for a in weights],
        out_specs=out_specs,
        out_shape=out_shape,
        scratch_shapes=scratch,
        compiler_params=_params(1, 1),
        name="inproj",
    )(x, sc, sh, g, cos_t, sin_t, *weights)


def _gla_local(gq_ref, gk_ref, la_ref, chunk):
    la = la_ref[...]
    tt = la.shape[0]
    r = _iota((tt, tt), 0)
    c = _iota((tt, tt), 1)
    same = (r // chunk) == (c // chunk)
    tri = jnp.where(same & (c <= r), 1.0, 0.0).astype(BF16)
    ones = jnp.where(same, 1.0, 0.0).astype(BF16)
    hi, mid, lo = _split3(la)
    b = _nn(tri, hi) + _nn(tri, mid) + _nn(tri, lo)
    bl = _nn(ones, hi) + _nn(ones, mid) + _nn(ones, lo)
    gq = gq_ref[...].astype(F32)
    gk = gk_ref[...].astype(F32)
    qg = (gq * GLA_SCALE * jnp.exp(b)).astype(BF16)
    kd = (gk * jnp.exp(-b)).astype(BF16)
    kl = (gk * jnp.exp(bl - b)).astype(BF16)
    causal = same & (c <= r)
    return qg, kd, kl, jnp.exp(bl), causal


def _gla_finish(o, gr, g):
    return (_rms(o, g) * (gr * jax.nn.sigmoid(gr))).astype(BF16)


def _gla_prompt_body(first_tile, gq_ref, gk_ref, gv_ref, la_ref, gr_ref, g_ref, o_ref, st_ref):
    @pl.when(first_tile)
    def _():
        st_ref[...] = jnp.zeros_like(st_ref)

    qg, kd, kl, dec, causal = _gla_local(gq_ref, gk_ref, la_ref, GLA_CHUNK)
    tt = qg.shape[0]
    for h in range(GLA_HEADS):
        kc = slice(h * GLA_DK, (h + 1) * GLA_DK)
        vc = slice(h * GLA_DV, (h + 1) * GLA_DV)
        v = gv_ref[:, vc].astype(BF16)
        att = jnp.where(causal, _nt(qg[:, kc], kd[:, kc]), 0.0).astype(BF16)
        intra = _nn(att, v)
        st = st_ref[h]
        inter = []
        for ci in range(tt // GLA_CHUNK):
            rows = slice(ci * GLA_CHUNK, (ci + 1) * GLA_CHUNK)
            inter.append(_nt(qg[rows, kc], st.astype(BF16)))
            st = dec[ci * GLA_CHUNK:ci * GLA_CHUNK + 1, kc] * st + _tn(v[rows], kl[rows, kc])
        st_ref[h] = st
        o = intra + jnp.concatenate(inter, axis=0)
        o_ref[:, vc] = _gla_finish(o, gr_ref[:, vc].astype(F32), g_ref[...])


def _gla_prompt_call(gq, gk, gv, la, gr, g, tt):
    nb, t, _ = gq.shape
    tiles = t // tt
    tok = lambda n: pl.BlockSpec((None, tt, n), lambda i: (i // tiles, i % tiles, 0))
    return dict(
        steps=nb * tiles, tiles=tiles, args=(gq, gk, gv, la, gr, g),
        in_specs=[tok(512), tok(512), tok(1024), tok(512), tok(1024), _resident((1, GLA_DV))],
        out_specs=[tok(1024),
                   pl.BlockSpec((None, GLA_HEADS, GLA_DV, GLA_DK), lambda i: (i // tiles, 0, 0, 0))],
        out_shape=[jax.ShapeDtypeStruct((nb, t, 1024), BF16),
                   jax.ShapeDtypeStruct((nb, GLA_HEADS, GLA_DV, GLA_DK), F32)])


def _gla_prompt_kernel(*refs, tiles):
    _gla_prompt_body(pl.program_id(0) % tiles == 0, *refs)


def _gla_prompt(gq, gk, gv, la, gr, g, tt):
    c = _gla_prompt_call(gq, gk, gv, la, gr, g, tt)
    return pl.pallas_call(
        functools.partial(_gla_prompt_kernel, tiles=c["tiles"]),
        grid=(c["steps"],), in_specs=c["in_specs"], out_specs=c["out_specs"], out_shape=c["out_shape"],
        compiler_params=_params(0, 1),
        name="gla_prompt",
    )(*c["args"])


def _gla_prompt_dsw_sample_kernel(*refs, tiles, seq, dil):
    _gla_prompt_body(pl.program_id(0) % tiles == 0, *refs[0:6], *refs[10:12])
    _dsw_sample_kernel(*refs[6:10], *refs[12:15], seq=seq, dil=dil)


def _gla_prompt_dsw_sample(gla, dsw, seq, dil):
    assert gla["steps"] == dsw["steps"]
    return pl.pallas_call(
        functools.partial(_gla_prompt_dsw_sample_kernel, tiles=gla["tiles"], seq=seq, dil=dil),
        grid=(gla["steps"],),
        in_specs=gla["in_specs"] + dsw["in_specs"],
        out_specs=gla["out_specs"] + dsw["out_specs"],
        out_shape=gla["out_shape"] + dsw["out_shape"],
        compiler_params=_params(0, 1),
        name=f"gla_prompt_dsw_sample_d{dil}",
    )(*gla["args"], *dsw["args"])


def _gla_sample_kernel(gq_ref, gk_ref, gv_ref, la_ref, gr_ref, g_ref, s_ref, o_ref, so_ref, *, seq):
    qg, kd, kl, dec, causal = _gla_local(gq_ref, gk_ref, la_ref, seq)
    rows_total = qg.shape[0]
    per8 = 8 // seq
    row8 = _iota((8, 1), 0)
    for h in range(GLA_HEADS):
        kc = slice(h * GLA_DK, (h + 1) * GLA_DK)
        vc = slice(h * GLA_DV, (h + 1) * GLA_DV)
        v = gv_ref[:, vc].astype(BF16)
        att = jnp.where(causal, _nt(qg[:, kc], kd[:, kc]), 0.0).astype(BF16)
        intra = _nn(att, v)
        inter = []
        for p in range(rows_total // 8):
            rows = slice(p * 8, (p + 1) * 8)
            d_hi, d_mid, d_lo = _split3(dec[rows, kc])
            inter_p = jnp.zeros((8, GLA_DV), F32)
            for j in range(per8):
                b = p * per8 + j
                r0 = j * seq
                s0 = s_ref[b, h]
                mine = (row8 >= r0) & (row8 < r0 + seq)
                inter_p = jnp.where(mine, _nn(qg[rows, kc], s0.astype(BF16)), inter_p)
                dl = jnp.where(row8 == r0, d_hi, jnp.where(row8 == r0 + 1, d_mid,
                               jnp.where(row8 == r0 + 2, d_lo, jnp.zeros_like(d_lo))))
                e = jnp.where((row8 >= r0) & (row8 < r0 + 3), 1.0, 0.0).astype(BF16)
                dec_b = _tn(dl, jnp.broadcast_to(e, (8, GLA_DV)))
                upd = _tn(jnp.where(mine, kl[rows, kc], jnp.zeros_like(kl[rows, kc])), v[rows])
                so_ref[b, h] = dec_b * s0 + upd
            inter.append(inter_p)
        o = intra + jnp.concatenate(inter, axis=0)
        o_ref[:, vc] = _gla_finish(o, gr_ref[:, vc].astype(F32), g_ref[...])


def _gla_sample(gq, gk, gv, la, gr, g, s0, seq, bb):
    n_seq = s0.shape[0]
    rows = bb * seq
    tok = lambda n: pl.BlockSpec((None, rows, n), lambda i: (0, i, 0))
    st = pl.BlockSpec((bb, GLA_HEADS, GLA_DK, GLA_DV), lambda i: (i, 0, 0, 0))
    return pl.pallas_call(
        functools.partial(_gla_sample_kernel, seq=seq),
        grid=(n_seq // bb,),
        in_specs=[tok(512), tok(512), tok(1024), tok(512), tok(1024), _resident((1, GLA_DV)), st],
        out_specs=[tok(1024), st],
        out_shape=[jax.ShapeDtypeStruct((1, n_seq * seq, 1024), BF16),
                   jax.ShapeDtypeStruct(s0.shape, F32)],
        compiler_params=_params(1),
        name="gla_sample",
    )(gq, gk, gv, la, gr, g, s0)


def _dsw_prompt_kernel(q_ref, kp_ref, kc_ref, vp_ref, vc_ref, o_ref, lse_ref, *, dil):
    qb = q_ref.shape[0]
    res = pl.program_id(2)
    first_key = jnp.where(pl.program_id(1) == 0, BAND, 0)
    qi = _iota((BAND, 2 * BAND), 0) + BAND
    ki = _iota((BAND, 2 * BAND), 1)
    band = (qi - ki >= 0) & (qi - ki <= BAND)
    lane = _iota((BAND, LANES), 1)
    for s in range(qb // BAND):
        rows = slice(s * BAND, (s + 1) * BAND)
        if s == 0:
            valid = band & (ki >= first_key)
        else:
            valid = band
        if dil == 1:
            tok_rows = rows
        else:
            tok_rows = pl.ds(s * BAND * dil + res, BAND, stride=dil)
        for hp in range(DSW_GW // LANES):
            cols = slice(hp * LANES, (hp + 1) * LANES)
            qp = q_ref[rows, cols]
            if s == 0:
                kcat = jnp.concatenate([kp_ref[:, cols], kc_ref[0:BAND, cols]], axis=0)
                vcat = jnp.concatenate([vp_ref[:, cols], vc_ref[0:BAND, cols]], axis=0)
            else:
                kcat = kc_ref[(s - 1) * BAND:(s + 1) * BAND, cols]
                vcat = vc_ref[(s - 1) * BAND:(s + 1) * BAND, cols]
            outs, lses = [], []
            for hh in range(LANES // DSW_HEAD_DIM):
                in_head = (lane // DSW_HEAD_DIM) == hh
                sc = _nt(jnp.where(in_head, qp, jnp.zeros_like(qp)), kcat)
                sc = jnp.where(valid, sc, -jnp.inf)
                m = jnp.max(sc, axis=-1, keepdims=True)
                e = jnp.exp(sc - m)
                den = jnp.sum(e, axis=-1, keepdims=True)
                outs.append(_nn((e / den).astype(BF16), vcat))
                lses.append(m + jnp.log(den))
            first = lane < DSW_HEAD_DIM
            o_ref[hp, tok_rows, :] = jnp.where(first, outs[0], outs[1])
            lse_ref[hp, tok_rows, :] = jnp.where(first, lses[0], jnp.broadcast_to(lses[1], (BAND, LANES)))


def _dsw_prompt(q, k, v, dil):
    nb = q.shape[0]
    seq_len = q.shape[-2]
    t = seq_len * dil
    tq = min(512, seq_len)
    sub = tq // BAND
    if dil == 1:
        cur = pl.BlockSpec((None, tq, DSW_GW), lambda b, j, r: (b, j, 0))
        prev = pl.BlockSpec((None, BAND, DSW_GW), lambda b, j, r: (b, jnp.maximum(j * sub - 1, 0), 0))
    else:
        cur = pl.BlockSpec((None, None, tq, DSW_GW), lambda b, j, r: (b, r, j, 0))
        prev = pl.BlockSpec((None, None, BAND, DSW_GW), lambda b, j, r: (b, r, jnp.maximum(j * sub - 1, 0), 0))
    n_slab = DSW_GW // LANES
    out = pl.BlockSpec((None, n_slab, tq * dil, LANES), lambda b, j, r: (b, 0, j, 0))
    return pl.pallas_call(
        functools.partial(_dsw_prompt_kernel, dil=dil),
        grid=(nb, seq_len // tq, dil),
        in_specs=[cur, prev, cur, prev, cur],
        out_specs=[out, out],
        out_shape=[jax.ShapeDtypeStruct((nb, n_slab, t, LANES), F32)] * 2,
        compiler_params=_params(2, 1),
        name=f"dsw_prompt_d{dil}",
    )(q, k, k, v, v)


def _dsw_sample_kernel(q_ref, kn_ref, vn_ref, c_ref, o_ref, lse_ref, co_ref, *, seq, dil):
    per8 = 8 // seq
    win = c_ref.shape[2]
    n_rows = DSW_HEADS * 8
    lane = _iota((8, LANES), 1)
    head_of_lane = _iota((8, DSW_GW), 1) // DSW_HEAD_DIM
    row8 = _iota((8, 1), 0)
    r = _iota((n_rows, 1), 0)
    r_step = r % seq
    r_seq = (r % 8) // seq
    key = _iota((n_rows, win), 1)
    cache_ok = ((key % dil) == (r_step % dil)) & (key >= r_step)
    c128 = _iota((n_rows, LANES), 1)
    new_ok = ((c128 < 8) & ((c128 // seq) == r_seq) & ((c128 % seq) <= r_step)
              & (((r_step - c128 % seq) % dil) == 0))
    pad = jnp.zeros((LANES - 8, DSW_GW), BF16)

    def by_head(x):
        out = x[(DSW_HEADS - 1) * 8:DSW_HEADS * 8]
        for h in range(DSW_HEADS - 2, -1, -1):
            out = jnp.where(head_of_lane == h, x[h * 8:(h + 1) * 8], out)
        return out

    lane_sq = _iota((LANES, LANES), 1)
    p_row = _iota((8, LANES), 0)
    for grp in range(q_ref.shape[0] // 8):
        r8 = slice(grp * 8, (grp + 1) * 8)
        q8 = q_ref[r8, :]
        qrows = jnp.concatenate([jnp.where(head_of_lane == h, q8, 0.0) for h in range(DSW_HEADS)],
                                axis=0).astype(BF16)
        kn8 = kn_ref[r8, :]
        vn8 = vn_ref[r8, :]
        kn_t = jnp.concatenate([kn8.astype(BF16), pad], axis=0)
        vn_t = jnp.concatenate([vn8.astype(BF16), pad], axis=0)
        scn = jnp.where(new_ok, _nt(qrows, kn_t), -jnp.inf)
        m_new = jnp.max(scn, axis=-1, keepdims=True)
        o_p = jnp.zeros((8, DSW_GW), F32)
        l_p = jnp.zeros((8, DSW_GW), F32)
        for j in range(per8):
            b = grp * per8 + j
            kt = c_ref[b, 0:DSW_GW, :].astype(BF16)
            vt = c_ref[b, DSW_GW:2 * DSW_GW, :].astype(BF16)
            sc = jnp.where(cache_ok, _nn(qrows, kt), -jnp.inf)
            m = jnp.maximum(jnp.max(sc, axis=-1, keepdims=True), m_new)
            e = jnp.exp(sc - m)
            en = jnp.exp(scn - m)
            den = jnp.sum(e, axis=-1, keepdims=True) + jnp.sum(en, axis=-1, keepdims=True)
            o = _nt((e / den).astype(BF16), vt) + _nn((en / den).astype(BF16), vn_t)
            lse = jnp.broadcast_to(m + jnp.log(den), (n_rows, DSW_GW))
            mine = (row8 // seq) == j
            o_p = jnp.where(mine, by_head(o), o_p)
            l_p = jnp.where(mine, by_head(lse), l_p)
        o_ref[r8, :] = o_p
        lse_ref[r8, :] = l_p

        hi, mid, lo = _split3(jnp.concatenate([kn8, vn8], axis=1))
        for j in range(per8):
            b = grp * per8 + j
            place = jnp.where(((p_row // seq) == j) & (lane == LANES - seq + p_row % seq), 1.0, 0.0).astype(BF16)
            new_cols = _tn(hi, place) + _tn(mid, place) + _tn(lo, place)
            for blk in range(2 * DSW_GW // LANES):
                rows = slice(blk * LANES, (blk + 1) * LANES)
                rolled = pltpu.roll(c_ref[b, rows, :], win - seq, 1)
                if win > LANES:
                    co_ref[b, rows, 0:win - LANES] = rolled[:, 0:win - LANES]
                co_ref[b, rows, win - LANES:win] = jnp.where(lane_sq < LANES - seq, rolled[:, win - LANES:win],
                                                             new_cols[rows])


def _dsw_sample_call(q32, k32, v32, cache, g, seq):
    n_seq, win = cache.shape[0], cache.shape[1]
    per8 = 8 // seq
    groups = max(1, min(8, 4 * 512 // win))
    n_blk = per8 * groups
    view = jnp.transpose(cache, (0, 2, 3, 4, 1)).reshape(n_seq, 2 * DSW_GW, win)
    tok = pl.BlockSpec((None, 8 * groups, DSW_GW), lambda i: (0, i, g))
    tok_out = pl.BlockSpec((None, 8 * groups, DSW_GW), lambda i: (0, i, 0))
    cspec = pl.BlockSpec((n_blk, 2 * DSW_GW, win), lambda i: (i, 0, 0))
    return dict(
        steps=n_seq // n_blk, args=(q32, k32, v32, view),
        in_specs=[tok, tok, tok, cspec],
        out_specs=[tok_out, tok_out, cspec],
        out_shape=[jax.ShapeDtypeStruct((1, n_seq * seq, DSW_GW), F32)] * 2
                  + [jax.ShapeDtypeStruct(view.shape, F32)])


def _dsw_sample_finish(o, lse, new, cache_shape):
    n_seq, win = cache_shape[0], cache_shape[1]
    new = jnp.transpose(new.reshape(n_seq, 2, DSW_HEADS, DSW_HEAD_DIM, win), (0, 4, 1, 2, 3))
    n_slab = DSW_GW // LANES
    slabs = lambda a: jnp.transpose(a.reshape(1, -1, n_slab, LANES), (0, 2, 1, 3))
    return slabs(o), slabs(lse), new


def _dsw_sample(q32, k32, v32, cache, g, dil, seq):
    c = _dsw_sample_call(q32, k32, v32, cache, g, seq)
    o, lse, new = pl.pallas_call(
        functools.partial(_dsw_sample_kernel, seq=seq, dil=dil),
        grid=(c["steps"],), in_specs=c["in_specs"], out_specs=c["out_specs"], out_shape=c["out_shape"],
        compiler_params=_params(1),
        name=f"dsw_sample_d{dil}",
    )(*c["args"])
    return _dsw_sample_finish(o, lse, new, cache.shape)


def _merge_kernel(oa_ref, o0_ref, o1_ref, o2_ref, l0_ref, l1_ref, l2_ref, ga_ref, gb_ref, x_ref,
                  g1_ref, sc_ref, sh_ref, n2_ref, wpa_ref, wpb_ref, wo_ref, x1_ref, h2_ref):
    ob = []
    for slab in range(DSW_GW // LANES):
        l0, l1, l2 = l0_ref[slab], l1_ref[slab], l2_ref[slab]
        m = jnp.maximum(jnp.maximum(l0, l1), l2)
        w0, w1, w2 = jnp.exp(l0 - m), jnp.exp(l1 - m), jnp.exp(l2 - m)
        den = w0 + w1 + w2
        ob.append((w0 / den) * o0_ref[slab] + (w1 / den) * o1_ref[slab] + (w2 / den) * o2_ref[slab])
    ob = jnp.concatenate(ob, axis=1).astype(BF16)
    merged = (jax.nn.sigmoid(ga_ref[...].astype(F32)) * _nn(oa_ref[...], wpa_ref[...])
              + jax.nn.sigmoid(gb_ref[...].astype(F32)) * _nn(ob, wpb_ref[...]))
    x1 = x_ref[...] + g1_ref[...] * _nn(merged.astype(BF16), wo_ref[...])
    x1_ref[...] = x1
    h2_ref[...] = (_rms(x1, n2_ref[...]) * (1.0 + sc_ref[...]) + sh_ref[...]).astype(BF16)


def _merge(oa, og, lg, ga, gb, x, g1, sc2, sh2, n2, wpa, wpb, wo, tm):
    nb, t, d = x.shape
    tok = lambda n: pl.BlockSpec((None, tm, n), lambda b, i: (b, i, 0))
    slab = pl.BlockSpec((None, DSW_GW // LANES, tm, LANES), lambda b, i: (b, 0, i, 0))
    return pl.pallas_call(
        _merge_kernel,
        grid=(nb, t // tm),
        in_specs=[tok(1024)] + [slab] * 6 + [tok(d), tok(d), tok(d),
                  _mod_spec(g1, tm), _mod_spec(sc2, tm), _mod_spec(sh2, tm), _resident((1, d)),
                  _resident(wpa.shape), _resident(wpb.shape), _resident(wo.shape)],
        out_specs=[tok(d), tok(d)],
        out_shape=[jax.ShapeDtypeStruct((nb, t, d), F32), jax.ShapeDtypeStruct((nb, t, d), BF16)],
        compiler_params=_params(2),
        name="merge_outproj",
    )(oa, *og, *lg, ga, gb, x, g1, sc2, sh2, n2, wpa, wpb, wo)


def _ffn_kernel(h_ref, x_ref, g2_ref, nf_ref, wu_ref, wd_ref, y_ref, *, final_norm, n_split):
    h = h_ref[...]
    d_ff = wd_ref.shape[0]
    step = d_ff // n_split
    acc = None
    for j in range(n_split):
        u1 = _nn(h, wu_ref[:, j * step:(j + 1) * step])
        u2 = _nn(h, wu_ref[:, d_ff + j * step:d_ff + (j + 1) * step])
        a = (u1 * jax.nn.sigmoid(u1) * u2).astype(BF16)
        part = _nn(a, wd_ref[j * step:(j + 1) * step, :])
        acc = part if acc is None else acc + part
    x2 = x_ref[...] + g2_ref[...] * acc
    y_ref[...] = _rms(x2, nf_ref[...]) if final_norm else x2


def _ffn(h2, x1, g2, nf, wu, wd, tm, final_norm):
    nb, t, d = x1.shape
    tok = lambda n: pl.BlockSpec((None, tm, n), lambda b, i: (b, i, 0))
    return pl.pallas_call(
        functools.partial(_ffn_kernel, final_norm=final_norm, n_split=11),
        grid=(nb, t // tm),
        in_specs=[tok(d), tok(d), _mod_spec(g2, tm), _resident((1, d)),
                  _resident(wu.shape), _resident(wd.shape)],
        out_specs=tok(d),
        out_shape=jax.ShapeDtypeStruct((nb, t, d), F32),
        compiler_params=_params(2),
        name="ffn",
    )(h2, x1, g2, nf, wu, wd)


def _rope_tables(pos):
    half = DSW_HEAD_DIM // 2
    inv = ROPE_THETA ** (-jnp.arange(half, dtype=F32) / half)
    ang = pos[:, None] * inv[None, :]
    cos, sin = jnp.cos(ang), jnp.sin(ang)
    reps = LANES // half
    cosf = jnp.tile(cos, (1, reps))
    sign = jnp.tile(jnp.concatenate([-jnp.ones((half,), F32), jnp.ones((half,), F32)]), LANES // DSW_HEAD_DIM)
    return cosf, jnp.tile(sin, (1, reps)) * sign[None, :]


def _layer_weights(w_in, b_in, w_alpha2, b_alpha2, w_proj_a, w_proj_b, w_out, w_up, w_down):
    bf = lambda a: a.astype(BF16)
    row = lambda a: a.reshape(1, -1)
    o_glr, o_dq, o_ga = 3072, 3088, 5392
    pad_r = LANES - GLA_RANK
    return dict(
        wa=bf(w_in[:, :o_glr]), ba=row(b_in[:o_glr]),
        wg=bf(jnp.pad(w_in[:, o_glr:o_dq], ((0, 0), (0, pad_r)))), bg=row(jnp.pad(b_in[o_glr:o_dq], (0, pad_r))),
        w2=bf(jnp.pad(w_alpha2, ((0, pad_r), (0, 0)))), b2=row(b_alpha2),
        wb=bf(w_in[:, o_dq:o_ga]), bb=row(b_in[o_dq:o_ga]),
        wc=bf(w_in[:, o_ga:]), bc=row(b_in[o_ga:]),
        wpa=bf(w_proj_a), wpb=bf(w_proj_b), wo=bf(w_out), wu=bf(w_up), wd=bf(w_down))


def _kv_unstack(kvt, keep):
    nb, _, width = kvt.shape
    kv = kvt[:, :, width - keep:].reshape(nb, 2, DSW_HEADS, DSW_HEAD_DIM, keep)
    return jnp.transpose(kv, (0, 4, 1, 2, 3))


def kernel(x_prompt, x_sample, state_gla, cache_kv_w128, cache_kv_w512, cache_kv_w2048, c_prompt, c_sample,
           norm1_g, norm2_g, w_ada, b_ada, w_in, b_in, w_alpha2, b_alpha2, gla_norm_g, w_proj_a, w_proj_b,
           w_out, w_up, w_down, normf_g):
    depth = w_ada.shape[0]
    nb, t, d = x_prompt.shape
    n_seq, seq, _ = x_sample.shape
    assert 8 % seq == 0 and seq >= 3, "sample kernels pack whole sequences into 8-row groups"
    past = PAST_LEN
    caches = (cache_kv_w128, cache_kv_w512, cache_kv_w2048)

    cos_p, sin_p = _rope_tables(jnp.arange(t, dtype=F32))
    cos_s, sin_s = _rope_tables(jnp.tile(past + jnp.arange(seq, dtype=F32), n_seq))

    n_c = nb + n_seq
    pad_c = (-n_c) % 8
    c_all = jnp.pad(jnp.concatenate([c_prompt, c_sample], axis=0), ((0, pad_c), (0, 0)))

    xp = x_prompt
    xs = x_sample.reshape(1, n_seq * seq, d)
    row = lambda a: a.reshape(1, -1)
    sp_l, kvp_l, ss_l, kvs_l = [], [], [], []
    for l in range(depth):
        w = _layer_weights(w_in[l], b_in[l], w_alpha2[l], b_alpha2[l], w_proj_a[l], w_proj_b[l],
                           w_out[l], w_up[l], w_down[l])
        mod = _ada(c_all, w_ada[l], b_ada[l])
        mod_p = [mod[:nb, i * d:(i + 1) * d].reshape(nb, 1, d) for i in range(6)]
        mod_s = [jnp.repeat(mod[nb:nb + n_seq, i * d:(i + 1) * d], seq, axis=0).reshape(1, n_seq * seq, d)
                 for i in range(6)]
        last = l == depth - 1

        sh1_s, sc1_s, g1_s, sh2_s, sc2_s, g2_s = mod_s
        gq_s, gk_s, gv_s, gr_s, la_s, ga_s, gb_s, k32, v32, q32 = _inproj(
            xs, sc1_s, sh1_s, row(norm1_g[l]), cos_s, sin_s, w, tm=256, fold=False)
        sh1, sc1, g1, sh2, sc2, g2 = mod_p
        gq, gk, gv, gr, la, ga, gb, *dsw_p = _inproj(
            xp, sc1, sh1, row(norm1_g[l]), cos_p, sin_p, w, tm=512, fold=True)
        qkv, kvt = dsw_p[:3 * len(DSW_GROUPS)], dsw_p[3 * len(DSW_GROUPS):]

        g_big = len(DSW_GROUPS) - 1
        dil_big = DSW_GROUPS[g_big][1]
        gla_c = _gla_prompt_call(gq, gk, gv, la, gr, row(gla_norm_g[l]), tt=256)
        dsw_c = _dsw_sample_call(q32, k32, v32, caches[g_big][l], g_big, seq)
        sample_dsw = [None] * len(DSW_GROUPS)
        if gla_c["steps"] == dsw_c["steps"]:
            oa, st, o_b, lse_b, new_b = _gla_prompt_dsw_sample(gla_c, dsw_c, seq, dil_big)
            sample_dsw[g_big] = _dsw_sample_finish(o_b, lse_b, new_b, caches[g_big][l].shape)
        else:
            oa, st = _gla_prompt(gq, gk, gv, la, gr, row(gla_norm_g[l]), tt=256)
        for g, (_, dil) in enumerate(DSW_GROUPS):
            if sample_dsw[g] is None:
                sample_dsw[g] = _dsw_sample(q32, k32, v32, caches[g][l], g, dil, seq)
        og_s, lg_s, new_kv = zip(*sample_dsw)
        oa_s, s_new = _gla_sample(gq_s, gk_s, gv_s, la_s, gr_s, row(gla_norm_g[l]), state_gla[l], seq, bb=8)
        og, lg = zip(*[_dsw_prompt(*qkv[3 * g:3 * g + 3], dil) for g, (_, dil) in enumerate(DSW_GROUPS)])

        x1, h2 = _merge(oa, og, lg, ga, gb, xp, g1, sc2, sh2, row(norm2_g[l]), w["wpa"], w["wpb"], w["wo"], tm=512)
        xp = _ffn(h2, x1, g2, row(normf_g), w["wu"], w["wd"], tm=512, final_norm=last)
        x1, h2 = _merge(oa_s, og_s, lg_s, ga_s, gb_s, xs, g1_s, sc2_s, sh2_s, row(norm2_g[l]),
                        w["wpa"], w["wpb"], w["wo"], tm=256)
        xs = _ffn(h2, x1, g2_s, row(normf_g), w["wu"], w["wd"], tm=256, final_norm=last)

        sp_l.append(jnp.swapaxes(st, 2, 3))
        kvp_l.append(tuple(_kv_unstack(kvt[g], min(win, t)) for g, (win, _) in enumerate(DSW_GROUPS)))
        ss_l.append(s_new)
        kvs_l.append(new_kv)

    y_prompt = xp
    y_sample = xs.reshape(n_seq, seq, d)
    stack = lambda items: jnp.stack(list(items))
    return (y_prompt, y_sample, stack(sp_l),
            stack(kv[0] for kv in kvp_l), stack(kv[1] for kv in kvp_l), stack(kv[2] for kv in kvp_l),
            stack(ss_l),
            stack(kv[0] for kv in kvs_l), stack(kv[1] for kv in kvs_l), stack(kv[2] for kv in kvs_l))
```

```python
import functools

import jax
import jax.numpy as jnp
import numpy as np
from jax import lax
from jax.experimental import pallas as pl
from jax.experimental.pallas import tpu as pltpu

F32 = jnp.float32
BF16 = jnp.bfloat16

EPS = 1e-6
GLA_HEADS = 4
GLA_DK = 128
GLA_DV = 256
GLA_RANK = 16
GLA_TAU = 16.0
GLA_CHUNK = 64
GLA_SCALE = GLA_DK ** -0.5
DSW_GROUPS = ((128, 1), (512, 4), (2048, 16))
DSW_HEADS = 4
DSW_HEAD_DIM = 64
DSW_SCALE = DSW_HEAD_DIM ** -0.5
DSW_GW = DSW_HEADS * DSW_HEAD_DIM
BAND = 128
ROPE_THETA = 10000.0
PAST_LEN = 8192
LANES = 128
VMEM_LIMIT = 56 * 1024 * 1024


def _nn(a, b):
    return jnp.dot(a, b, preferred_element_type=F32)


def _nt(a, b):
    return lax.dot_general(a, b, (((1,), (1,)), ((), ())), preferred_element_type=F32)


def _tn(a, b):
    return lax.dot_general(a, b, (((0,), (0,)), ((), ())), preferred_element_type=F32)


def _split3(x):
    hi = x.astype(BF16)
    r1 = x - hi.astype(F32)
    mid = r1.astype(BF16)
    lo = (r1 - mid.astype(F32)).astype(BF16)
    return hi, mid, lo


def _iota(shape, dim):
    return lax.broadcasted_iota(jnp.int32, shape, dim)


def _rms(x, g):
    return x * lax.rsqrt(jnp.mean(x * x, axis=-1, keepdims=True) + EPS) * g


def _params(n_parallel, n_arbitrary=0):
    sem = ("parallel",) * n_parallel + ("arbitrary",) * n_arbitrary
    return pltpu.CompilerParams(dimension_semantics=sem, vmem_limit_bytes=VMEM_LIMIT)


def _resident(shape):
    nd = len(shape)
    return pl.BlockSpec(shape, lambda *_: (0,) * nd, pipeline_mode=pl.Buffered(1))


def _ada_kernel(c_ref, w_ref, b_ref, o_ref):
    c = c_ref[...]
    a = (c * jax.nn.sigmoid(c)).astype(BF16)
    o_ref[...] = _nn(a, w_ref[...].astype(BF16)) + b_ref[...]


def _ada(c_all, w_ada, b_ada):
    n, d = c_all.shape
    ncol = w_ada.shape[1]
    tn = 1536
    return pl.pallas_call(
        _ada_kernel,
        grid=(ncol // tn,),
        in_specs=[pl.BlockSpec((n, d), lambda j: (0, 0)),
                  pl.BlockSpec((d, tn), lambda j: (0, j)),
                  pl.BlockSpec((1, tn), lambda j: (0, j))],
        out_specs=pl.BlockSpec((n, tn), lambda j: (0, j)),
        out_shape=jax.ShapeDtypeStruct((n, ncol), F32),
        compiler_params=_params(1),
        name="ada_mod",
    )(c_all, w_ada, b_ada.reshape(1, ncol))


def _rope(x, cosf, sins, first_half):
    rot = jnp.where(first_half, pltpu.roll(x, LANES - 32, 1), pltpu.roll(x, 32, 1))
    return x * cosf + rot * sins


def _inproj_kernel(x_ref, sc_ref, sh_ref, g_ref, cos_ref, sin_ref,
                   wa_ref, ba_ref, wg_ref, bg_ref, w2_ref, b2_ref, wb_ref, bb_ref, wc_ref, bc_ref,
                   gq_ref, gk_ref, gv_ref, gr_ref, la_ref, ga_ref, gb_ref, *rest, fold):
    x = x_ref[...]
    h = (_rms(x, g_ref[...]) * (1.0 + sc_ref[...]) + sh_ref[...]).astype(BF16)

    gq_ref[...] = (_nn(h, wa_ref[:, 0:512]) + ba_ref[:, 0:512]).astype(gq_ref.dtype)
    gk_ref[...] = (_nn(h, wa_ref[:, 512:1024]) + ba_ref[:, 512:1024]).astype(gk_ref.dtype)
    gv_ref[...] = (_nn(h, wa_ref[:, 1024:2048]) + ba_ref[:, 1024:2048]).astype(gv_ref.dtype)
    gr_ref[...] = (_nn(h, wa_ref[:, 2048:3072]) + ba_ref[:, 2048:3072]).astype(gr_ref.dtype)

    glr = (_nn(h, wg_ref[...]) + bg_ref[...]).astype(BF16)
    z = _nn(glr, w2_ref[...]) + b2_ref[...]
    la_ref[...] = jax.nn.log_sigmoid(z) * (1.0 / GLA_TAU)

    cosf = cos_ref[...]
    sins = sin_ref[...]
    first_half = (_iota(cosf.shape, 1) % DSW_HEAD_DIM) < (DSW_HEAD_DIM // 2)
    tm = x.shape[0]
    per_group = DSW_GW // LANES
    width = 3 * DSW_GW

    def proj(off, g):
        cols = slice(off + g * DSW_GW, off + (g + 1) * DSW_GW)
        full = _nn(h, wb_ref[:, cols]) + bb_ref[:, cols]
        return [full[:, s * LANES:(s + 1) * LANES] for s in range(per_group)]

    for g, (_, dil) in enumerate(DSW_GROUPS):
        qs = [_rope(a, cosf, sins, first_half) * DSW_SCALE for a in proj(0, g)]
        ks = [_rope(a, cosf, sins, first_half) for a in proj(width, g)]
        vs = proj(2 * width, g)
        if fold:
            kvt_ref = rest[3 * len(DSW_GROUPS) + g]
            kvt_ref[0:DSW_GW, :] = jnp.concatenate(ks, axis=1).T
            kvt_ref[DSW_GW:2 * DSW_GW, :] = jnp.concatenate(vs, axis=1).T

        for slab in range(per_group):
            cols = slice(g * DSW_GW + slab * LANES, g * DSW_GW + (slab + 1) * LANES)
            lanes = slice(slab * LANES, (slab + 1) * LANES)
            if not fold:
                rest[0][:, cols] = ks[slab]
                rest[1][:, cols] = vs[slab]
                rest[2][:, cols] = qs[slab]
                continue
            scratch = rest[-1]
            for which, val in enumerate((qs[slab], ks[slab], vs[slab])):
                out_ref = rest[3 * g + which]
                if dil == 1:
                    out_ref[:, lanes] = val.astype(BF16)
                else:
                    scratch[which] = val
                    for r in range(dil):
                        out_ref[r, :, lanes] = scratch[which, pl.ds(r, tm // dil, stride=dil), :].astype(BF16)

    ga_ref[...] = (_nn(h, wc_ref[:, 0:1024]) + bc_ref[:, 0:1024]).astype(ga_ref.dtype)
    gb_ref[...] = (_nn(h, wc_ref[:, 1024:2048]) + bc_ref[:, 1024:2048]).astype(gb_ref.dtype)


def _mod_spec(arr, tm):
    if arr.shape[1] == 1:
        return pl.BlockSpec((None, 1, arr.shape[2]), lambda b, i: (b, 0, 0))
    return pl.BlockSpec((None, tm, arr.shape[2]), lambda b, i: (b, i, 0))


def _inproj(x, sc, sh, g, cos_t, sin_t, w, tm, fold):
    nb, t, d = x.shape
    tok = lambda n: pl.BlockSpec((None, tm, n), lambda b, i: (b, i, 0))
    out_cols = (512, 512, 1024, 1024, 512, 1024, 1024)
    out_dt = (BF16, BF16, BF16, BF16, F32, BF16, BF16)
    out_specs = [tok(n) for n in out_cols]
    out_shape = [jax.ShapeDtypeStruct((nb, t, n), dt) for n, dt in zip(out_cols, out_dt)]
    scratch = []
    if fold:
        for _, dil in DSW_GROUPS:
            for _ in range(3):
                if dil == 1:
                    out_specs.append(tok(DSW_GW))
                    out_shape.append(jax.ShapeDtypeStruct((nb, t, DSW_GW), BF16))
                else:
                    out_specs.append(pl.BlockSpec((None, dil, tm // dil, DSW_GW), lambda b, i: (b, 0, i, 0)))
                    out_shape.append(jax.ShapeDtypeStruct((nb, dil, t // dil, DSW_GW), BF16))
        for win, _ in DSW_GROUPS:
            width = min(max(win, tm), t)
            first = (t - width) // tm
            out_specs.append(pl.BlockSpec((None, 2 * DSW_GW, tm),
                                          lambda b, i, first=first: (b, 0, jnp.maximum(i - first, 0))))
            out_shape.append(jax.ShapeDtypeStruct((nb, 2 * DSW_GW, width), F32))
        scratch = [pltpu.VMEM((3, tm, LANES), F32)]
    else:
        for _ in range(3):
            out_specs.append(tok(3 * DSW_GW))
            out_shape.append(jax.ShapeDtypeStruct((nb, t, 3 * DSW_GW), F32))
    weights = (w["wa"], w["ba"], w["wg"], w["bg"], w["w2"], w["b2"], w["wb"], w["bb"], w["wc"], w["bc"])
    return pl.pallas_call(
        functools.partial(_inproj_kernel, fold=fold),
        grid=(nb, t // tm),
        in_specs=[tok(d), _mod_spec(sc, tm), _mod_spec(sh, tm), _resident((1, d)),
                  pl.BlockSpec((tm, LANES), lambda b, i: (i, 0)),
                  pl.BlockSpec((tm, LANES), lambda b, i: (i, 0))]
                 + [_resident(a.shape) for a in weights],
        out_specs=out_specs,
        out_shape=out_shape,
        scratch_shapes=scratch,
        compiler_params=_params(1, 1),
        name="inproj",
    )(x, sc, sh, g, cos_t, sin_t, *weights)


def _gla_local(gq_ref, gk_ref, la_ref, chunk):
    la = la_ref[...]
    tt = la.shape[0]
    r = _iota((tt, tt), 0)
    c = _iota((tt, tt), 1)
    same = (r // chunk) == (c // chunk)
    tri = jnp.where(same & (c <= r), 1.0, 0.0).astype(BF16)
    hi, mid, lo = _split3(la)
    b = _nn(tri, hi) + _nn(tri, mid) + _nn(tri, lo)
    if chunk % 8 == 0:
        bl = jnp.concatenate([jnp.broadcast_to(b[e - 1:e, :], (chunk, b.shape[1]))
                              for e in range(chunk, tt + 1, chunk)], axis=0)
    else:
        ones = jnp.where(same, 1.0, 0.0).astype(BF16)
        bl = _nn(ones, hi) + _nn(ones, mid) + _nn(ones, lo)
    gq = gq_ref[...].astype(F32)
    gk = gk_ref[...].astype(F32)
    qg = (gq * GLA_SCALE * jnp.exp(b)).astype(BF16)
    kd = (gk * jnp.exp(-b)).astype(BF16)
    kl = (gk * jnp.exp(bl - b)).astype(BF16)
    causal = same & (c <= r)
    return qg, kd, kl, jnp.exp(bl), causal


def _gla_finish(o, gr, g):
    return (_rms(o, g) * (gr * jax.nn.sigmoid(gr))).astype(BF16)


def _gla_prompt_body(first_tile, gq_ref, gk_ref, gv_ref, la_ref, gr_ref, g_ref, o_ref, st_ref):
    @pl.when(first_tile)
    def _():
        st_ref[...] = jnp.zeros_like(st_ref)

    qg, kd, kl, dec, causal = _gla_local(gq_ref, gk_ref, la_ref, GLA_CHUNK)
    tt = qg.shape[0]
    chunk_of_row = _iota((tt, GLA_DK), 0) // GLA_CHUNK
    for h in range(GLA_HEADS):
        kc = slice(h * GLA_DK, (h + 1) * GLA_DK)
        vc = slice(h * GLA_DV, (h + 1) * GLA_DV)
        v = gv_ref[:, vc].astype(BF16)
        att = jnp.where(causal, _nt(qg[:, kc], kd[:, kc]), 0.0).astype(BF16)
        intra = _nn(att, v)
        st = st_ref[h]
        n_chunks = tt // GLA_CHUNK
        kl_h = kl[:, kc]
        kl_bd = jnp.concatenate([jnp.where(chunk_of_row == ci, kl_h, jnp.zeros_like(kl_h))
                                 for ci in range(n_chunks)], axis=1)
        upd = _tn(v, kl_bd)
        inter = []
        for ci in range(n_chunks):
            rows = slice(ci * GLA_CHUNK, (ci + 1) * GLA_CHUNK)
            inter.append(_nt(qg[rows, kc], st.astype(BF16)))
            st = dec[ci * GLA_CHUNK:ci * GLA_CHUNK + 1, kc] * st + upd[:, ci * GLA_DK:(ci + 1) * GLA_DK]
        st_ref[h] = st
        o = intra + jnp.concatenate(inter, axis=0)
        o_ref[:, vc] = _gla_finish(o, gr_ref[:, vc].astype(F32), g_ref[...])


def _gla_prompt_call(gq, gk, gv, la, gr, g, tt):
    nb, t, _ = gq.shape
    tiles = t // tt
    tok = lambda n: pl.BlockSpec((None, tt, n), lambda i: (i // tiles, i % tiles, 0))
    def body(*refs):
        _gla_prompt_body(pl.program_id(0) % tiles == 0, *refs)

    return dict(
        name="gla_prompt", steps=nb * tiles, body=body, args=(gq, gk, gv, la, gr, g),
        in_specs=[tok(512), tok(512), tok(1024), tok(512), tok(1024), _resident((1, GLA_DV))],
        out_specs=[tok(1024),
                   pl.BlockSpec((None, GLA_HEADS, GLA_DV, GLA_DK), lambda i: (i // tiles, 0, 0, 0))],
        out_shape=[jax.ShapeDtypeStruct((nb, t, 1024), BF16),
                   jax.ShapeDtypeStruct((nb, GLA_HEADS, GLA_DV, GLA_DK), F32)])


def _run_jobs(*jobs):
    steps = jobs[0]["steps"]
    assert all(j["steps"] == steps for j in jobs)
    n_in = [len(j["in_specs"]) for j in jobs]
    n_out = [len(j["out_specs"]) for j in jobs]

    def kern(*refs):
        i_pos, o_pos = 0, sum(n_in)
        for j, ni, no in zip(jobs, n_in, n_out):
            j["body"](*refs[i_pos:i_pos + ni], *refs[o_pos:o_pos + no])
            i_pos, o_pos = i_pos + ni, o_pos + no

    outs = pl.pallas_call(
        kern,
        grid=(steps,),
        in_specs=[s for j in jobs for s in j["in_specs"]],
        out_specs=[s for j in jobs for s in j["out_specs"]],
        out_shape=[s for j in jobs for s in j["out_shape"]],
        compiler_params=_params(0, 1),
        name="__".join(j["name"] for j in jobs),
    )(*[a for j in jobs for a in j["args"]])
    split, pos = [], 0
    for no in n_out:
        split.append(list(outs[pos:pos + no]))
        pos += no
    return split


def _gla_sample_kernel(gq_ref, gk_ref, gv_ref, la_ref, gr_ref, g_ref, s_ref, o_ref, so_ref, *, seq):
    qg, kd, kl, dec, causal = _gla_local(gq_ref, gk_ref, la_ref, seq)
    rows_total = qg.shape[0]
    per8 = 8 // seq
    row8 = _iota((8, 1), 0)
    for h in range(GLA_HEADS):
        kc = slice(h * GLA_DK, (h + 1) * GLA_DK)
        vc = slice(h * GLA_DV, (h + 1) * GLA_DV)
        v = gv_ref[:, vc].astype(BF16)
        att = jnp.where(causal, _nt(qg[:, kc], kd[:, kc]), 0.0).astype(BF16)
        intra = _nn(att, v)
        inter = []
        for p in range(rows_total // 8):
            rows = slice(p * 8, (p + 1) * 8)
            d_hi, d_mid, d_lo = _split3(dec[rows, kc])
            inter_p = jnp.zeros((8, GLA_DV), F32)
            for j in range(per8):
                b = p * per8 + j
                r0 = j * seq
                s0 = s_ref[b, h]
                mine = (row8 >= r0) & (row8 < r0 + seq)
                inter_p = jnp.where(mine, _nn(qg[rows, kc], s0.astype(BF16)), inter_p)
                dl = jnp.where(row8 == r0, d_hi, jnp.where(row8 == r0 + 1, d_mid,
                               jnp.where(row8 == r0 + 2, d_lo, jnp.zeros_like(d_lo))))
                e = jnp.where((row8 >= r0) & (row8 < r0 + 3), 1.0, 0.0).astype(BF16)
                dec_b = _tn(dl, jnp.broadcast_to(e, (8, GLA_DV)))
                upd = _tn(jnp.where(mine, kl[rows, kc], jnp.zeros_like(kl[rows, kc])), v[rows])
                so_ref[b, h] = dec_b * s0 + upd
            inter.append(inter_p)
        o = intra + jnp.concatenate(inter, axis=0)
        o_ref[:, vc] = _gla_finish(o, gr_ref[:, vc].astype(F32), g_ref[...])


def _gla_sample(gq, gk, gv, la, gr, g, s0, seq, bb):
    n_seq = s0.shape[0]
    rows = bb * seq
    tok = lambda n: pl.BlockSpec((None, rows, n), lambda i: (0, i, 0))
    st = pl.BlockSpec((bb, GLA_HEADS, GLA_DK, GLA_DV), lambda i: (i, 0, 0, 0))
    return pl.pallas_call(
        functools.partial(_gla_sample_kernel, seq=seq),
        grid=(n_seq // bb,),
        in_specs=[tok(512), tok(512), tok(1024), tok(512), tok(1024), _resident((1, GLA_DV)), st],
        out_specs=[tok(1024), st],
        out_shape=[jax.ShapeDtypeStruct((1, n_seq * seq, 1024), BF16),
                   jax.ShapeDtypeStruct(s0.shape, F32)],
        compiler_params=_params(1),
        name="gla_sample",
    )(gq, gk, gv, la, gr, g, s0)


def _dsw_prompt_kernel(q_ref, kp_ref, kc_ref, vp_ref, vc_ref, o_ref, lse_ref, *, dil):
    qb = q_ref.shape[0]
    res = pl.program_id(2)
    first_key = jnp.where(pl.program_id(1) == 0, BAND, 0)
    qi = _iota((BAND, 2 * BAND), 0) + BAND
    ki = _iota((BAND, 2 * BAND), 1)
    band = (qi - ki >= 0) & (qi - ki <= BAND)
    lane = _iota((BAND, LANES), 1)
    for s in range(qb // BAND):
        rows = slice(s * BAND, (s + 1) * BAND)
        if s == 0:
            valid = band & (ki >= first_key)
        else:
            valid = band
        if dil == 1:
            tok_rows = rows
        else:
            tok_rows = pl.ds(s * BAND * dil + res, BAND, stride=dil)
        for hp in range(DSW_GW // LANES):
            cols = slice(hp * LANES, (hp + 1) * LANES)
            qp = q_ref[rows, cols]
            if s == 0:
                kcat = jnp.concatenate([kp_ref[:, cols], kc_ref[0:BAND, cols]], axis=0)
                vcat = jnp.concatenate([vp_ref[:, cols], vc_ref[0:BAND, cols]], axis=0)
            else:
                kcat = kc_ref[(s - 1) * BAND:(s + 1) * BAND, cols]
                vcat = vc_ref[(s - 1) * BAND:(s + 1) * BAND, cols]
            outs, lses = [], []
            for hh in range(LANES // DSW_HEAD_DIM):
                in_head = (lane // DSW_HEAD_DIM) == hh
                sc = _nt(jnp.where(in_head, qp, jnp.zeros_like(qp)), kcat)
                sc = jnp.where(valid, sc, -jnp.inf)
                m = jnp.max(sc, axis=-1, keepdims=True)
                e = jnp.exp(sc - m)
                den = jnp.sum(e, axis=-1, keepdims=True)
                outs.append(_nn((e / den).astype(BF16), vcat))
                lses.append(m + jnp.log(den))
            first = lane < DSW_HEAD_DIM
            o_ref[hp, tok_rows, :] = jnp.where(first, outs[0], outs[1])
            lse_ref[hp, tok_rows, :] = jnp.where(first, lses[0], jnp.broadcast_to(lses[1], (BAND, LANES)))


def _dsw_prompt(q, k, v, dil):
    nb = q.shape[0]
    seq_len = q.shape[-2]
    t = seq_len * dil
    tq = min(512, seq_len)
    sub = tq // BAND
    if dil == 1:
        cur = pl.BlockSpec((None, tq, DSW_GW), lambda b, j, r: (b, j, 0))
        prev = pl.BlockSpec((None, BAND, DSW_GW), lambda b, j, r: (b, jnp.maximum(j * sub - 1, 0), 0))
    else:
        cur = pl.BlockSpec((None, None, tq, DSW_GW), lambda b, j, r: (b, r, j, 0))
        prev = pl.BlockSpec((None, None, BAND, DSW_GW), lambda b, j, r: (b, r, jnp.maximum(j * sub - 1, 0), 0))
    n_slab = DSW_GW // LANES
    out = pl.BlockSpec((None, n_slab, tq * dil, LANES), lambda b, j, r: (b, 0, j, 0))
    return pl.pallas_call(
        functools.partial(_dsw_prompt_kernel, dil=dil),
        grid=(nb, seq_len // tq, dil),
        in_specs=[cur, prev, cur, prev, cur],
        out_specs=[out, out],
        out_shape=[jax.ShapeDtypeStruct((nb, n_slab, t, LANES), F32)] * 2,
        compiler_params=_params(2, 1),
        name=f"dsw_prompt_d{dil}",
    )(q, k, k, v, v)


def _dsw_sample_kernel(q_ref, kn_ref, vn_ref, c_ref, o_ref, lse_ref, co_ref, *, seq, dil):
    per8 = 8 // seq
    win = c_ref.shape[2]
    n_rows = DSW_HEADS * 8
    lane = _iota((8, LANES), 1)
    head_of_lane = _iota((8, DSW_GW), 1) // DSW_HEAD_DIM
    row8 = _iota((8, 1), 0)
    r = _iota((n_rows, 1), 0)
    r_step = r % seq
    r_seq = (r % 8) // seq
    key = _iota((n_rows, win), 1)
    cache_ok = ((key % dil) == (r_step % dil)) & (key >= r_step)
    c128 = _iota((n_rows, LANES), 1)
    new_ok = ((c128 < 8) & ((c128 // seq) == r_seq) & ((c128 % seq) <= r_step)
              & (((r_step - c128 % seq) % dil) == 0))
    pad = jnp.zeros((LANES - 8, DSW_GW), BF16)

    def by_head(x):
        out = x[(DSW_HEADS - 1) * 8:DSW_HEADS * 8]
        for h in range(DSW_HEADS - 2, -1, -1):
            out = jnp.where(head_of_lane == h, x[h * 8:(h + 1) * 8], out)
        return out

    lane_sq = _iota((LANES, LANES), 1)
    p_row = _iota((8, LANES), 0)
    for grp in range(q_ref.shape[0] // 8):
        r8 = slice(grp * 8, (grp + 1) * 8)
        q8 = q_ref[r8, :]
        qrows = jnp.concatenate([jnp.where(head_of_lane == h, q8, 0.0) for h in range(DSW_HEADS)],
                                axis=0).astype(BF16)
        kn8 = kn_ref[r8, :]
        vn8 = vn_ref[r8, :]
        kn_t = jnp.concatenate([kn8.astype(BF16), pad], axis=0)
        vn_t = jnp.concatenate([vn8.astype(BF16), pad], axis=0)
        scn = jnp.where(new_ok, _nt(qrows, kn_t), -jnp.inf)
        m_new = jnp.max(scn, axis=-1, keepdims=True)
        o_p = jnp.zeros((8, DSW_GW), F32)
        l_p = jnp.zeros((8, DSW_GW), F32)
        for j in range(per8):
            b = grp * per8 + j
            kt = c_ref[b, 0:DSW_GW, :].astype(BF16)
            vt = c_ref[b, DSW_GW:2 * DSW_GW, :].astype(BF16)
            sc = jnp.where(cache_ok, _nn(qrows, kt), -jnp.inf)
            m = jnp.maximum(jnp.max(sc, axis=-1, keepdims=True), m_new)
            e = jnp.exp(sc - m)
            en = jnp.exp(scn - m)
            den = jnp.sum(e, axis=-1, keepdims=True) + jnp.sum(en, axis=-1, keepdims=True)
            o = _nt((e / den).astype(BF16), vt) + _nn((en / den).astype(BF16), vn_t)
            lse = jnp.broadcast_to(m + jnp.log(den), (n_rows, DSW_GW))
            mine = (row8 // seq) == j
            o_p = jnp.where(mine, by_head(o), o_p)
            l_p = jnp.where(mine, by_head(lse), l_p)
        o_ref[r8, :] = o_p
        lse_ref[r8, :] = l_p

        hi, mid, lo = _split3(jnp.concatenate([kn8, vn8], axis=1))
        for j in range(per8):
            b = grp * per8 + j
            place = jnp.where(((p_row // seq) == j) & (lane == LANES - seq + p_row % seq), 1.0, 0.0).astype(BF16)
            new_cols = _tn(hi, place) + _tn(mid, place) + _tn(lo, place)
            for blk in range(2 * DSW_GW // LANES):
                rows = slice(blk * LANES, (blk + 1) * LANES)
                rolled = pltpu.roll(c_ref[b, rows, :], win - seq, 1)
                if win > LANES:
                    co_ref[b, rows, 0:win - LANES] = rolled[:, 0:win - LANES]
                co_ref[b, rows, win - LANES:win] = jnp.where(lane_sq < LANES - seq, rolled[:, win - LANES:win],
                                                             new_cols[rows])


def _dsw_sample_call(q32, k32, v32, cache, g, dil, seq, steps=None):
    n_seq, win = cache.shape[0], cache.shape[1]
    per8 = 8 // seq
    if steps is None:
        groups = max(1, min(8, 4 * 512 // win))
    else:
        groups = n_seq // (per8 * steps)
    n_blk = per8 * groups
    view = jnp.transpose(cache, (0, 2, 3, 4, 1)).reshape(n_seq, 2 * DSW_GW, win)
    tok = pl.BlockSpec((None, 8 * groups, DSW_GW), lambda i: (0, i, g))
    tok_out = pl.BlockSpec((None, 8 * groups, DSW_GW), lambda i: (0, i, 0))
    cspec = pl.BlockSpec((n_blk, 2 * DSW_GW, win), lambda i: (i, 0, 0))
    return dict(
        name=f"dsw_sample_d{dil}", steps=n_seq // n_blk,
        body=functools.partial(_dsw_sample_kernel, seq=seq, dil=dil), args=(q32, k32, v32, view),
        in_specs=[tok, tok, tok, cspec],
        out_specs=[tok_out, tok_out, cspec],
        out_shape=[jax.ShapeDtypeStruct((1, n_seq * seq, DSW_GW), F32)] * 2
                  + [jax.ShapeDtypeStruct(view.shape, F32)])


def _dsw_sample_finish(o, lse, new, cache_shape):
    n_seq, win = cache_shape[0], cache_shape[1]
    new = jnp.transpose(new.reshape(n_seq, 2, DSW_HEADS, DSW_HEAD_DIM, win), (0, 4, 1, 2, 3))
    n_slab = DSW_GW // LANES
    slabs = lambda a: jnp.transpose(a.reshape(1, -1, n_slab, LANES), (0, 2, 1, 3))
    return slabs(o), slabs(lse), new


def _merge_kernel(oa_ref, o0_ref, o1_ref, o2_ref, l0_ref, l1_ref, l2_ref, ga_ref, gb_ref, x_ref,
                  g1_ref, sc_ref, sh_ref, n2_ref, wpa_ref, wpb_ref, wo_ref, x1_ref, h2_ref):
    ob = []
    for slab in range(DSW_GW // LANES):
        l0, l1, l2 = l0_ref[slab], l1_ref[slab], l2_ref[slab]
        m = jnp.maximum(jnp.maximum(l0, l1), l2)
        w0, w1, w2 = jnp.exp(l0 - m), jnp.exp(l1 - m), jnp.exp(l2 - m)
        den = w0 + w1 + w2
        ob.append((w0 / den) * o0_ref[slab] + (w1 / den) * o1_ref[slab] + (w2 / den) * o2_ref[slab])
    ob = jnp.concatenate(ob, axis=1).astype(BF16)
    merged = (jax.nn.sigmoid(ga_ref[...].astype(F32)) * _nn(oa_ref[...], wpa_ref[...])
              + jax.nn.sigmoid(gb_ref[...].astype(F32)) * _nn(ob, wpb_ref[...]))
    x1 = x_ref[...] + g1_ref[...] * _nn(merged.astype(BF16), wo_ref[...])
    x1_ref[...] = x1
    h2_ref[...] = (_rms(x1, n2_ref[...]) * (1.0 + sc_ref[...]) + sh_ref[...]).astype(BF16)


def _flat_tok_spec(tm, tiles, n):
    return pl.BlockSpec((None, tm, n), lambda i: (i // tiles, i % tiles, 0))


def _flat_mod_spec(arr, tm, tiles):
    if arr.shape[1] == 1:
        return pl.BlockSpec((None, 1, arr.shape[2]), lambda i: (i // tiles, 0, 0))
    return pl.BlockSpec((None, tm, arr.shape[2]), lambda i: (i // tiles, i % tiles, 0))


def _merge_call(oa, og, lg, ga, gb, x, g1, sc2, sh2, n2, wpa, wpb, wo, tm):
    nb, t, d = x.shape
    tiles = t // tm
    tok = functools.partial(_flat_tok_spec, tm, tiles)
    mod = lambda a: _flat_mod_spec(a, tm, tiles)
    slab = pl.BlockSpec((None, DSW_GW // LANES, tm, LANES), lambda i: (i // tiles, 0, i % tiles, 0))
    return dict(
        name="merge_outproj", steps=nb * tiles, body=_merge_kernel,
        args=(oa, *og, *lg, ga, gb, x, g1, sc2, sh2, n2, wpa, wpb, wo),
        in_specs=[tok(1024)] + [slab] * 6 + [tok(d), tok(d), tok(d), mod(g1), mod(sc2), mod(sh2),
                  _resident((1, d)), _resident(wpa.shape), _resident(wpb.shape), _resident(wo.shape)],
        out_specs=[tok(d), tok(d)],
        out_shape=[jax.ShapeDtypeStruct((nb, t, d), F32), jax.ShapeDtypeStruct((nb, t, d), BF16)])


def _ffn_kernel(h_ref, x_ref, g2_ref, nf_ref, wu_ref, wd_ref, y_ref, *, final_norm, n_split):
    h = h_ref[...]
    d_ff = wd_ref.shape[0]
    step = d_ff // n_split
    acc = None
    for j in range(n_split):
        u1 = _nn(h, wu_ref[:, j * step:(j + 1) * step])
        u2 = _nn(h, wu_ref[:, d_ff + j * step:d_ff + (j + 1) * step])
        a = (u1 * jax.nn.sigmoid(u1) * u2).astype(BF16)
        part = _nn(a, wd_ref[j * step:(j + 1) * step, :])
        acc = part if acc is None else acc + part
    x2 = x_ref[...] + g2_ref[...] * acc
    y_ref[...] = _rms(x2, nf_ref[...]) if final_norm else x2


def _ffn_call(h2, x1, g2, nf, wu, wd, tm, final_norm):
    nb, t, d = x1.shape
    tiles = t // tm
    tok = functools.partial(_flat_tok_spec, tm, tiles)
    return dict(
        name="ffn", steps=nb * tiles,
        body=functools.partial(_ffn_kernel, final_norm=final_norm, n_split=wd.shape[0] // (2 * LANES)),
        args=(h2, x1, g2, nf, wu, wd),
        in_specs=[tok(d), tok(d), _flat_mod_spec(g2, tm, tiles), _resident((1, d)),
                  _resident(wu.shape), _resident(wd.shape)],
        out_specs=[tok(d)],
        out_shape=[jax.ShapeDtypeStruct((nb, t, d), F32)])


def _rope_tables(pos):
    half = DSW_HEAD_DIM // 2
    inv = ROPE_THETA ** (-np.arange(half, dtype=np.float64) / half)
    ang = np.asarray(pos, np.float64)[:, None] * inv[None, :]
    reps = LANES // half
    sign = np.tile(np.concatenate([-np.ones(half), np.ones(half)]), LANES // DSW_HEAD_DIM)
    cosf = np.tile(np.cos(ang), (1, reps))
    sins = np.tile(np.sin(ang), (1, reps)) * sign[None, :]
    return jnp.asarray(cosf, F32), jnp.asarray(sins, F32)


def _layer_weights(w_in, b_in, w_alpha2, b_alpha2, w_proj_a, w_proj_b, w_out, w_up, w_down):
    bf = lambda a: a.astype(BF16)
    row = lambda a: a.reshape(1, -1)
    o_glr, o_dq, o_ga = 3072, 3088, 5392
    pad_r = LANES - GLA_RANK
    return dict(
        wa=bf(w_in[:, :o_glr]), ba=row(b_in[:o_glr]),
        wg=bf(jnp.pad(w_in[:, o_glr:o_dq], ((0, 0), (0, pad_r)))), bg=row(jnp.pad(b_in[o_glr:o_dq], (0, pad_r))),
        w2=bf(jnp.pad(w_alpha2, ((0, pad_r), (0, 0)))), b2=row(b_alpha2),
        wb=bf(w_in[:, o_dq:o_ga]), bb=row(b_in[o_dq:o_ga]),
        wc=bf(w_in[:, o_ga:]), bc=row(b_in[o_ga:]),
        wpa=bf(w_proj_a), wpb=bf(w_proj_b), wo=bf(w_out), wu=bf(w_up), wd=bf(w_down))


def _kv_unstack(kvt, keep):
    nb, _, width = kvt.shape
    kv = kvt[:, :, width - keep:].reshape(nb, 2, DSW_HEADS, DSW_HEAD_DIM, keep)
    return jnp.transpose(kv, (0, 4, 1, 2, 3))


def kernel(x_prompt, x_sample, state_gla, cache_kv_w128, cache_kv_w512, cache_kv_w2048, c_prompt, c_sample,
           norm1_g, norm2_g, w_ada, b_ada, w_in, b_in, w_alpha2, b_alpha2, gla_norm_g, w_proj_a, w_proj_b,
           w_out, w_up, w_down, normf_g):
    depth = w_ada.shape[0]
    nb, t, d = x_prompt.shape
    n_seq, seq, _ = x_sample.shape
    assert 8 % seq == 0 and seq >= 3, "sample kernels pack whole sequences into 8-row groups"
    past = PAST_LEN
    caches = (cache_kv_w128, cache_kv_w512, cache_kv_w2048)

    cos_p, sin_p = _rope_tables(np.arange(t))
    cos_s, sin_s = _rope_tables(np.tile(past + np.arange(seq), n_seq))

    n_c = nb + n_seq
    pad_c = (-n_c) % 8
    c_all = jnp.pad(jnp.concatenate([c_prompt, c_sample], axis=0), ((0, pad_c), (0, 0)))

    xp = x_prompt
    xs = x_sample.reshape(1, n_seq * seq, d)
    row = lambda a: a.reshape(1, -1)
    sp_l, kvp_l, ss_l, kvs_l = [], [], [], []
    for l in range(depth):
        w = _layer_weights(w_in[l], b_in[l], w_alpha2[l], b_alpha2[l], w_proj_a[l], w_proj_b[l],
                           w_out[l], w_up[l], w_down[l])
        mod = _ada(c_all, w_ada[l], b_ada[l])
        mod_p = [mod[:nb, i * d:(i + 1) * d].reshape(nb, 1, d) for i in range(6)]
        mod_s = [jnp.repeat(mod[nb:nb + n_seq, i * d:(i + 1) * d], seq, axis=0).reshape(1, n_seq * seq, d)
                 for i in range(6)]
        last = l == depth - 1

        sh1_s, sc1_s, g1_s, sh2_s, sc2_s, g2_s = mod_s
        gq_s, gk_s, gv_s, gr_s, la_s, ga_s, gb_s, k32, v32, q32 = _inproj(
            xs, sc1_s, sh1_s, row(norm1_g[l]), cos_s, sin_s, w, tm=256, fold=False)
        sh1, sc1, g1, sh2, sc2, g2 = mod_p
        gq, gk, gv, gr, la, ga, gb, *dsw_p = _inproj(
            xp, sc1, sh1, row(norm1_g[l]), cos_p, sin_p, w, tm=512, fold=True)
        qkv, kvt = dsw_p[:3 * len(DSW_GROUPS)], dsw_p[3 * len(DSW_GROUPS):]

        def with_sample_group(host, g):
            steps = host["steps"]
            fits = n_seq % ((8 // seq) * steps) == 0 and n_seq // ((8 // seq) * steps) <= 8
            job = _dsw_sample_call(q32, k32, v32, caches[g][l], g, DSW_GROUPS[g][1], seq, steps if fits else None)
            if fits:
                host_out, job_out = _run_jobs(host, job)
            else:
                (host_out,), (job_out,) = _run_jobs(host), _run_jobs(job)
            return host_out, _dsw_sample_finish(*job_out, caches[g][l].shape)

        sample_dsw = [None] * len(DSW_GROUPS)
        (oa, st), sample_dsw[2] = with_sample_group(
            _gla_prompt_call(gq, gk, gv, la, gr, row(gla_norm_g[l]), tt=256), 2)
        oa_s, s_new = _gla_sample(gq_s, gk_s, gv_s, la_s, gr_s, row(gla_norm_g[l]), state_gla[l], seq, bb=8)
        og, lg = zip(*[_dsw_prompt(*qkv[3 * g:3 * g + 3], dil) for g, (_, dil) in enumerate(DSW_GROUPS)])

        (x1, h2), sample_dsw[0] = with_sample_group(
            _merge_call(oa, og, lg, ga, gb, xp, g1, sc2, sh2, row(norm2_g[l]), w["wpa"], w["wpb"], w["wo"], tm=512), 0)
        (xp,), sample_dsw[1] = with_sample_group(
            _ffn_call(h2, x1, g2, row(normf_g), w["wu"], w["wd"], tm=512, final_norm=last), 1)
        og_s, lg_s, new_kv = zip(*sample_dsw)

        ((x1, h2),) = _run_jobs(_merge_call(oa_s, og_s, lg_s, ga_s, gb_s, xs, g1_s, sc2_s, sh2_s, row(norm2_g[l]),
                                            w["wpa"], w["wpb"], w["wo"], tm=256))
        ((xs,),) = _run_jobs(_ffn_call(h2, x1, g2_s, row(normf_g), w["wu"], w["wd"], tm=256, final_norm=last))

        sp_l.append(jnp.swapaxes(st, 2, 3))
        kvp_l.append(tuple(_kv_unstack(kvt[g], min(win, t)) for g, (win, _) in enumerate(DSW_GROUPS)))
        ss_l.append(s_new)
        kvs_l.append(new_kv)

    y_prompt = xp
    y_sample = xs.reshape(n_seq, seq, d)
    stack = lambda items: jnp.stack(list(items))
    return (y_prompt, y_sample, stack(sp_l),
            stack(kv[0] for kv in kvp_l), stack(kv[1] for kv in kvp_l), stack(kv[2] for kv in kvp_l),
            stack(ss_l),
            stack(kv[0] for kv in kvs_l), stack(kv[1] for kv in kvs_l), stack(kv[2] for kv in kvs_l))
```

```python
import functools

import jax
import jax.numpy as jnp
import numpy as np
from jax import lax
from jax.experimental import pallas as pl
from jax.experimental.pallas import tpu as pltpu

F32 = jnp.float32
BF16 = jnp.bfloat16

EPS = 1e-6
GLA_HEADS = 4
GLA_DK = 128
GLA_DV = 256
GLA_RANK = 16
GLA_TAU = 16.0
GLA_CHUNK = 64
GLA_SCALE = GLA_DK ** -0.5
DSW_GROUPS = ((128, 1), (512, 4), (2048, 16))
DSW_HEADS = 4
DSW_HEAD_DIM = 64
DSW_SCALE = DSW_HEAD_DIM ** -0.5
DSW_GW = DSW_HEADS * DSW_HEAD_DIM
BAND = 128
ROPE_THETA = 10000.0
PAST_LEN = 8192
LANES = 128
VMEM_LIMIT = 56 * 1024 * 1024


def _nn(a, b):
    return jnp.dot(a, b, preferred_element_type=F32)


def _nt(a, b):
    return lax.dot_general(a, b, (((1,), (1,)), ((), ())), preferred_element_type=F32)


def _tn(a, b):
    return lax.dot_general(a, b, (((0,), (0,)), ((), ())), preferred_element_type=F32)


def _split3(x):
    hi = x.astype(BF16)
    r1 = x - hi.astype(F32)
    mid = r1.astype(BF16)
    lo = (r1 - mid.astype(F32)).astype(BF16)
    return hi, mid, lo


def _iota(shape, dim):
    return lax.broadcasted_iota(jnp.int32, shape, dim)


def _rms(x, g):
    return x * lax.rsqrt(jnp.mean(x * x, axis=-1, keepdims=True) + EPS) * g


def _params(n_parallel, n_arbitrary=0):
    sem = ("parallel",) * n_parallel + ("arbitrary",) * n_arbitrary
    return pltpu.CompilerParams(dimension_semantics=sem, vmem_limit_bytes=VMEM_LIMIT)


def _resident(shape):
    nd = len(shape)
    return pl.BlockSpec(shape, lambda *_: (0,) * nd, pipeline_mode=pl.Buffered(1))


def _ada_kernel(c_ref, w_ref, b_ref, o_ref):
    c = c_ref[...]
    a = (c * jax.nn.sigmoid(c)).astype(BF16)
    o_ref[...] = _nn(a, w_ref[...].astype(BF16)) + b_ref[...]


def _ada(c_all, w_ada, b_ada):
    n, d = c_all.shape
    ncol = w_ada.shape[1]
    tn = 1536
    return pl.pallas_call(
        _ada_kernel,
        grid=(ncol // tn,),
        in_specs=[pl.BlockSpec((n, d), lambda j: (0, 0)),
                  pl.BlockSpec((d, tn), lambda j: (0, j)),
                  pl.BlockSpec((1, tn), lambda j: (0, j))],
        out_specs=pl.BlockSpec((n, tn), lambda j: (0, j)),
        out_shape=jax.ShapeDtypeStruct((n, ncol), F32),
        compiler_params=_params(1),
        name="ada_mod",
    )(c_all, w_ada, b_ada.reshape(1, ncol))


def _rope(x, cosf, sins, first_half):
    rot = jnp.where(first_half, pltpu.roll(x, LANES - 32, 1), pltpu.roll(x, 32, 1))
    return x * cosf + rot * sins


def _inproj_kernel(x_ref, sc_ref, sh_ref, g_ref, cos_ref, sin_ref,
                   wa_ref, ba_ref, wg_ref, bg_ref, w2_ref, b2_ref, wb_ref, bb_ref, wc_ref, bc_ref,
                   gq_ref, gk_ref, gv_ref, gr_ref, la_ref, ga_ref, gb_ref, *rest, fold):
    x = x_ref[...]
    h = (_rms(x, g_ref[...]) * (1.0 + sc_ref[...]) + sh_ref[...]).astype(BF16)

    cosf = cos_ref[...]
    sins = sin_ref[...]
    first_half = (_iota(cosf.shape, 1) % DSW_HEAD_DIM) < (DSW_HEAD_DIM // 2)
    tm = x.shape[0]
    per_group = DSW_GW // LANES
    width = 3 * DSW_GW

    def proj(off, g):
        cols = slice(off + g * DSW_GW, off + (g + 1) * DSW_GW)
        full = _nn(h, wb_ref[:, cols]) + bb_ref[:, cols]
        return [full[:, s * LANES:(s + 1) * LANES] for s in range(per_group)]

    for g, (_, dil) in enumerate(DSW_GROUPS):
        qs = [_rope(a, cosf, sins, first_half) * DSW_SCALE for a in proj(0, g)]
        ks = [_rope(a, cosf, sins, first_half) for a in proj(width, g)]
        vs = proj(2 * width, g)
        if fold:
            kvt_ref = rest[3 * len(DSW_GROUPS) + g]
            kvt_ref[0:DSW_GW, :] = jnp.concatenate(ks, axis=1).T
            kvt_ref[DSW_GW:2 * DSW_GW, :] = jnp.concatenate(vs, axis=1).T

        for slab in range(per_group):
            cols = slice(g * DSW_GW + slab * LANES, g * DSW_GW + (slab + 1) * LANES)
            lanes = slice(slab * LANES, (slab + 1) * LANES)
            if not fold:
                rest[0][:, cols] = ks[slab]
                rest[1][:, cols] = vs[slab]
                rest[2][:, cols] = qs[slab]
                continue
            scratch = rest[-1]
            for which, val in enumerate((qs[slab], ks[slab], vs[slab])):
                out_ref = rest[3 * g + which]
                if dil == 1:
                    out_ref[:, lanes] = val.astype(BF16)
                else:
                    scratch[which] = val
                    for r in range(dil):
                        out_ref[r, :, lanes] = scratch[which, pl.ds(r, tm // dil, stride=dil), :].astype(BF16)

    glr = (_nn(h, wg_ref[...]) + bg_ref[...]).astype(BF16)
    z = _nn(glr, w2_ref[...]) + b2_ref[...]
    la_ref[...] = jax.nn.log_sigmoid(z) * (1.0 / GLA_TAU)

    gq_ref[...] = (_nn(h, wa_ref[:, 0:512]) + ba_ref[:, 0:512]).astype(gq_ref.dtype)
    gk_ref[...] = (_nn(h, wa_ref[:, 512:1024]) + ba_ref[:, 512:1024]).astype(gk_ref.dtype)
    gv_ref[...] = (_nn(h, wa_ref[:, 1024:2048]) + ba_ref[:, 1024:2048]).astype(gv_ref.dtype)
    gr_ref[...] = (_nn(h, wa_ref[:, 2048:3072]) + ba_ref[:, 2048:3072]).astype(gr_ref.dtype)

    ga_ref[...] = (_nn(h, wc_ref[:, 0:1024]) + bc_ref[:, 0:1024]).astype(ga_ref.dtype)
    gb_ref[...] = (_nn(h, wc_ref[:, 1024:2048]) + bc_ref[:, 1024:2048]).astype(gb_ref.dtype)


def _mod_spec(arr, tm):
    if arr.shape[1] == 1:
        return pl.BlockSpec((None, 1, arr.shape[2]), lambda b, i: (b, 0, 0))
    return pl.BlockSpec((None, tm, arr.shape[2]), lambda b, i: (b, i, 0))


def _inproj(x, sc, sh, g, cos_t, sin_t, w, tm, fold):
    nb, t, d = x.shape
    tok = lambda n: pl.BlockSpec((None, tm, n), lambda b, i: (b, i, 0))
    out_cols = (512, 512, 1024, 1024, 512, 1024, 1024)
    out_dt = (BF16, BF16, BF16, BF16, F32, BF16, BF16)
    out_specs = [tok(n) for n in out_cols]
    out_shape = [jax.ShapeDtypeStruct((nb, t, n), dt) for n, dt in zip(out_cols, out_dt)]
    scratch = []
    if fold:
        for _, dil in DSW_GROUPS:
            for _ in range(3):
                if dil == 1:
                    out_specs.append(tok(DSW_GW))
                    out_shape.append(jax.ShapeDtypeStruct((nb, t, DSW_GW), BF16))
                else:
                    out_specs.append(pl.BlockSpec((None, dil, tm // dil, DSW_GW), lambda b, i: (b, 0, i, 0)))
                    out_shape.append(jax.ShapeDtypeStruct((nb, dil, t // dil, DSW_GW), BF16))
        for win, _ in DSW_GROUPS:
            width = min(max(win, tm), t)
            first = (t - width) // tm
            out_specs.append(pl.BlockSpec((None, 2 * DSW_GW, tm),
                                          lambda b, i, first=first: (b, 0, jnp.maximum(i - first, 0))))
            out_shape.append(jax.ShapeDtypeStruct((nb, 2 * DSW_GW, width), F32))
        scratch = [pltpu.VMEM((3, tm, LANES), F32)]
    else:
        for _ in range(3):
            out_specs.append(tok(3 * DSW_GW))
            out_shape.append(jax.ShapeDtypeStruct((nb, t, 3 * DSW_GW), F32))
    weights = (w["wa"], w["ba"], w["wg"], w["bg"], w["w2"], w["b2"], w["wb"], w["bb"], w["wc"], w["bc"])
    return pl.pallas_call(
        functools.partial(_inproj_kernel, fold=fold),
        grid=(nb, t // tm),
        in_specs=[tok(d), _mod_spec(sc, tm), _mod_spec(sh, tm), _resident((1, d)),
                  pl.BlockSpec((tm, LANES), lambda b, i: (i, 0)),
                  pl.BlockSpec((tm, LANES), lambda b, i: (i, 0))]
                 + [_resident(a.shape) for a in weights],
        out_specs=out_specs,
        out_shape=out_shape,
        scratch_shapes=scratch,
        compiler_params=_params(1, 1),
        name="inproj",
    )(x, sc, sh, g, cos_t, sin_t, *weights)


def _gla_local(gq_ref, gk_ref, la_ref, chunk):
    la = la_ref[...]
    tt = la.shape[0]
    r = _iota((tt, tt), 0)
    c = _iota((tt, tt), 1)
    same = (r // chunk) == (c // chunk)
    tri = jnp.where(same & (c <= r), 1.0, 0.0).astype(BF16)
    hi, mid, lo = _split3(la)
    b = _nn(tri, hi) + _nn(tri, mid) + _nn(tri, lo)
    if chunk % 8 == 0:
        bl = jnp.concatenate([jnp.broadcast_to(b[e - 1:e, :], (chunk, b.shape[1]))
                              for e in range(chunk, tt + 1, chunk)], axis=0)
    else:
        ones = jnp.where(same, 1.0, 0.0).astype(BF16)
        bl = _nn(ones, hi) + _nn(ones, mid) + _nn(ones, lo)
    gq = gq_ref[...].astype(F32)
    gk = gk_ref[...].astype(F32)
    qg = (gq * GLA_SCALE * jnp.exp(b)).astype(BF16)
    kd = (gk * jnp.exp(-b)).astype(BF16)
    kl = (gk * jnp.exp(bl - b)).astype(BF16)
    causal = same & (c <= r)
    return qg, kd, kl, jnp.exp(bl), causal


def _gla_finish(o, gr, g):
    return (_rms(o, g) * (gr * jax.nn.sigmoid(gr))).astype(BF16)


def _gla_prompt_body(first_tile, gq_ref, gk_ref, gv_ref, la_ref, gr_ref, g_ref, o_ref, st_ref):
    @pl.when(first_tile)
    def _():
        st_ref[...] = jnp.zeros_like(st_ref)

    qg, kd, kl, dec, causal = _gla_local(gq_ref, gk_ref, la_ref, GLA_CHUNK)
    tt = qg.shape[0]
    chunk_of_row = _iota((tt, GLA_DK), 0) // GLA_CHUNK
    for h in range(GLA_HEADS):
        kc = slice(h * GLA_DK, (h + 1) * GLA_DK)
        vc = slice(h * GLA_DV, (h + 1) * GLA_DV)
        v = gv_ref[:, vc].astype(BF16)
        att = jnp.where(causal, _nt(qg[:, kc], kd[:, kc]), 0.0).astype(BF16)
        intra = _nn(att, v)
        st = st_ref[h]
        n_chunks = tt // GLA_CHUNK
        kl_h = kl[:, kc]
        kl_bd = jnp.concatenate([jnp.where(chunk_of_row == ci, kl_h, jnp.zeros_like(kl_h))
                                 for ci in range(n_chunks)], axis=1)
        upd = _tn(v, kl_bd)
        inter = []
        for ci in range(n_chunks):
            rows = slice(ci * GLA_CHUNK, (ci + 1) * GLA_CHUNK)
            inter.append(_nt(qg[rows, kc], st.astype(BF16)))
            st = dec[ci * GLA_CHUNK:ci * GLA_CHUNK + 1, kc] * st + upd[:, ci * GLA_DK:(ci + 1) * GLA_DK]
        st_ref[h] = st
        o = intra + jnp.concatenate(inter, axis=0)
        o_ref[:, vc] = _gla_finish(o, gr_ref[:, vc].astype(F32), g_ref[...])


def _gla_prompt_call(gq, gk, gv, la, gr, g, tt):
    nb, t, _ = gq.shape
    tiles = t // tt
    tok = lambda n: pl.BlockSpec((None, tt, n), lambda i: (i // tiles, i % tiles, 0))
    def body(*refs):
        _gla_prompt_body(pl.program_id(0) % tiles == 0, *refs)

    return dict(
        name="gla_prompt", steps=nb * tiles, body=body, args=(gq, gk, gv, la, gr, g),
        in_specs=[tok(512), tok(512), tok(1024), tok(512), tok(1024), _resident((1, GLA_DV))],
        out_specs=[tok(1024),
                   pl.BlockSpec((None, GLA_HEADS, GLA_DV, GLA_DK), lambda i: (i // tiles, 0, 0, 0))],
        out_shape=[jax.ShapeDtypeStruct((nb, t, 1024), BF16),
                   jax.ShapeDtypeStruct((nb, GLA_HEADS, GLA_DV, GLA_DK), F32)])


def _run_jobs(*jobs):
    steps = jobs[0]["steps"]
    assert all(j["steps"] == steps for j in jobs)
    n_in = [len(j["in_specs"]) for j in jobs]
    n_out = [len(j["out_specs"]) for j in jobs]

    def kern(*refs):
        i_pos, o_pos = 0, sum(n_in)
        for j, ni, no in zip(jobs, n_in, n_out):
            j["body"](*refs[i_pos:i_pos + ni], *refs[o_pos:o_pos + no])
            i_pos, o_pos = i_pos + ni, o_pos + no

    outs = pl.pallas_call(
        kern,
        grid=(steps,),
        in_specs=[s for j in jobs for s in j["in_specs"]],
        out_specs=[s for j in jobs for s in j["out_specs"]],
        out_shape=[s for j in jobs for s in j["out_shape"]],
        compiler_params=_params(0, 1),
        name="__".join(j["name"] for j in jobs),
    )(*[a for j in jobs for a in j["args"]])
    split, pos = [], 0
    for no in n_out:
        split.append(list(outs[pos:pos + no]))
        pos += no
    return split


def _gla_sample_kernel(gq_ref, gk_ref, gv_ref, la_ref, gr_ref, g_ref, s_ref, o_ref, so_ref, *, seq):
    qg, kd, kl, dec, causal = _gla_local(gq_ref, gk_ref, la_ref, seq)
    rows_total = qg.shape[0]
    per8 = 8 // seq
    row8 = _iota((8, 1), 0)
    for h in range(GLA_HEADS):
        kc = slice(h * GLA_DK, (h + 1) * GLA_DK)
        vc = slice(h * GLA_DV, (h + 1) * GLA_DV)
        v = gv_ref[:, vc].astype(BF16)
        att = jnp.where(causal, _nt(qg[:, kc], kd[:, kc]), 0.0).astype(BF16)
        intra = _nn(att, v)
        inter = []
        for p in range(rows_total // 8):
            rows = slice(p * 8, (p + 1) * 8)
            d_hi, d_mid, d_lo = _split3(dec[rows, kc])
            inter_p = jnp.zeros((8, GLA_DV), F32)
            for j in range(per8):
                b = p * per8 + j
                r0 = j * seq
                s0 = s_ref[b, h]
                mine = (row8 >= r0) & (row8 < r0 + seq)
                inter_p = jnp.where(mine, _nn(qg[rows, kc], s0.astype(BF16)), inter_p)
                dl = jnp.where(row8 == r0, d_hi, jnp.where(row8 == r0 + 1, d_mid,
                               jnp.where(row8 == r0 + 2, d_lo, jnp.zeros_like(d_lo))))
                e = jnp.where((row8 >= r0) & (row8 < r0 + 3), 1.0, 0.0).astype(BF16)
                dec_b = _tn(dl, jnp.broadcast_to(e, (8, GLA_DV)))
                upd = _tn(jnp.where(mine, kl[rows, kc], jnp.zeros_like(kl[rows, kc])), v[rows])
                so_ref[b, h] = dec_b * s0 + upd
            inter.append(inter_p)
        o = intra + jnp.concatenate(inter, axis=0)
        o_ref[:, vc] = _gla_finish(o, gr_ref[:, vc].astype(F32), g_ref[...])


def _gla_sample_call(gq, gk, gv, la, gr, g, s0, seq, bb):
    n_seq = s0.shape[0]
    rows = bb * seq
    tok = lambda n: pl.BlockSpec((None, rows, n), lambda i: (0, i, 0))
    st = pl.BlockSpec((bb, GLA_HEADS, GLA_DK, GLA_DV), lambda i: (i, 0, 0, 0))
    return dict(
        name="gla_sample", steps=n_seq // bb, body=functools.partial(_gla_sample_kernel, seq=seq),
        args=(gq, gk, gv, la, gr, g, s0),
        in_specs=[tok(512), tok(512), tok(1024), tok(512), tok(1024), _resident((1, GLA_DV)), st],
        out_specs=[tok(1024), st],
        out_shape=[jax.ShapeDtypeStruct((1, n_seq * seq, 1024), BF16),
                   jax.ShapeDtypeStruct(s0.shape, F32)])


def _dsw_prompt_kernel(q_ref, kp_ref, kc_ref, vp_ref, vc_ref, o_ref, lse_ref, *, dil, tile, res):
    qb = q_ref.shape[0]
    first_key = jnp.where(tile == 0, BAND, 0)
    qi = _iota((BAND, 2 * BAND), 0) + BAND
    ki = _iota((BAND, 2 * BAND), 1)
    band = (qi - ki >= 0) & (qi - ki <= BAND)
    lane = _iota((BAND, LANES), 1)
    for s in range(qb // BAND):
        rows = slice(s * BAND, (s + 1) * BAND)
        if s == 0:
            valid = band & (ki >= first_key)
        else:
            valid = band
        if dil == 1:
            tok_rows = rows
        else:
            tok_rows = pl.ds(s * BAND * dil + res, BAND, stride=dil)
        for hp in range(DSW_GW // LANES):
            cols = slice(hp * LANES, (hp + 1) * LANES)
            qp = q_ref[rows, cols]
            if s == 0:
                kcat = jnp.concatenate([kp_ref[:, cols], kc_ref[0:BAND, cols]], axis=0)
                vcat = jnp.concatenate([vp_ref[:, cols], vc_ref[0:BAND, cols]], axis=0)
            else:
                kcat = kc_ref[(s - 1) * BAND:(s + 1) * BAND, cols]
                vcat = vc_ref[(s - 1) * BAND:(s + 1) * BAND, cols]
            outs, lses = [], []
            for hh in range(LANES // DSW_HEAD_DIM):
                in_head = (lane // DSW_HEAD_DIM) == hh
                sc = _nt(jnp.where(in_head, qp, jnp.zeros_like(qp)), kcat)
                sc = jnp.where(valid, sc, -jnp.inf)
                m = jnp.max(sc, axis=-1, keepdims=True)
                e = jnp.exp(sc - m)
                den = jnp.sum(e, axis=-1, keepdims=True)
                outs.append(_nn((e / den).astype(BF16), vcat))
                lses.append(m + jnp.log(den))
            first = lane < DSW_HEAD_DIM
            o_ref[hp, tok_rows, :] = jnp.where(first, outs[0], outs[1])
            lse_ref[hp, tok_rows, :] = jnp.where(first, lses[0], jnp.broadcast_to(lses[1], (BAND, LANES)))


def _dsw_prompt_call(q, k, v, dil):
    nb = q.shape[0]
    seq_len = q.shape[-2]
    t = seq_len * dil
    tq = min(512, seq_len)
    sub = tq // BAND
    n_tiles = seq_len // tq
    bat = lambda i: i // (n_tiles * dil)
    til = lambda i: (i // dil) % n_tiles
    res = lambda i: i % dil
    prev_blk = lambda i: jnp.maximum(til(i) * sub - 1, 0)
    if dil == 1:
        cur = pl.BlockSpec((None, tq, DSW_GW), lambda i: (bat(i), til(i), 0))
        prev = pl.BlockSpec((None, BAND, DSW_GW), lambda i: (bat(i), prev_blk(i), 0))
    else:
        cur = pl.BlockSpec((None, None, tq, DSW_GW), lambda i: (bat(i), res(i), til(i), 0))
        prev = pl.BlockSpec((None, None, BAND, DSW_GW), lambda i: (bat(i), res(i), prev_blk(i), 0))
    n_slab = DSW_GW // LANES
    out = pl.BlockSpec((None, n_slab, tq * dil, LANES), lambda i: (bat(i), 0, til(i), 0))

    def body(*refs):
        i = pl.program_id(0)
        _dsw_prompt_kernel(*refs, dil=dil, tile=til(i), res=res(i))

    return dict(
        name=f"dsw_prompt_d{dil}", steps=nb * n_tiles * dil, body=body, args=(q, k, k, v, v),
        in_specs=[cur, prev, cur, prev, cur],
        out_specs=[out, out],
        out_shape=[jax.ShapeDtypeStruct((nb, n_slab, t, LANES), F32)] * 2)


def _dsw_sample_kernel(q_ref, kn_ref, vn_ref, c_ref, o_ref, lse_ref, co_ref, *, seq, dil):
    per8 = 8 // seq
    win = c_ref.shape[2]
    n_rows = DSW_HEADS * 8
    lane = _iota((8, LANES), 1)
    head_of_lane = _iota((8, DSW_GW), 1) // DSW_HEAD_DIM
    row8 = _iota((8, 1), 0)
    r = _iota((n_rows, 1), 0)
    r_step = r % seq
    r_seq = (r % 8) // seq
    key = _iota((n_rows, win), 1)
    cache_ok = ((key % dil) == (r_step % dil)) & (key >= r_step)
    c128 = _iota((n_rows, LANES), 1)
    new_ok = ((c128 < 8) & ((c128 // seq) == r_seq) & ((c128 % seq) <= r_step)
              & (((r_step - c128 % seq) % dil) == 0))
    pad = jnp.zeros((LANES - 8, DSW_GW), BF16)

    def by_head(x):
        out = x[(DSW_HEADS - 1) * 8:DSW_HEADS * 8]
        for h in range(DSW_HEADS - 2, -1, -1):
            out = jnp.where(head_of_lane == h, x[h * 8:(h + 1) * 8], out)
        return out

    lane_sq = _iota((LANES, LANES), 1)
    p_row = _iota((8, LANES), 0)
    for grp in range(q_ref.shape[0] // 8):
        r8 = slice(grp * 8, (grp + 1) * 8)
        q8 = q_ref[r8, :]
        qrows = jnp.concatenate([jnp.where(head_of_lane == h, q8, 0.0) for h in range(DSW_HEADS)],
                                axis=0).astype(BF16)
        kn8 = kn_ref[r8, :]
        vn8 = vn_ref[r8, :]
        kn_t = jnp.concatenate([kn8.astype(BF16), pad], axis=0)
        vn_t = jnp.concatenate([vn8.astype(BF16), pad], axis=0)
        scn = jnp.where(new_ok, _nt(qrows, kn_t), -jnp.inf)
        m_new = jnp.max(scn, axis=-1, keepdims=True)
        o_p = jnp.zeros((8, DSW_GW), F32)
        l_p = jnp.zeros((8, DSW_GW), F32)
        for j in range(per8):
            b = grp * per8 + j
            kt = c_ref[b, 0:DSW_GW, :].astype(BF16)
            vt = c_ref[b, DSW_GW:2 * DSW_GW, :].astype(BF16)
            sc = jnp.where(cache_ok, _nn(qrows, kt), -jnp.inf)
            m = jnp.maximum(jnp.max(sc, axis=-1, keepdims=True), m_new)
            e = jnp.exp(sc - m)
            en = jnp.exp(scn - m)
            den = jnp.sum(e, axis=-1, keepdims=True) + jnp.sum(en, axis=-1, keepdims=True)
            o = _nt((e / den).astype(BF16), vt) + _nn((en / den).astype(BF16), vn_t)
            lse = jnp.broadcast_to(m + jnp.log(den), (n_rows, DSW_GW))
            mine = (row8 // seq) == j
            o_p = jnp.where(mine, by_head(o), o_p)
            l_p = jnp.where(mine, by_head(lse), l_p)
        o_ref[r8, :] = o_p
        lse_ref[r8, :] = l_p

        hi, mid, lo = _split3(jnp.concatenate([kn8, vn8], axis=1))
        for j in range(per8):
            b = grp * per8 + j
            place = jnp.where(((p_row // seq) == j) & (lane == LANES - seq + p_row % seq), 1.0, 0.0).astype(BF16)
            new_cols = _tn(hi, place) + _tn(mid, place) + _tn(lo, place)
            for blk in range(2 * DSW_GW // LANES):
                rows = slice(blk * LANES, (blk + 1) * LANES)
                rolled = pltpu.roll(c_ref[b, rows, :], win - seq, 1)
                if win > LANES:
                    co_ref[b, rows, 0:win - LANES] = rolled[:, 0:win - LANES]
                co_ref[b, rows, win - LANES:win] = jnp.where(lane_sq < LANES - seq, rolled[:, win - LANES:win],
                                                             new_cols[rows])


def _dsw_sample_call(q32, k32, v32, cache, g, dil, seq, steps=None):
    n_seq, win = cache.shape[0], cache.shape[1]
    per8 = 8 // seq
    if steps is None:
        groups = max(1, min(8, 4 * 512 // win))
    else:
        groups = n_seq // (per8 * steps)
    n_blk = per8 * groups
    view = jnp.transpose(cache, (0, 2, 3, 4, 1)).reshape(n_seq, 2 * DSW_GW, win)
    tok = pl.BlockSpec((None, 8 * groups, DSW_GW), lambda i: (0, i, g))
    tok_out = pl.BlockSpec((None, 8 * groups, DSW_GW), lambda i: (0, i, 0))
    cspec = pl.BlockSpec((n_blk, 2 * DSW_GW, win), lambda i: (i, 0, 0))
    return dict(
        name=f"dsw_sample_d{dil}", steps=n_seq // n_blk,
        body=functools.partial(_dsw_sample_kernel, seq=seq, dil=dil), args=(q32, k32, v32, view),
        in_specs=[tok, tok, tok, cspec],
        out_specs=[tok_out, tok_out, cspec],
        out_shape=[jax.ShapeDtypeStruct((1, n_seq * seq, DSW_GW), F32)] * 2
                  + [jax.ShapeDtypeStruct(view.shape, F32)])


def _dsw_sample_finish(o, lse, new, cache_shape):
    n_seq, win = cache_shape[0], cache_shape[1]
    new = jnp.transpose(new.reshape(n_seq, 2, DSW_HEADS, DSW_HEAD_DIM, win), (0, 4, 1, 2, 3))
    n_slab = DSW_GW // LANES
    slabs = lambda a: jnp.transpose(a.reshape(1, -1, n_slab, LANES), (0, 2, 1, 3))
    return slabs(o), slabs(lse), new


def _merge_kernel(oa_ref, o0_ref, o1_ref, o2_ref, l0_ref, l1_ref, l2_ref, ga_ref, gb_ref, x_ref,
                  g1_ref, sc_ref, sh_ref, n2_ref, wpa_ref, wpb_ref, wo_ref, x1_ref, h2_ref):
    ob = []
    for slab in range(DSW_GW // LANES):
        l0, l1, l2 = l0_ref[slab], l1_ref[slab], l2_ref[slab]
        m = jnp.maximum(jnp.maximum(l0, l1), l2)
        w0, w1, w2 = jnp.exp(l0 - m), jnp.exp(l1 - m), jnp.exp(l2 - m)
        den = w0 + w1 + w2
        ob.append((w0 / den) * o0_ref[slab] + (w1 / den) * o1_ref[slab] + (w2 / den) * o2_ref[slab])
    ob = jnp.concatenate(ob, axis=1).astype(BF16)
    merged = (jax.nn.sigmoid(ga_ref[...].astype(F32)) * _nn(oa_ref[...], wpa_ref[...])
              + jax.nn.sigmoid(gb_ref[...].astype(F32)) * _nn(ob, wpb_ref[...]))
    x1 = x_ref[...] + g1_ref[...] * _nn(merged.astype(BF16), wo_ref[...])
    x1_ref[...] = x1
    h2_ref[...] = (_rms(x1, n2_ref[...]) * (1.0 + sc_ref[...]) + sh_ref[...]).astype(BF16)


def _flat_tok_spec(tm, tiles, n):
    return pl.BlockSpec((None, tm, n), lambda i: (i // tiles, i % tiles, 0))


def _flat_mod_spec(arr, tm, tiles):
    if arr.shape[1] == 1:
        return pl.BlockSpec((None, 1, arr.shape[2]), lambda i: (i // tiles, 0, 0))
    return pl.BlockSpec((None, tm, arr.shape[2]), lambda i: (i // tiles, i % tiles, 0))


def _merge_call(oa, og, lg, ga, gb, x, g1, sc2, sh2, n2, wpa, wpb, wo, tm):
    nb, t, d = x.shape
    tiles = t // tm
    tok = functools.partial(_flat_tok_spec, tm, tiles)
    mod = lambda a: _flat_mod_spec(a, tm, tiles)
    slab = pl.BlockSpec((None, DSW_GW // LANES, tm, LANES), lambda i: (i // tiles, 0, i % tiles, 0))
    return dict(
        name="merge_outproj", steps=nb * tiles, body=_merge_kernel,
        args=(oa, *og, *lg, ga, gb, x, g1, sc2, sh2, n2, wpa, wpb, wo),
        in_specs=[tok(1024)] + [slab] * 6 + [tok(d), tok(d), tok(d), mod(g1), mod(sc2), mod(sh2),
                  _resident((1, d)), _resident(wpa.shape), _resident(wpb.shape), _resident(wo.shape)],
        out_specs=[tok(d), tok(d)],
        out_shape=[jax.ShapeDtypeStruct((nb, t, d), F32), jax.ShapeDtypeStruct((nb, t, d), BF16)])


def _ffn_kernel(h_ref, x_ref, g2_ref, nf_ref, wu_ref, wd_ref, y_ref, *, final_norm, n_split):
    h = h_ref[...]
    d_ff = wd_ref.shape[0]
    step = d_ff // n_split
    acc = None
    for j in range(n_split):
        u1 = _nn(h, wu_ref[:, j * step:(j + 1) * step])
        u2 = _nn(h, wu_ref[:, d_ff + j * step:d_ff + (j + 1) * step])
        a = (u1 * jax.nn.sigmoid(u1) * u2).astype(BF16)
        part = _nn(a, wd_ref[j * step:(j + 1) * step, :])
        acc = part if acc is None else acc + part
    x2 = x_ref[...] + g2_ref[...] * acc
    y_ref[...] = _rms(x2, nf_ref[...]) if final_norm else x2


def _ffn_call(h2, x1, g2, nf, wu, wd, tm, final_norm):
    nb, t, d = x1.shape
    tiles = t // tm
    tok = functools.partial(_flat_tok_spec, tm, tiles)
    return dict(
        name="ffn", steps=nb * tiles,
        body=functools.partial(_ffn_kernel, final_norm=final_norm, n_split=wd.shape[0] // (2 * LANES)),
        args=(h2, x1, g2, nf, wu, wd),
        in_specs=[tok(d), tok(d), _flat_mod_spec(g2, tm, tiles), _resident((1, d)),
                  _resident(wu.shape), _resident(wd.shape)],
        out_specs=[tok(d)],
        out_shape=[jax.ShapeDtypeStruct((nb, t, d), F32)])


def _rope_tables(pos):
    half = DSW_HEAD_DIM // 2
    inv = ROPE_THETA ** (-np.arange(half, dtype=np.float64) / half)
    ang = np.asarray(pos, np.float64)[:, None] * inv[None, :]
    reps = LANES // half
    sign = np.tile(np.concatenate([-np.ones(half), np.ones(half)]), LANES // DSW_HEAD_DIM)
    cosf = np.tile(np.cos(ang), (1, reps))
    sins = np.tile(np.sin(ang), (1, reps)) * sign[None, :]
    return jnp.asarray(cosf, F32), jnp.asarray(sins, F32)


def _layer_weights(w_in, b_in, w_alpha2, b_alpha2, w_proj_a, w_proj_b, w_out, w_up, w_down):
    bf = lambda a: a.astype(BF16)
    row = lambda a: a.reshape(1, -1)
    o_glr, o_dq, o_ga = 3072, 3088, 5392
    pad_r = LANES - GLA_RANK
    return dict(
        wa=bf(w_in[:, :o_glr]), ba=row(b_in[:o_glr]),
        wg=bf(jnp.pad(w_in[:, o_glr:o_dq], ((0, 0), (0, pad_r)))), bg=row(jnp.pad(b_in[o_glr:o_dq], (0, pad_r))),
        w2=bf(jnp.pad(w_alpha2, ((0, pad_r), (0, 0)))), b2=row(b_alpha2),
        wb=bf(w_in[:, o_dq:o_ga]), bb=row(b_in[o_dq:o_ga]),
        wc=bf(w_in[:, o_ga:]), bc=row(b_in[o_ga:]),
        wpa=bf(w_proj_a), wpb=bf(w_proj_b), wo=bf(w_out), wu=bf(w_up), wd=bf(w_down))


def _kv_unstack(kvt, keep):
    nb, _, width = kvt.shape
    kv = kvt[:, :, width - keep:].reshape(nb, 2, DSW_HEADS, DSW_HEAD_DIM, keep)
    return jnp.transpose(kv, (0, 4, 1, 2, 3))


def kernel(x_prompt, x_sample, state_gla, cache_kv_w128, cache_kv_w512, cache_kv_w2048, c_prompt, c_sample,
           norm1_g, norm2_g, w_ada, b_ada, w_in, b_in, w_alpha2, b_alpha2, gla_norm_g, w_proj_a, w_proj_b,
           w_out, w_up, w_down, normf_g):
    depth = w_ada.shape[0]
    nb, t, d = x_prompt.shape
    n_seq, seq, _ = x_sample.shape
    assert 8 % seq == 0 and seq >= 3, "sample kernels pack whole sequences into 8-row groups"
    past = PAST_LEN
    caches = (cache_kv_w128, cache_kv_w512, cache_kv_w2048)

    cos_p, sin_p = _rope_tables(np.arange(t))
    cos_s, sin_s = _rope_tables(np.tile(past + np.arange(seq), n_seq))

    n_c = nb + n_seq
    pad_c = (-n_c) % 8
    c_all = jnp.pad(jnp.concatenate([c_prompt, c_sample], axis=0), ((0, pad_c), (0, 0)))

    xp = x_prompt
    xs = x_sample.reshape(1, n_seq * seq, d)
    row = lambda a: a.reshape(1, -1)
    sp_l, kvp_l, ss_l, kvs_l = [], [], [], []
    for l in range(depth):
        w = _layer_weights(w_in[l], b_in[l], w_alpha2[l], b_alpha2[l], w_proj_a[l], w_proj_b[l],
                           w_out[l], w_up[l], w_down[l])
        mod = _ada(c_all, w_ada[l], b_ada[l])
        mod_p = [mod[:nb, i * d:(i + 1) * d].reshape(nb, 1, d) for i in range(6)]
        mod_s = [jnp.repeat(mod[nb:nb + n_seq, i * d:(i + 1) * d], seq, axis=0).reshape(1, n_seq * seq, d)
                 for i in range(6)]
        last = l == depth - 1

        sh1_s, sc1_s, g1_s, sh2_s, sc2_s, g2_s = mod_s
        gq_s, gk_s, gv_s, gr_s, la_s, ga_s, gb_s, k32, v32, q32 = _inproj(
            xs, sc1_s, sh1_s, row(norm1_g[l]), cos_s, sin_s, w, tm=256, fold=False)
        sh1, sc1, g1, sh2, sc2, g2 = mod_p
        gq, gk, gv, gr, la, ga, gb, *dsw_p = _inproj(
            xp, sc1, sh1, row(norm1_g[l]), cos_p, sin_p, w, tm=512, fold=True)
        qkv, kvt = dsw_p[:3 * len(DSW_GROUPS)], dsw_p[3 * len(DSW_GROUPS):]

        def with_sample_group(host, g):
            steps = host["steps"]
            fits = n_seq % ((8 // seq) * steps) == 0 and n_seq // ((8 // seq) * steps) <= 8
            job = _dsw_sample_call(q32, k32, v32, caches[g][l], g, DSW_GROUPS[g][1], seq, steps if fits else None)
            if fits:
                host_out, job_out = _run_jobs(host, job)
            else:
                (host_out,), (job_out,) = _run_jobs(host), _run_jobs(job)
            return host_out, _dsw_sample_finish(*job_out, caches[g][l].shape)

        sample_dsw = [None] * len(DSW_GROUPS)
        (oa, st), sample_dsw[2] = with_sample_group(
            _gla_prompt_call(gq, gk, gv, la, gr, row(gla_norm_g[l]), tt=256), 2)
        dsw_jobs = [_dsw_prompt_call(*qkv[3 * g:3 * g + 3], dil) for g, (_, dil) in enumerate(DSW_GROUPS)]
        gla_s_args = (gq_s, gk_s, gv_s, la_s, gr_s, row(gla_norm_g[l]), state_gla[l], seq)
        host = next((j for j in dsw_jobs if n_seq % j["steps"] == 0 and (n_seq // j["steps"]) * seq % 16 == 0), None)
        og, lg = [], []
        for job in dsw_jobs:
            if job is host:
                (o_g, l_g), (oa_s, s_new) = _run_jobs(job, _gla_sample_call(*gla_s_args, bb=n_seq // job["steps"]))
            else:
                ((o_g, l_g),) = _run_jobs(job)
            og.append(o_g)
            lg.append(l_g)
        if host is None:
            ((oa_s, s_new),) = _run_jobs(_gla_sample_call(*gla_s_args, bb=8))

        (x1, h2), sample_dsw[0] = with_sample_group(
            _merge_call(oa, og, lg, ga, gb, xp, g1, sc2, sh2, row(norm2_g[l]), w["wpa"], w["wpb"], w["wo"], tm=512), 0)
        (xp,), sample_dsw[1] = with_sample_group(
            _ffn_call(h2, x1, g2, row(normf_g), w["wu"], w["wd"], tm=512, final_norm=last), 1)
        og_s, lg_s, new_kv = zip(*sample_dsw)

        ((x1, h2),) = _run_jobs(_merge_call(oa_s, og_s, lg_s, ga_s, gb_s, xs, g1_s, sc2_s, sh2_s, row(norm2_g[l]),
                                            w["wpa"], w["wpb"], w["wo"], tm=256))
        ((xs,),) = _run_jobs(_ffn_call(h2, x1, g2_s, row(normf_g), w["wu"], w["wd"], tm=256, final_norm=last))

        sp_l.append(jnp.swapaxes(st, 2, 3))
        kvp_l.append(tuple(_kv_unstack(kvt[g], min(win, t)) for g, (win, _) in enumerate(DSW_GROUPS)))
        ss_l.append(s_new)
        kvs_l.append(new_kv)

    y_prompt = xp
    y_sample = xs.reshape(n_seq, seq, d)
    stack = lambda items: jnp.stack(list(items))
    return (y_prompt, y_sample, stack(sp_l),
            stack(kv[0] for kv in kvp_l), stack(kv[1] for kv in kvp_l), stack(kv[2] for kv in kvp_l),
            stack(ss_l),
            stack(kv[0] for kv in kvs_l), stack(kv[1] for kv in kvs_l), stack(kv[2] for kv in kvs_l))
```

```python
import functools

import jax
import jax.numpy as jnp
import numpy as np
from jax import lax
from jax.experimental import pallas as pl
from jax.experimental.pallas import tpu as pltpu

F32 = jnp.float32
BF16 = jnp.bfloat16

EPS = 1e-6
GLA_HEADS = 4
GLA_DK = 128
GLA_DV = 256
GLA_RANK = 16
GLA_TAU = 16.0
GLA_CHUNK = 64
GLA_SCALE = GLA_DK ** -0.5
DSW_GROUPS = ((128, 1), (512, 4), (2048, 16))
DSW_HEADS = 4
DSW_HEAD_DIM = 64
DSW_SCALE = DSW_HEAD_DIM ** -0.5
DSW_GW = DSW_HEADS * DSW_HEAD_DIM
BAND = 128
ROPE_THETA = 10000.0
PAST_LEN = 8192
LANES = 128
VMEM_LIMIT = 56 * 1024 * 1024


def _nn(a, b):
    return jnp.dot(a, b, preferred_element_type=F32)


def _nt(a, b):
    return lax.dot_general(a, b, (((1,), (1,)), ((), ())), preferred_element_type=F32)


def _tn(a, b):
    return lax.dot_general(a, b, (((0,), (0,)), ((), ())), preferred_element_type=F32)


def _split3(x):
    hi = x.astype(BF16)
    r1 = x - hi.astype(F32)
    mid = r1.astype(BF16)
    lo = (r1 - mid.astype(F32)).astype(BF16)
    return hi, mid, lo


def _iota(shape, dim):
    return lax.broadcasted_iota(jnp.int32, shape, dim)


def _rms(x, g):
    return x * lax.rsqrt(jnp.mean(x * x, axis=-1, keepdims=True) + EPS) * g


def _params(n_parallel, n_arbitrary=0):
    sem = ("parallel",) * n_parallel + ("arbitrary",) * n_arbitrary
    return pltpu.CompilerParams(dimension_semantics=sem, vmem_limit_bytes=VMEM_LIMIT)


def _resident(shape):
    nd = len(shape)
    return pl.BlockSpec(shape, lambda *_: (0,) * nd, pipeline_mode=pl.Buffered(1))


def _ada_kernel(c_ref, w_ref, b_ref, o_ref):
    c = c_ref[...]
    a = (c * jax.nn.sigmoid(c)).astype(BF16)
    o_ref[...] = _nn(a, w_ref[...].astype(BF16)) + b_ref[...]


def _ada(c_all, w_ada, b_ada):
    n, d = c_all.shape
    ncol = w_ada.shape[1]
    tn = 1536
    return pl.pallas_call(
        _ada_kernel,
        grid=(ncol // tn,),
        in_specs=[pl.BlockSpec((n, d), lambda j: (0, 0)),
                  pl.BlockSpec((d, tn), lambda j: (0, j)),
                  pl.BlockSpec((1, tn), lambda j: (0, j))],
        out_specs=pl.BlockSpec((n, tn), lambda j: (0, j)),
        out_shape=jax.ShapeDtypeStruct((n, ncol), F32),
        compiler_params=_params(1),
        name="ada_mod",
    )(c_all, w_ada, b_ada.reshape(1, ncol))


def _rope(x, cosf, sins, first_half):
    rot = jnp.where(first_half, pltpu.roll(x, LANES - 32, 1), pltpu.roll(x, 32, 1))
    return x * cosf + rot * sins


def _inproj_kernel(x_ref, sc_ref, sh_ref, g_ref, cos_ref, sin_ref,
                   wa_ref, ba_ref, wg_ref, bg_ref, w2_ref, b2_ref, wb_ref, bb_ref, wc_ref, bc_ref,
                   gq_ref, gk_ref, gv_ref, gr_ref, la_ref, ga_ref, gb_ref, *rest, fold):
    x = x_ref[...]
    h = (_rms(x, g_ref[...]) * (1.0 + sc_ref[...]) + sh_ref[...]).astype(BF16)

    cosf = cos_ref[...]
    sins = sin_ref[...]
    first_half = (_iota(cosf.shape, 1) % DSW_HEAD_DIM) < (DSW_HEAD_DIM // 2)
    tm = x.shape[0]
    per_group = DSW_GW // LANES
    width = 3 * DSW_GW

    def proj(off, g):
        cols = slice(off + g * DSW_GW, off + (g + 1) * DSW_GW)
        full = _nn(h, wb_ref[:, cols]) + bb_ref[:, cols]
        return [full[:, s * LANES:(s + 1) * LANES] for s in range(per_group)]

    for g, (_, dil) in enumerate(DSW_GROUPS):
        qs = [_rope(a, cosf, sins, first_half) * DSW_SCALE for a in proj(0, g)]
        ks = [_rope(a, cosf, sins, first_half) for a in proj(width, g)]
        vs = proj(2 * width, g)
        if fold:
            kvt_ref = rest[3 * len(DSW_GROUPS) + g]
            kvt_ref[0:DSW_GW, :] = jnp.concatenate(ks, axis=1).T
            kvt_ref[DSW_GW:2 * DSW_GW, :] = jnp.concatenate(vs, axis=1).T

        for slab in range(per_group):
            cols = slice(g * DSW_GW + slab * LANES, g * DSW_GW + (slab + 1) * LANES)
            lanes = slice(slab * LANES, (slab + 1) * LANES)
            if not fold:
                rest[0][:, cols] = ks[slab]
                rest[1][:, cols] = vs[slab]
                rest[2][:, cols] = qs[slab]
                continue
            scratch = rest[-1]
            for which, val in enumerate((qs[slab], ks[slab], vs[slab])):
                out_ref = rest[3 * g + which]
                if dil == 1:
                    out_ref[:, lanes] = val.astype(BF16)
                else:
                    scratch[which] = val
                    for r in range(dil):
                        out_ref[r, :, lanes] = scratch[which, pl.ds(r, tm // dil, stride=dil), :].astype(BF16)

    glr = (_nn(h, wg_ref[...]) + bg_ref[...]).astype(BF16)
    z = _nn(glr, w2_ref[...]) + b2_ref[...]
    la_ref[...] = jax.nn.log_sigmoid(z) * (1.0 / GLA_TAU)

    gq_ref[...] = (_nn(h, wa_ref[:, 0:512]) + ba_ref[:, 0:512]).astype(gq_ref.dtype)
    gk_ref[...] = (_nn(h, wa_ref[:, 512:1024]) + ba_ref[:, 512:1024]).astype(gk_ref.dtype)
    gv_ref[...] = (_nn(h, wa_ref[:, 1024:2048]) + ba_ref[:, 1024:2048]).astype(gv_ref.dtype)
    gr_ref[...] = (_nn(h, wa_ref[:, 2048:3072]) + ba_ref[:, 2048:3072]).astype(gr_ref.dtype)

    ga_ref[...] = (_nn(h, wc_ref[:, 0:1024]) + bc_ref[:, 0:1024]).astype(ga_ref.dtype)
    gb_ref[...] = (_nn(h, wc_ref[:, 1024:2048]) + bc_ref[:, 1024:2048]).astype(gb_ref.dtype)


def _inproj_call(x, sc, sh, g, cos_t, sin_t, w, tm, fold):
    nb, t, d = x.shape
    tiles = t // tm
    tok = functools.partial(_flat_tok_spec, tm, tiles)
    mod = lambda a: _flat_mod_spec(a, tm, tiles)
    out_cols = (512, 512, 1024, 1024, 512, 1024, 1024)
    out_dt = (BF16, BF16, BF16, BF16, F32, BF16, BF16)
    out_specs = [tok(n) for n in out_cols]
    out_shape = [jax.ShapeDtypeStruct((nb, t, n), dt) for n, dt in zip(out_cols, out_dt)]
    scratch = []
    if fold:
        for _, dil in DSW_GROUPS:
            for _ in range(3):
                if dil == 1:
                    out_specs.append(tok(DSW_GW))
                    out_shape.append(jax.ShapeDtypeStruct((nb, t, DSW_GW), BF16))
                else:
                    out_specs.append(pl.BlockSpec((None, dil, tm // dil, DSW_GW),
                                                  lambda i: (i // tiles, 0, i % tiles, 0)))
                    out_shape.append(jax.ShapeDtypeStruct((nb, dil, t // dil, DSW_GW), BF16))
        for win, _ in DSW_GROUPS:
            width = min(max(win, tm), t)
            first = (t - width) // tm
            out_specs.append(pl.BlockSpec((None, 2 * DSW_GW, tm),
                                          lambda i, first=first: (i // tiles, 0, jnp.maximum(i % tiles - first, 0))))
            out_shape.append(jax.ShapeDtypeStruct((nb, 2 * DSW_GW, width), F32))
        scratch = [pltpu.VMEM((3, tm, LANES), F32)]
    else:
        for _ in range(3):
            out_specs.append(tok(3 * DSW_GW))
            out_shape.append(jax.ShapeDtypeStruct((nb, t, 3 * DSW_GW), F32))
    weights = (w["wa"], w["ba"], w["wg"], w["bg"], w["w2"], w["b2"], w["wb"], w["bb"], w["wc"], w["bc"])
    table = pl.BlockSpec((tm, LANES), lambda i: (i % tiles, 0))
    return dict(
        name="inproj", steps=nb * tiles, body=functools.partial(_inproj_kernel, fold=fold),
        args=(x, sc, sh, g, cos_t, sin_t, *weights),
        in_specs=[tok(d), mod(sc), mod(sh), _resident((1, d)), table, table] + [_resident(a.shape) for a in weights],
        out_specs=out_specs, out_shape=out_shape, scratch=scratch)


def _cast_kernel(x_ref, o_ref):
    o_ref[...] = x_ref[...].astype(o_ref.dtype)


def _cast_call(w, steps):
    rows, cols = w.shape
    blk = next(b for b in range(16, rows + 1, 16) if rows % b == 0 and b * steps >= rows)
    n_blk = rows // blk
    spec = pl.BlockSpec((blk, cols), lambda i: (jnp.minimum(i, n_blk - 1), 0))
    return dict(name="cast", steps=steps, body=_cast_kernel, args=(w,), in_specs=[spec], out_specs=[spec],
                out_shape=[jax.ShapeDtypeStruct(w.shape, BF16)])


def _gla_local(gq_ref, gk_ref, la_ref, chunk):
    la = la_ref[...]
    tt = la.shape[0]
    r = _iota((tt, tt), 0)
    c = _iota((tt, tt), 1)
    same = (r // chunk) == (c // chunk)
    tri = jnp.where(same & (c <= r), 1.0, 0.0).astype(BF16)
    hi, mid, lo = _split3(la)
    b = _nn(tri, hi) + _nn(tri, mid) + _nn(tri, lo)
    if chunk % 8 == 0:
        bl = jnp.concatenate([jnp.broadcast_to(b[e - 1:e, :], (chunk, b.shape[1]))
                              for e in range(chunk, tt + 1, chunk)], axis=0)
    else:
        ones = jnp.where(same, 1.0, 0.0).astype(BF16)
        bl = _nn(ones, hi) + _nn(ones, mid) + _nn(ones, lo)
    gq = gq_ref[...].astype(F32)
    gk = gk_ref[...].astype(F32)
    qg = (gq * GLA_SCALE * jnp.exp(b)).astype(BF16)
    kd = (gk * jnp.exp(-b)).astype(BF16)
    kl = (gk * jnp.exp(bl - b)).astype(BF16)
    causal = same & (c <= r)
    return qg, kd, kl, jnp.exp(bl), causal


def _gla_finish(o, gr, g):
    return (_rms(o, g) * (gr * jax.nn.sigmoid(gr))).astype(BF16)


def _gla_prompt_body(first_tile, gq_ref, gk_ref, gv_ref, la_ref, gr_ref, g_ref, o_ref, st_ref):
    @pl.when(first_tile)
    def _():
        st_ref[...] = jnp.zeros_like(st_ref)

    qg, kd, kl, dec, causal = _gla_local(gq_ref, gk_ref, la_ref, GLA_CHUNK)
    tt = qg.shape[0]
    chunk_of_row = _iota((tt, GLA_DK), 0) // GLA_CHUNK
    for h in range(GLA_HEADS):
        kc = slice(h * GLA_DK, (h + 1) * GLA_DK)
        vc = slice(h * GLA_DV, (h + 1) * GLA_DV)
        v = gv_ref[:, vc].astype(BF16)
        att = jnp.where(causal, _nt(qg[:, kc], kd[:, kc]), 0.0).astype(BF16)
        intra = _nn(att, v)
        st = st_ref[h]
        n_chunks = tt // GLA_CHUNK
        kl_h = kl[:, kc]
        kl_bd = jnp.concatenate([jnp.where(chunk_of_row == ci, kl_h, jnp.zeros_like(kl_h))
                                 for ci in range(n_chunks)], axis=1)
        upd = _tn(v, kl_bd)
        inter = []
        for ci in range(n_chunks):
            rows = slice(ci * GLA_CHUNK, (ci + 1) * GLA_CHUNK)
            inter.append(_nt(qg[rows, kc], st.astype(BF16)))
            st = dec[ci * GLA_CHUNK:ci * GLA_CHUNK + 1, kc] * st + upd[:, ci * GLA_DK:(ci + 1) * GLA_DK]
        st_ref[h] = st
        o = intra + jnp.concatenate(inter, axis=0)
        o_ref[:, vc] = _gla_finish(o, gr_ref[:, vc].astype(F32), g_ref[...])


def _gla_prompt_call(gq, gk, gv, la, gr, g, tt):
    nb, t, _ = gq.shape
    tiles = t // tt
    tok = lambda n: pl.BlockSpec((None, tt, n), lambda i: (i // tiles, i % tiles, 0))
    def body(*refs):
        _gla_prompt_body(pl.program_id(0) % tiles == 0, *refs)

    return dict(
        name="gla_prompt", steps=nb * tiles, body=body, args=(gq, gk, gv, la, gr, g),
        in_specs=[tok(512), tok(512), tok(1024), tok(512), tok(1024), _resident((1, GLA_DV))],
        out_specs=[tok(1024),
                   pl.BlockSpec((None, GLA_HEADS, GLA_DV, GLA_DK), lambda i: (i // tiles, 0, 0, 0))],
        out_shape=[jax.ShapeDtypeStruct((nb, t, 1024), BF16),
                   jax.ShapeDtypeStruct((nb, GLA_HEADS, GLA_DV, GLA_DK), F32)])


def _run_jobs(*jobs):
    steps = jobs[0]["steps"]
    assert all(j["steps"] == steps for j in jobs)
    n_in = [len(j["in_specs"]) for j in jobs]
    n_out = [len(j["out_specs"]) for j in jobs]
    n_scr = [len(j.get("scratch", ())) for j in jobs]

    def kern(*refs):
        i_pos, o_pos, s_pos = 0, sum(n_in), sum(n_in) + sum(n_out)
        for j, ni, no, ns in zip(jobs, n_in, n_out, n_scr):
            j["body"](*refs[i_pos:i_pos + ni], *refs[o_pos:o_pos + no], *refs[s_pos:s_pos + ns])
            i_pos, o_pos, s_pos = i_pos + ni, o_pos + no, s_pos + ns

    outs = pl.pallas_call(
        kern,
        grid=(steps,),
        in_specs=[s for j in jobs for s in j["in_specs"]],
        out_specs=[s for j in jobs for s in j["out_specs"]],
        out_shape=[s for j in jobs for s in j["out_shape"]],
        scratch_shapes=[s for j in jobs for s in j.get("scratch", ())],
        compiler_params=_params(0, 1),
        name="__".join(j["name"] for j in jobs),
    )(*[a for j in jobs for a in j["args"]])
    split, pos = [], 0
    for no in n_out:
        split.append(list(outs[pos:pos + no]))
        pos += no
    return split


def _gla_sample_kernel(gq_ref, gk_ref, gv_ref, la_ref, gr_ref, g_ref, s_ref, o_ref, so_ref, *, seq):
    qg, kd, kl, dec, causal = _gla_local(gq_ref, gk_ref, la_ref, seq)
    rows_total = qg.shape[0]
    per8 = 8 // seq
    row8 = _iota((8, 1), 0)
    for h in range(GLA_HEADS):
        kc = slice(h * GLA_DK, (h + 1) * GLA_DK)
        vc = slice(h * GLA_DV, (h + 1) * GLA_DV)
        v = gv_ref[:, vc].astype(BF16)
        att = jnp.where(causal, _nt(qg[:, kc], kd[:, kc]), 0.0).astype(BF16)
        intra = _nn(att, v)
        inter = []
        for p in range(rows_total // 8):
            rows = slice(p * 8, (p + 1) * 8)
            d_hi, d_mid, d_lo = _split3(dec[rows, kc])
            inter_p = jnp.zeros((8, GLA_DV), F32)
            for j in range(per8):
                b = p * per8 + j
                r0 = j * seq
                s0 = s_ref[b, h]
                mine = (row8 >= r0) & (row8 < r0 + seq)
                inter_p = jnp.where(mine, _nn(qg[rows, kc], s0.astype(BF16)), inter_p)
                dl = jnp.where(row8 == r0, d_hi, jnp.where(row8 == r0 + 1, d_mid,
                               jnp.where(row8 == r0 + 2, d_lo, jnp.zeros_like(d_lo))))
                e = jnp.where((row8 >= r0) & (row8 < r0 + 3), 1.0, 0.0).astype(BF16)
                dec_b = _tn(dl, jnp.broadcast_to(e, (8, GLA_DV)))
                upd = _tn(jnp.where(mine, kl[rows, kc], jnp.zeros_like(kl[rows, kc])), v[rows])
                so_ref[b, h] = dec_b * s0 + upd
            inter.append(inter_p)
        o = intra + jnp.concatenate(inter, axis=0)
        o_ref[:, vc] = _gla_finish(o, gr_ref[:, vc].astype(F32), g_ref[...])


def _gla_sample_call(gq, gk, gv, la, gr, g, s0, seq, bb):
    n_seq = s0.shape[0]
    rows = bb * seq
    tok = lambda n: pl.BlockSpec((None, rows, n), lambda i: (0, i, 0))
    st = pl.BlockSpec((bb, GLA_HEADS, GLA_DK, GLA_DV), lambda i: (i, 0, 0, 0))
    return dict(
        name="gla_sample", steps=n_seq // bb, body=functools.partial(_gla_sample_kernel, seq=seq),
        args=(gq, gk, gv, la, gr, g, s0),
        in_specs=[tok(512), tok(512), tok(1024), tok(512), tok(1024), _resident((1, GLA_DV)), st],
        out_specs=[tok(1024), st],
        out_shape=[jax.ShapeDtypeStruct((1, n_seq * seq, 1024), BF16),
                   jax.ShapeDtypeStruct(s0.shape, F32)])


def _dsw_prompt_kernel(q_ref, kp_ref, kc_ref, vp_ref, vc_ref, o_ref, lse_ref, *, dil, tile, res):
    qb = q_ref.shape[0]
    first_key = jnp.where(tile == 0, BAND, 0)
    qi = _iota((BAND, 2 * BAND), 0) + BAND
    ki = _iota((BAND, 2 * BAND), 1)
    band = (qi - ki >= 0) & (qi - ki <= BAND)
    lane = _iota((BAND, LANES), 1)
    for s in range(qb // BAND):
        rows = slice(s * BAND, (s + 1) * BAND)
        if s == 0:
            valid = band & (ki >= first_key)
        else:
            valid = band
        if dil == 1:
            tok_rows = rows
        else:
            tok_rows = pl.ds(s * BAND * dil + res, BAND, stride=dil)
        for hp in range(DSW_GW // LANES):
            cols = slice(hp * LANES, (hp + 1) * LANES)
            qp = q_ref[rows, cols]
            if s == 0:
                kcat = jnp.concatenate([kp_ref[:, cols], kc_ref[0:BAND, cols]], axis=0)
                vcat = jnp.concatenate([vp_ref[:, cols], vc_ref[0:BAND, cols]], axis=0)
            else:
                kcat = kc_ref[(s - 1) * BAND:(s + 1) * BAND, cols]
                vcat = vc_ref[(s - 1) * BAND:(s + 1) * BAND, cols]
            outs, lses = [], []
            for hh in range(LANES // DSW_HEAD_DIM):
                in_head = (lane // DSW_HEAD_DIM) == hh
                sc = _nt(jnp.where(in_head, qp, jnp.zeros_like(qp)), kcat)
                sc = jnp.where(valid, sc, -jnp.inf)
                m = jnp.max(sc, axis=-1, keepdims=True)
                e = jnp.exp(sc - m)
                den = jnp.sum(e, axis=-1, keepdims=True)
                outs.append(_nn((e / den).astype(BF16), vcat))
                lses.append(m + jnp.log(den))
            first = lane < DSW_HEAD_DIM
            o_ref[hp, tok_rows, :] = jnp.where(first, outs[0], outs[1])
            lse_ref[hp, tok_rows, :] = jnp.where(first, lses[0], jnp.broadcast_to(lses[1], (BAND, LANES)))


def _dsw_prompt_call(q, k, v, dil):
    nb = q.shape[0]
    seq_len = q.shape[-2]
    t = seq_len * dil
    tq = min(512, seq_len)
    sub = tq // BAND
    n_tiles = seq_len // tq
    bat = lambda i: i // (n_tiles * dil)
    til = lambda i: (i // dil) % n_tiles
    res = lambda i: i % dil
    prev_blk = lambda i: jnp.maximum(til(i) * sub - 1, 0)
    if dil == 1:
        cur = pl.BlockSpec((None, tq, DSW_GW), lambda i: (bat(i), til(i), 0))
        prev = pl.BlockSpec((None, BAND, DSW_GW), lambda i: (bat(i), prev_blk(i), 0))
    else:
        cur = pl.BlockSpec((None, None, tq, DSW_GW), lambda i: (bat(i), res(i), til(i), 0))
        prev = pl.BlockSpec((None, None, BAND, DSW_GW), lambda i: (bat(i), res(i), prev_blk(i), 0))
    n_slab = DSW_GW // LANES
    out = pl.BlockSpec((None, n_slab, tq * dil, LANES), lambda i: (bat(i), 0, til(i), 0))

    def body(*refs):
        i = pl.program_id(0)
        _dsw_prompt_kernel(*refs, dil=dil, tile=til(i), res=res(i))

    return dict(
        name=f"dsw_prompt_d{dil}", steps=nb * n_tiles * dil, body=body, args=(q, k, k, v, v),
        in_specs=[cur, prev, cur, prev, cur],
        out_specs=[out, out],
        out_shape=[jax.ShapeDtypeStruct((nb, n_slab, t, LANES), F32)] * 2)


def _dsw_sample_kernel(q_ref, kn_ref, vn_ref, c_ref, o_ref, lse_ref, co_ref, *, seq, dil):
    per8 = 8 // seq
    win = c_ref.shape[2]
    n_rows = DSW_HEADS * 8
    lane = _iota((8, LANES), 1)
    head_of_lane = _iota((8, DSW_GW), 1) // DSW_HEAD_DIM
    row8 = _iota((8, 1), 0)
    r = _iota((n_rows, 1), 0)
    r_step = r % seq
    r_seq = (r % 8) // seq
    key = _iota((n_rows, win), 1)
    cache_ok = ((key % dil) == (r_step % dil)) & (key >= r_step)
    c128 = _iota((n_rows, LANES), 1)
    new_ok = ((c128 < 8) & ((c128 // seq) == r_seq) & ((c128 % seq) <= r_step)
              & (((r_step - c128 % seq) % dil) == 0))
    pad = jnp.zeros((LANES - 8, DSW_GW), BF16)

    def by_head(x):
        out = x[(DSW_HEADS - 1) * 8:DSW_HEADS * 8]
        for h in range(DSW_HEADS - 2, -1, -1):
            out = jnp.where(head_of_lane == h, x[h * 8:(h + 1) * 8], out)
        return out

    lane_sq = _iota((LANES, LANES), 1)
    p_row = _iota((8, LANES), 0)
    for grp in range(q_ref.shape[0] // 8):
        r8 = slice(grp * 8, (grp + 1) * 8)
        q8 = q_ref[r8, :]
        qrows = jnp.concatenate([jnp.where(head_of_lane == h, q8, 0.0) for h in range(DSW_HEADS)],
                                axis=0).astype(BF16)
        kn8 = kn_ref[r8, :]
        vn8 = vn_ref[r8, :]
        kn_t = jnp.concatenate([kn8.astype(BF16), pad], axis=0)
        vn_t = jnp.concatenate([vn8.astype(BF16), pad], axis=0)
        scn = jnp.where(new_ok, _nt(qrows, kn_t), -jnp.inf)
        m_new = jnp.max(scn, axis=-1, keepdims=True)
        o_p = jnp.zeros((8, DSW_GW), F32)
        l_p = jnp.zeros((8, DSW_GW), F32)
        for j in range(per8):
            b = grp * per8 + j
            kt = c_ref[b, 0:DSW_GW, :].astype(BF16)
            vt = c_ref[b, DSW_GW:2 * DSW_GW, :].astype(BF16)
            sc = jnp.where(cache_ok, _nn(qrows, kt), -jnp.inf)
            m = jnp.maximum(jnp.max(sc, axis=-1, keepdims=True), m_new)
            e = jnp.exp(sc - m)
            en = jnp.exp(scn - m)
            den = jnp.sum(e, axis=-1, keepdims=True) + jnp.sum(en, axis=-1, keepdims=True)
            o = _nt((e / den).astype(BF16), vt) + _nn((en / den).astype(BF16), vn_t)
            lse = jnp.broadcast_to(m + jnp.log(den), (n_rows, DSW_GW))
            mine = (row8 // seq) == j
            o_p = jnp.where(mine, by_head(o), o_p)
            l_p = jnp.where(mine, by_head(lse), l_p)
        o_ref[r8, :] = o_p
        lse_ref[r8, :] = l_p

        hi, mid, lo = _split3(jnp.concatenate([kn8, vn8], axis=1))
        for j in range(per8):
            b = grp * per8 + j
            place = jnp.where(((p_row // seq) == j) & (lane == LANES - seq + p_row % seq), 1.0, 0.0).astype(BF16)
            new_cols = _tn(hi, place) + _tn(mid, place) + _tn(lo, place)
            for blk in range(2 * DSW_GW // LANES):
                rows = slice(blk * LANES, (blk + 1) * LANES)
                rolled = pltpu.roll(c_ref[b, rows, :], win - seq, 1)
                if win > LANES:
                    co_ref[b, rows, 0:win - LANES] = rolled[:, 0:win - LANES]
                co_ref[b, rows, win - LANES:win] = jnp.where(lane_sq < LANES - seq, rolled[:, win - LANES:win],
                                                             new_cols[rows])


def _dsw_sample_call(q32, k32, v32, cache, g, dil, seq, steps=None):
    n_seq, win = cache.shape[0], cache.shape[1]
    per8 = 8 // seq
    if steps is None:
        groups = max(1, min(8, 4 * 512 // win))
    else:
        groups = n_seq // (per8 * steps)
    n_blk = per8 * groups
    view = jnp.transpose(cache, (0, 2, 3, 4, 1)).reshape(n_seq, 2 * DSW_GW, win)
    tok = pl.BlockSpec((None, 8 * groups, DSW_GW), lambda i: (0, i, g))
    tok_out = pl.BlockSpec((None, 8 * groups, DSW_GW), lambda i: (0, i, 0))
    cspec = pl.BlockSpec((n_blk, 2 * DSW_GW, win), lambda i: (i, 0, 0))
    return dict(
        name=f"dsw_sample_d{dil}", steps=n_seq // n_blk,
        body=functools.partial(_dsw_sample_kernel, seq=seq, dil=dil), args=(q32, k32, v32, view),
        in_specs=[tok, tok, tok, cspec],
        out_specs=[tok_out, tok_out, cspec],
        out_shape=[jax.ShapeDtypeStruct((1, n_seq * seq, DSW_GW), F32)] * 2
                  + [jax.ShapeDtypeStruct(view.shape, F32)])


def _dsw_sample_finish(o, lse, new, cache_shape):
    n_seq, win = cache_shape[0], cache_shape[1]
    new = jnp.transpose(new.reshape(n_seq, 2, DSW_HEADS, DSW_HEAD_DIM, win), (0, 4, 1, 2, 3))
    n_slab = DSW_GW // LANES
    slabs = lambda a: jnp.transpose(a.reshape(1, -1, n_slab, LANES), (0, 2, 1, 3))
    return slabs(o), slabs(lse), new


def _merge_kernel(oa_ref, o0_ref, o1_ref, o2_ref, l0_ref, l1_ref, l2_ref, ga_ref, gb_ref, x_ref,
                  g1_ref, sc_ref, sh_ref, n2_ref, wpa_ref, wpb_ref, wo_ref, x1_ref, h2_ref):
    ob = []
    for slab in range(DSW_GW // LANES):
        l0, l1, l2 = l0_ref[slab], l1_ref[slab], l2_ref[slab]
        m = jnp.maximum(jnp.maximum(l0, l1), l2)
        w0, w1, w2 = jnp.exp(l0 - m), jnp.exp(l1 - m), jnp.exp(l2 - m)
        den = w0 + w1 + w2
        ob.append((w0 / den) * o0_ref[slab] + (w1 / den) * o1_ref[slab] + (w2 / den) * o2_ref[slab])
    ob = jnp.concatenate(ob, axis=1).astype(BF16)
    merged = (jax.nn.sigmoid(ga_ref[...].astype(F32)) * _nn(oa_ref[...], wpa_ref[...])
              + jax.nn.sigmoid(gb_ref[...].astype(F32)) * _nn(ob, wpb_ref[...]))
    x1 = x_ref[...] + g1_ref[...] * _nn(merged.astype(BF16), wo_ref[...])
    x1_ref[...] = x1
    h2_ref[...] = (_rms(x1, n2_ref[...]) * (1.0 + sc_ref[...]) + sh_ref[...]).astype(BF16)


def _flat_tok_spec(tm, tiles, n):
    return pl.BlockSpec((None, tm, n), lambda i: (i // tiles, i % tiles, 0))


def _flat_mod_spec(arr, tm, tiles):
    if arr.shape[1] == 1:
        return pl.BlockSpec((None, 1, arr.shape[2]), lambda i: (i // tiles, 0, 0))
    return pl.BlockSpec((None, tm, arr.shape[2]), lambda i: (i // tiles, i % tiles, 0))


def _merge_call(oa, og, lg, ga, gb, x, g1, sc2, sh2, n2, wpa, wpb, wo, tm):
    nb, t, d = x.shape
    tiles = t // tm
    tok = functools.partial(_flat_tok_spec, tm, tiles)
    mod = lambda a: _flat_mod_spec(a, tm, tiles)
    slab = pl.BlockSpec((None, DSW_GW // LANES, tm, LANES), lambda i: (i // tiles, 0, i % tiles, 0))
    return dict(
        name="merge_outproj", steps=nb * tiles, body=_merge_kernel,
        args=(oa, *og, *lg, ga, gb, x, g1, sc2, sh2, n2, wpa, wpb, wo),
        in_specs=[tok(1024)] + [slab] * 6 + [tok(d), tok(d), tok(d), mod(g1), mod(sc2), mod(sh2),
                  _resident((1, d)), _resident(wpa.shape), _resident(wpb.shape), _resident(wo.shape)],
        out_specs=[tok(d), tok(d)],
        out_shape=[jax.ShapeDtypeStruct((nb, t, d), F32), jax.ShapeDtypeStruct((nb, t, d), BF16)])


def _ffn_kernel(h_ref, x_ref, g2_ref, nf_ref, wu_ref, wd_ref, y_ref, *, final_norm, n_split):
    h = h_ref[...]
    d_ff = wd_ref.shape[0]
    step = d_ff // n_split
    acc = None
    for j in range(n_split):
        u1 = _nn(h, wu_ref[:, j * step:(j + 1) * step])
        u2 = _nn(h, wu_ref[:, d_ff + j * step:d_ff + (j + 1) * step])
        a = (u1 * jax.nn.sigmoid(u1) * u2).astype(BF16)
        part = _nn(a, wd_ref[j * step:(j + 1) * step, :])
        acc = part if acc is None else acc + part
    x2 = x_ref[...] + g2_ref[...] * acc
    y_ref[...] = _rms(x2, nf_ref[...]) if final_norm else x2


def _ffn_call(h2, x1, g2, nf, wu, wd, tm, final_norm):
    nb, t, d = x1.shape
    tiles = t // tm
    tok = functools.partial(_flat_tok_spec, tm, tiles)
    return dict(
        name="ffn", steps=nb * tiles,
        body=functools.partial(_ffn_kernel, final_norm=final_norm, n_split=wd.shape[0] // (2 * LANES)),
        args=(h2, x1, g2, nf, wu, wd),
        in_specs=[tok(d), tok(d), _flat_mod_spec(g2, tm, tiles), _resident((1, d)),
                  _resident(wu.shape), _resident(wd.shape)],
        out_specs=[tok(d)],
        out_shape=[jax.ShapeDtypeStruct((nb, t, d), F32)])


def _rope_tables(pos):
    half = DSW_HEAD_DIM // 2
    inv = ROPE_THETA ** (-np.arange(half, dtype=np.float64) / half)
    ang = np.asarray(pos, np.float64)[:, None] * inv[None, :]
    reps = LANES // half
    sign = np.tile(np.concatenate([-np.ones(half), np.ones(half)]), LANES // DSW_HEAD_DIM)
    cosf = np.tile(np.cos(ang), (1, reps))
    sins = np.tile(np.sin(ang), (1, reps)) * sign[None, :]
    return jnp.asarray(cosf, F32), jnp.asarray(sins, F32)


def _layer_weights(w_in, b_in, w_alpha2, b_alpha2):
    bf = lambda a: a.astype(BF16)
    row = lambda a: a.reshape(1, -1)
    o_glr, o_dq, o_ga = 3072, 3088, 5392
    pad_r = LANES - GLA_RANK
    return dict(
        wa=bf(w_in[:, :o_glr]), ba=row(b_in[:o_glr]),
        wg=bf(jnp.pad(w_in[:, o_glr:o_dq], ((0, 0), (0, pad_r)))), bg=row(jnp.pad(b_in[o_glr:o_dq], (0, pad_r))),
        w2=bf(jnp.pad(w_alpha2, ((0, pad_r), (0, 0)))), b2=row(b_alpha2),
        wb=bf(w_in[:, o_dq:o_ga]), bb=row(b_in[o_dq:o_ga]),
        wc=bf(w_in[:, o_ga:]), bc=row(b_in[o_ga:]))


def _kv_unstack(kvt, keep):
    nb, _, width = kvt.shape
    kv = kvt[:, :, width - keep:].reshape(nb, 2, DSW_HEADS, DSW_HEAD_DIM, keep)
    return jnp.transpose(kv, (0, 4, 1, 2, 3))


def kernel(x_prompt, x_sample, state_gla, cache_kv_w128, cache_kv_w512, cache_kv_w2048, c_prompt, c_sample,
           norm1_g, norm2_g, w_ada, b_ada, w_in, b_in, w_alpha2, b_alpha2, gla_norm_g, w_proj_a, w_proj_b,
           w_out, w_up, w_down, normf_g):
    depth = w_ada.shape[0]
    nb, t, d = x_prompt.shape
    n_seq, seq, _ = x_sample.shape
    assert 8 % seq == 0 and seq >= 3, "sample kernels pack whole sequences into 8-row groups"
    past = PAST_LEN
    caches = (cache_kv_w128, cache_kv_w512, cache_kv_w2048)

    cos_p, sin_p = _rope_tables(np.arange(t))
    cos_s, sin_s = _rope_tables(np.tile(past + np.arange(seq), n_seq))

    n_c = nb + n_seq
    pad_c = (-n_c) % 8
    c_all = jnp.pad(jnp.concatenate([c_prompt, c_sample], axis=0), ((0, pad_c), (0, 0)))

    xp = x_prompt
    xs = x_sample.reshape(1, n_seq * seq, d)
    row = lambda a: a.reshape(1, -1)
    sp_l, kvp_l, ss_l, kvs_l = [], [], [], []
    for l in range(depth):
        w = _layer_weights(w_in[l], b_in[l], w_alpha2[l], b_alpha2[l])
        mod = _ada(c_all, w_ada[l], b_ada[l])
        mod_p = [mod[:nb, i * d:(i + 1) * d].reshape(nb, 1, d) for i in range(6)]
        mod_s = [jnp.repeat(mod[nb:nb + n_seq, i * d:(i + 1) * d], seq, axis=0).reshape(1, n_seq * seq, d)
                 for i in range(6)]
        last = l == depth - 1

        sh1_s, sc1_s, g1_s, sh2_s, sc2_s, g2_s = mod_s
        ((gq_s, gk_s, gv_s, gr_s, la_s, ga_s, gb_s, k32, v32, q32),) = _run_jobs(_inproj_call(
            xs, sc1_s, sh1_s, row(norm1_g[l]), cos_s, sin_s, w, tm=min(512, n_seq * seq), fold=False))
        sh1, sc1, g1, sh2, sc2, g2 = mod_p
        inproj_p = _inproj_call(xp, sc1, sh1, row(norm1_g[l]), cos_p, sin_p, w, tm=512, fold=True)
        late = (w_proj_a[l], w_proj_b[l], w_out[l], w_up[l], w_down[l])
        (gq, gk, gv, gr, la, ga, gb, *dsw_p), *cast = _run_jobs(
            inproj_p, *[_cast_call(a, inproj_p["steps"]) for a in late])
        w.update(zip(("wpa", "wpb", "wo", "wu", "wd"), (c[0] for c in cast)))
        qkv, kvt = dsw_p[:3 * len(DSW_GROUPS)], dsw_p[3 * len(DSW_GROUPS):]

        def with_sample_group(host, g):
            steps = host["steps"]
            fits = n_seq % ((8 // seq) * steps) == 0 and n_seq // ((8 // seq) * steps) <= 8
            job = _dsw_sample_call(q32, k32, v32, caches[g][l], g, DSW_GROUPS[g][1], seq, steps if fits else None)
            if fits:
                host_out, job_out = _run_jobs(host, job)
            else:
                (host_out,), (job_out,) = _run_jobs(host), _run_jobs(job)
            return host_out, _dsw_sample_finish(*job_out, caches[g][l].shape)

        sample_dsw = [None] * len(DSW_GROUPS)
        (oa, st), sample_dsw[2] = with_sample_group(
            _gla_prompt_call(gq, gk, gv, la, gr, row(gla_norm_g[l]), tt=256), 2)
        dsw_jobs = [_dsw_prompt_call(*qkv[3 * g:3 * g + 3], dil) for g, (_, dil) in enumerate(DSW_GROUPS)]
        gla_s_args = (gq_s, gk_s, gv_s, la_s, gr_s, row(gla_norm_g[l]), state_gla[l], seq)
        host = next((j for j in dsw_jobs if n_seq % j["steps"] == 0 and (n_seq // j["steps"]) * seq % 16 == 0), None)
        og, lg = [], []
        for job in dsw_jobs:
            if job is host:
                (o_g, l_g), (oa_s, s_new) = _run_jobs(job, _gla_sample_call(*gla_s_args, bb=n_seq // job["steps"]))
            else:
                ((o_g, l_g),) = _run_jobs(job)
            og.append(o_g)
            lg.append(l_g)
        if host is None:
            ((oa_s, s_new),) = _run_jobs(_gla_sample_call(*gla_s_args, bb=8))

        (x1, h2), sample_dsw[0] = with_sample_group(
            _merge_call(oa, og, lg, ga, gb, xp, g1, sc2, sh2, row(norm2_g[l]), w["wpa"], w["wpb"], w["wo"], tm=512), 0)
        (xp,), sample_dsw[1] = with_sample_group(
            _ffn_call(h2, x1, g2, row(normf_g), w["wu"], w["wd"], tm=512, final_norm=last), 1)
        og_s, lg_s, new_kv = zip(*sample_dsw)

        ((x1, h2),) = _run_jobs(_merge_call(oa_s, og_s, lg_s, ga_s, gb_s, xs, g1_s, sc2_s, sh2_s, row(norm2_g[l]),
                                            w["wpa"], w["wpb"], w["wo"], tm=min(512, n_seq * seq)))
        ((xs,),) = _run_jobs(_ffn_call(h2, x1, g2_s, row(normf_g), w["wu"], w["wd"], tm=min(512, n_seq * seq), final_norm=last))

        sp_l.append(jnp.swapaxes(st, 2, 3))
        kvp_l.append(tuple(_kv_unstack(kvt[g], min(win, t)) for g, (win, _) in enumerate(DSW_GROUPS)))
        ss_l.append(s_new)
        kvs_l.append(new_kv)

    y_prompt = xp
    y_sample = xs.reshape(n_seq, seq, d)
    stack = lambda items: jnp.stack(list(items))
    return (y_prompt, y_sample, stack(sp_l),
            stack(kv[0] for kv in kvp_l), stack(kv[1] for kv in kvp_l), stack(kv[2] for kv in kvp_l),
            stack(ss_l),
            stack(kv[0] for kv in kvs_l), stack(kv[1] for kv in kvs_l), stack(kv[2] for kv in kvs_l))
```

```python
import functools

import jax
import jax.numpy as jnp
import numpy as np
from jax import lax
from jax.experimental import pallas as pl
from jax.experimental.pallas import tpu as pltpu

F32 = jnp.float32
BF16 = jnp.bfloat16

EPS = 1e-6
GLA_HEADS = 4
GLA_DK = 128
GLA_DV = 256
GLA_RANK = 16
GLA_TAU = 16.0
GLA_CHUNK = 64
GLA_SCALE = GLA_DK ** -0.5
DSW_GROUPS = ((128, 1), (512, 4), (2048, 16))
DSW_HEADS = 4
DSW_HEAD_DIM = 64
DSW_SCALE = DSW_HEAD_DIM ** -0.5
DSW_GW = DSW_HEADS * DSW_HEAD_DIM
BAND = 128
ROPE_THETA = 10000.0
PAST_LEN = 8192
LANES = 128
VMEM_LIMIT = 56 * 1024 * 1024


def _nn(a, b):
    return jnp.dot(a, b, preferred_element_type=F32)


def _nt(a, b):
    return lax.dot_general(a, b, (((1,), (1,)), ((), ())), preferred_element_type=F32)


def _tn(a, b):
    return lax.dot_general(a, b, (((0,), (0,)), ((), ())), preferred_element_type=F32)


def _split3(x):
    hi = x.astype(BF16)
    r1 = x - hi.astype(F32)
    mid = r1.astype(BF16)
    lo = (r1 - mid.astype(F32)).astype(BF16)
    return hi, mid, lo


def _iota(shape, dim):
    return lax.broadcasted_iota(jnp.int32, shape, dim)


def _rms(x, g):
    return x * lax.rsqrt(jnp.mean(x * x, axis=-1, keepdims=True) + EPS) * g


def _params(n_parallel, n_arbitrary=0):
    sem = ("parallel",) * n_parallel + ("arbitrary",) * n_arbitrary
    return pltpu.CompilerParams(dimension_semantics=sem, vmem_limit_bytes=VMEM_LIMIT)


def _resident(shape):
    nd = len(shape)
    return pl.BlockSpec(shape, lambda *_: (0,) * nd, pipeline_mode=pl.Buffered(1))


def _ada_kernel(c_ref, w_ref, b_ref, o_ref):
    c = c_ref[...]
    a = (c * jax.nn.sigmoid(c)).astype(BF16)
    o_ref[...] = _nn(a, w_ref[...].astype(BF16)) + b_ref[...]


def _ada(c_all, w_ada, b_ada):
    n, d = c_all.shape
    ncol = w_ada.shape[1]
    tn = 1536
    return pl.pallas_call(
        _ada_kernel,
        grid=(ncol // tn,),
        in_specs=[pl.BlockSpec((n, d), lambda j: (0, 0)),
                  pl.BlockSpec((d, tn), lambda j: (0, j)),
                  pl.BlockSpec((1, tn), lambda j: (0, j))],
        out_specs=pl.BlockSpec((n, tn), lambda j: (0, j)),
        out_shape=jax.ShapeDtypeStruct((n, ncol), F32),
        compiler_params=_params(1),
        name="ada_mod",
    )(c_all, w_ada, b_ada.reshape(1, ncol))


def _rope(x, cosf, sins, first_half):
    rot = jnp.where(first_half, pltpu.roll(x, LANES - 32, 1), pltpu.roll(x, 32, 1))
    return x * cosf + rot * sins


def _inproj_kernel(x_ref, sc_ref, sh_ref, g_ref, cos_ref, sin_ref,
                   wa_ref, ba_ref, wg_ref, bg_ref, w2_ref, b2_ref, wb_ref, bb_ref, wc_ref, bc_ref,
                   gq_ref, gk_ref, gv_ref, gr_ref, la_ref, ga_ref, gb_ref, *rest, fold):
    x = x_ref[...]
    h = (_rms(x, g_ref[...]) * (1.0 + sc_ref[...]) + sh_ref[...]).astype(BF16)

    cosf = cos_ref[...]
    sins = sin_ref[...]
    first_half = (_iota(cosf.shape, 1) % DSW_HEAD_DIM) < (DSW_HEAD_DIM // 2)
    tm = x.shape[0]
    per_group = DSW_GW // LANES
    width = 3 * DSW_GW

    def proj(off, g):
        cols = slice(off + g * DSW_GW, off + (g + 1) * DSW_GW)
        full = _nn(h, wb_ref[:, cols]) + bb_ref[:, cols]
        return [full[:, s * LANES:(s + 1) * LANES] for s in range(per_group)]

    for g, (_, dil) in enumerate(DSW_GROUPS):
        qs = [_rope(a, cosf, sins, first_half) * DSW_SCALE for a in proj(0, g)]
        ks = [_rope(a, cosf, sins, first_half) for a in proj(width, g)]
        vs = proj(2 * width, g)
        if fold:
            kvt_ref = rest[3 * len(DSW_GROUPS) + g]
            kvt_ref[0:DSW_GW, :] = jnp.concatenate(ks, axis=1).T
            kvt_ref[DSW_GW:2 * DSW_GW, :] = jnp.concatenate(vs, axis=1).T

        for slab in range(per_group):
            cols = slice(g * DSW_GW + slab * LANES, g * DSW_GW + (slab + 1) * LANES)
            lanes = slice(slab * LANES, (slab + 1) * LANES)
            if not fold:
                rest[0][:, cols] = ks[slab]
                rest[1][:, cols] = vs[slab]
                rest[2][:, cols] = qs[slab]
                continue
            scratch = rest[-1]
            for which, val in enumerate((qs[slab], ks[slab], vs[slab])):
                out_ref = rest[3 * g + which]
                if dil == 1:
                    out_ref[:, lanes] = val.astype(BF16)
                else:
                    scratch[which] = val
                    for r in range(dil):
                        out_ref[r, :, lanes] = scratch[which, pl.ds(r, tm // dil, stride=dil), :].astype(BF16)

    glr = (_nn(h, wg_ref[...]) + bg_ref[...]).astype(BF16)
    z = _nn(glr, w2_ref[...]) + b2_ref[...]
    la_ref[...] = jax.nn.log_sigmoid(z) * (1.0 / GLA_TAU)

    gq_ref[...] = (_nn(h, wa_ref[:, 0:512]) + ba_ref[:, 0:512]).astype(gq_ref.dtype)
    gk_ref[...] = (_nn(h, wa_ref[:, 512:1024]) + ba_ref[:, 512:1024]).astype(gk_ref.dtype)
    gv_ref[...] = (_nn(h, wa_ref[:, 1024:2048]) + ba_ref[:, 1024:2048]).astype(gv_ref.dtype)
    gr_ref[...] = (_nn(h, wa_ref[:, 2048:3072]) + ba_ref[:, 2048:3072]).astype(gr_ref.dtype)

    ga_ref[...] = (_nn(h, wc_ref[:, 0:1024]) + bc_ref[:, 0:1024]).astype(ga_ref.dtype)
    gb_ref[...] = (_nn(h, wc_ref[:, 1024:2048]) + bc_ref[:, 1024:2048]).astype(gb_ref.dtype)


def _inproj_call(x, sc, sh, g, cos_t, sin_t, w, tm, fold):
    nb, t, d = x.shape
    tiles = t // tm
    tok = functools.partial(_flat_tok_spec, tm, tiles)
    mod = lambda a: _flat_mod_spec(a, tm, tiles, d)
    out_cols = (512, 512, 1024, 1024, 512, 1024, 1024)
    out_dt = (BF16, BF16, BF16, BF16, F32, BF16, BF16)
    out_specs = [tok(n) for n in out_cols]
    out_shape = [jax.ShapeDtypeStruct((nb, t, n), dt) for n, dt in zip(out_cols, out_dt)]
    scratch = []
    if fold:
        for _, dil in DSW_GROUPS:
            for _ in range(3):
                if dil == 1:
                    out_specs.append(tok(DSW_GW))
                    out_shape.append(jax.ShapeDtypeStruct((nb, t, DSW_GW), BF16))
                else:
                    out_specs.append(pl.BlockSpec((None, dil, tm // dil, DSW_GW),
                                                  lambda i: (i // tiles, 0, i % tiles, 0)))
                    out_shape.append(jax.ShapeDtypeStruct((nb, dil, t // dil, DSW_GW), BF16))
        for win, _ in DSW_GROUPS:
            width = min(max(win, tm), t)
            first = (t - width) // tm
            out_specs.append(pl.BlockSpec((None, 2 * DSW_GW, tm),
                                          lambda i, first=first: (i // tiles, 0, jnp.maximum(i % tiles - first, 0))))
            out_shape.append(jax.ShapeDtypeStruct((nb, 2 * DSW_GW, width), F32))
        scratch = [pltpu.VMEM((3, tm, LANES), F32)]
    else:
        for _ in range(3):
            out_specs.append(tok(3 * DSW_GW))
            out_shape.append(jax.ShapeDtypeStruct((nb, t, 3 * DSW_GW), F32))
    weights = (w["wa"], w["ba"], w["wg"], w["bg"], w["w2"], w["b2"], w["wb"], w["bb"], w["wc"], w["bc"])
    table = pl.BlockSpec((tm, LANES), lambda i: (i % tiles, 0))
    return dict(
        name="inproj", steps=nb * tiles, body=functools.partial(_inproj_kernel, fold=fold),
        args=(x, sc[0], sh[0], g, cos_t, sin_t, *weights),
        in_specs=[tok(d), mod(sc), mod(sh), _resident((1, d)), table, table] + [_resident(a.shape) for a in weights],
        out_specs=out_specs, out_shape=out_shape, scratch=scratch)


def _cast_kernel(x_ref, o_ref):
    o_ref[...] = x_ref[...].astype(o_ref.dtype)


def _cast_call(w, steps):
    rows, cols = w.shape
    blk = next(b for b in range(16, rows + 1, 16) if rows % b == 0 and b * steps >= rows)
    n_blk = rows // blk
    spec = pl.BlockSpec((blk, cols), lambda i: (jnp.minimum(i, n_blk - 1), 0))
    return dict(name="cast", steps=steps, body=_cast_kernel, args=(w,), in_specs=[spec], out_specs=[spec],
                out_shape=[jax.ShapeDtypeStruct(w.shape, BF16)])


def _gla_local(gq_ref, gk_ref, la_ref, chunk):
    la = la_ref[...]
    tt = la.shape[0]
    r = _iota((tt, tt), 0)
    c = _iota((tt, tt), 1)
    same = (r // chunk) == (c // chunk)
    tri = jnp.where(same & (c <= r), 1.0, 0.0).astype(BF16)
    hi, mid, lo = _split3(la)
    b = _nn(tri, hi) + _nn(tri, mid) + _nn(tri, lo)
    if chunk % 8 == 0:
        bl = jnp.concatenate([jnp.broadcast_to(b[e - 1:e, :], (chunk, b.shape[1]))
                              for e in range(chunk, tt + 1, chunk)], axis=0)
    else:
        ones = jnp.where(same, 1.0, 0.0).astype(BF16)
        bl = _nn(ones, hi) + _nn(ones, mid) + _nn(ones, lo)
    gq = gq_ref[...].astype(F32)
    gk = gk_ref[...].astype(F32)
    qg = (gq * GLA_SCALE * jnp.exp(b)).astype(BF16)
    kd = (gk * jnp.exp(-b)).astype(BF16)
    kl = (gk * jnp.exp(bl - b)).astype(BF16)
    causal = same & (c <= r)
    return qg, kd, kl, jnp.exp(bl), causal


def _gla_finish(o, gr, g):
    return (_rms(o, g) * (gr * jax.nn.sigmoid(gr))).astype(BF16)


def _gla_prompt_body(first_tile, gq_ref, gk_ref, gv_ref, la_ref, gr_ref, g_ref, o_ref, st_ref):
    @pl.when(first_tile)
    def _():
        st_ref[...] = jnp.zeros_like(st_ref)

    qg, kd, kl, dec, causal = _gla_local(gq_ref, gk_ref, la_ref, GLA_CHUNK)
    tt = qg.shape[0]
    chunk_of_row = _iota((tt, GLA_DK), 0) // GLA_CHUNK
    for h in range(GLA_HEADS):
        kc = slice(h * GLA_DK, (h + 1) * GLA_DK)
        vc = slice(h * GLA_DV, (h + 1) * GLA_DV)
        v = gv_ref[:, vc].astype(BF16)
        att = jnp.where(causal, _nt(qg[:, kc], kd[:, kc]), 0.0).astype(BF16)
        intra = _nn(att, v)
        st = st_ref[h]
        n_chunks = tt // GLA_CHUNK
        kl_h = kl[:, kc]
        kl_bd = jnp.concatenate([jnp.where(chunk_of_row == ci, kl_h, jnp.zeros_like(kl_h))
                                 for ci in range(n_chunks)], axis=1)
        upd = _tn(v, kl_bd)
        inter = []
        for ci in range(n_chunks):
            rows = slice(ci * GLA_CHUNK, (ci + 1) * GLA_CHUNK)
            inter.append(_nt(qg[rows, kc], st.astype(BF16)))
            st = dec[ci * GLA_CHUNK:ci * GLA_CHUNK + 1, kc] * st + upd[:, ci * GLA_DK:(ci + 1) * GLA_DK]
        st_ref[h] = st
        o = intra + jnp.concatenate(inter, axis=0)
        o_ref[:, vc] = _gla_finish(o, gr_ref[:, vc].astype(F32), g_ref[...])


def _gla_prompt_call(gq, gk, gv, la, gr, g, tt):
    nb, t, _ = gq.shape
    tiles = t // tt
    tok = lambda n: pl.BlockSpec((None, tt, n), lambda i: (i // tiles, i % tiles, 0))
    def body(*refs):
        _gla_prompt_body(pl.program_id(0) % tiles == 0, *refs)

    return dict(
        name="gla_prompt", steps=nb * tiles, body=body, args=(gq, gk, gv, la, gr, g),
        in_specs=[tok(512), tok(512), tok(1024), tok(512), tok(1024), _resident((1, GLA_DV))],
        out_specs=[tok(1024),
                   pl.BlockSpec((None, GLA_HEADS, GLA_DV, GLA_DK), lambda i: (i // tiles, 0, 0, 0))],
        out_shape=[jax.ShapeDtypeStruct((nb, t, 1024), BF16),
                   jax.ShapeDtypeStruct((nb, GLA_HEADS, GLA_DV, GLA_DK), F32)])


def _run_jobs(*jobs):
    steps = jobs[0]["steps"]
    assert all(j["steps"] == steps for j in jobs)
    n_in = [len(j["in_specs"]) for j in jobs]
    n_out = [len(j["out_specs"]) for j in jobs]
    n_scr = [len(j.get("scratch", ())) for j in jobs]

    def kern(*refs):
        i_pos, o_pos, s_pos = 0, sum(n_in), sum(n_in) + sum(n_out)
        for j, ni, no, ns in zip(jobs, n_in, n_out, n_scr):
            j["body"](*refs[i_pos:i_pos + ni], *refs[o_pos:o_pos + no], *refs[s_pos:s_pos + ns])
            i_pos, o_pos, s_pos = i_pos + ni, o_pos + no, s_pos + ns

    outs = pl.pallas_call(
        kern,
        grid=(steps,),
        in_specs=[s for j in jobs for s in j["in_specs"]],
        out_specs=[s for j in jobs for s in j["out_specs"]],
        out_shape=[s for j in jobs for s in j["out_shape"]],
        scratch_shapes=[s for j in jobs for s in j.get("scratch", ())],
        compiler_params=_params(0, 1),
        name="__".join(j["name"] for j in jobs),
    )(*[a for j in jobs for a in j["args"]])
    split, pos = [], 0
    for no in n_out:
        split.append(list(outs[pos:pos + no]))
        pos += no
    return split


def _gla_sample_kernel(gq_ref, gk_ref, gv_ref, la_ref, gr_ref, g_ref, s_ref, o_ref, so_ref, *, seq):
    qg, kd, kl, dec, causal = _gla_local(gq_ref, gk_ref, la_ref, seq)
    rows_total = qg.shape[0]
    per8 = 8 // seq
    row8 = _iota((8, 1), 0)
    for h in range(GLA_HEADS):
        kc = slice(h * GLA_DK, (h + 1) * GLA_DK)
        vc = slice(h * GLA_DV, (h + 1) * GLA_DV)
        v = gv_ref[:, vc].astype(BF16)
        att = jnp.where(causal, _nt(qg[:, kc], kd[:, kc]), 0.0).astype(BF16)
        intra = _nn(att, v)
        inter = []
        for p in range(rows_total // 8):
            rows = slice(p * 8, (p + 1) * 8)
            d_hi, d_mid, d_lo = _split3(dec[rows, kc])
            inter_p = jnp.zeros((8, GLA_DV), F32)
            for j in range(per8):
                b = p * per8 + j
                r0 = j * seq
                s0 = s_ref[b, h]
                mine = (row8 >= r0) & (row8 < r0 + seq)
                inter_p = jnp.where(mine, _nn(qg[rows, kc], s0.astype(BF16)), inter_p)
                dl = jnp.where(row8 == r0, d_hi, jnp.where(row8 == r0 + 1, d_mid,
                               jnp.where(row8 == r0 + 2, d_lo, jnp.zeros_like(d_lo))))
                e = jnp.where((row8 >= r0) & (row8 < r0 + 3), 1.0, 0.0).astype(BF16)
                dec_b = _tn(dl, jnp.broadcast_to(e, (8, GLA_DV)))
                upd = _tn(jnp.where(mine, kl[rows, kc], jnp.zeros_like(kl[rows, kc])), v[rows])
                so_ref[b, h] = dec_b * s0 + upd
            inter.append(inter_p)
        o = intra + jnp.concatenate(inter, axis=0)
        o_ref[:, vc] = _gla_finish(o, gr_ref[:, vc].astype(F32), g_ref[...])


def _gla_sample_call(gq, gk, gv, la, gr, g, s0, seq, bb):
    n_seq = s0.shape[0]
    rows = bb * seq
    tok = lambda n: pl.BlockSpec((None, rows, n), lambda i: (0, i, 0))
    st = pl.BlockSpec((bb, GLA_HEADS, GLA_DK, GLA_DV), lambda i: (i, 0, 0, 0))
    return dict(
        name="gla_sample", steps=n_seq // bb, body=functools.partial(_gla_sample_kernel, seq=seq),
        args=(gq, gk, gv, la, gr, g, s0),
        in_specs=[tok(512), tok(512), tok(1024), tok(512), tok(1024), _resident((1, GLA_DV)), st],
        out_specs=[tok(1024), st],
        out_shape=[jax.ShapeDtypeStruct((1, n_seq * seq, 1024), BF16),
                   jax.ShapeDtypeStruct(s0.shape, F32)])


def _dsw_prompt_kernel(q_ref, kp_ref, kc_ref, vp_ref, vc_ref, o_ref, lse_ref, *, dil, tile, res):
    qb = q_ref.shape[0]
    first_key = jnp.where(tile == 0, BAND, 0)
    qi = _iota((BAND, 2 * BAND), 0) + BAND
    ki = _iota((BAND, 2 * BAND), 1)
    band = (qi - ki >= 0) & (qi - ki <= BAND)
    lane = _iota((BAND, LANES), 1)
    for s in range(qb // BAND):
        rows = slice(s * BAND, (s + 1) * BAND)
        if s == 0:
            valid = band & (ki >= first_key)
        else:
            valid = band
        if dil == 1:
            tok_rows = rows
        else:
            tok_rows = pl.ds(s * BAND * dil + res, BAND, stride=dil)
        for hp in range(DSW_GW // LANES):
            cols = slice(hp * LANES, (hp + 1) * LANES)
            qp = q_ref[rows, cols]
            if s == 0:
                kcat = jnp.concatenate([kp_ref[:, cols], kc_ref[0:BAND, cols]], axis=0)
                vcat = jnp.concatenate([vp_ref[:, cols], vc_ref[0:BAND, cols]], axis=0)
            else:
                kcat = kc_ref[(s - 1) * BAND:(s + 1) * BAND, cols]
                vcat = vc_ref[(s - 1) * BAND:(s + 1) * BAND, cols]
            outs, lses = [], []
            for hh in range(LANES // DSW_HEAD_DIM):
                in_head = (lane // DSW_HEAD_DIM) == hh
                sc = _nt(jnp.where(in_head, qp, jnp.zeros_like(qp)), kcat)
                sc = jnp.where(valid, sc, -jnp.inf)
                m = jnp.max(sc, axis=-1, keepdims=True)
                e = jnp.exp(sc - m)
                den = jnp.sum(e, axis=-1, keepdims=True)
                outs.append(_nn((e / den).astype(BF16), vcat))
                lses.append(m + jnp.log(den))
            first = lane < DSW_HEAD_DIM
            o_ref[hp, tok_rows, :] = jnp.where(first, outs[0], outs[1])
            lse_ref[hp, tok_rows, :] = jnp.where(first, lses[0], jnp.broadcast_to(lses[1], (BAND, LANES)))


def _dsw_prompt_call(q, k, v, dil):
    nb = q.shape[0]
    seq_len = q.shape[-2]
    t = seq_len * dil
    tq = min(512, seq_len)
    sub = tq // BAND
    n_tiles = seq_len // tq
    bat = lambda i: i // (n_tiles * dil)
    til = lambda i: (i // dil) % n_tiles
    res = lambda i: i % dil
    prev_blk = lambda i: jnp.maximum(til(i) * sub - 1, 0)
    if dil == 1:
        cur = pl.BlockSpec((None, tq, DSW_GW), lambda i: (bat(i), til(i), 0))
        prev = pl.BlockSpec((None, BAND, DSW_GW), lambda i: (bat(i), prev_blk(i), 0))
    else:
        cur = pl.BlockSpec((None, None, tq, DSW_GW), lambda i: (bat(i), res(i), til(i), 0))
        prev = pl.BlockSpec((None, None, BAND, DSW_GW), lambda i: (bat(i), res(i), prev_blk(i), 0))
    n_slab = DSW_GW // LANES
    out = pl.BlockSpec((None, n_slab, tq * dil, LANES), lambda i: (bat(i), 0, til(i), 0))

    def body(*refs):
        i = pl.program_id(0)
        _dsw_prompt_kernel(*refs, dil=dil, tile=til(i), res=res(i))

    return dict(
        name=f"dsw_prompt_d{dil}", steps=nb * n_tiles * dil, body=body, args=(q, k, k, v, v),
        in_specs=[cur, prev, cur, prev, cur],
        out_specs=[out, out],
        out_shape=[jax.ShapeDtypeStruct((nb, n_slab, t, LANES), F32)] * 2)


def _dsw_sample_kernel(q_ref, kn_ref, vn_ref, c_ref, o_ref, lse_ref, co_ref, *, seq, dil):
    per8 = 8 // seq
    win = c_ref.shape[2]
    n_rows = DSW_HEADS * 8
    lane = _iota((8, LANES), 1)
    head_of_lane = _iota((8, DSW_GW), 1) // DSW_HEAD_DIM
    row8 = _iota((8, 1), 0)
    r = _iota((n_rows, 1), 0)
    r_step = r % seq
    r_seq = (r % 8) // seq
    key = _iota((n_rows, win), 1)
    cache_ok = ((key % dil) == (r_step % dil)) & (key >= r_step)
    c128 = _iota((n_rows, LANES), 1)
    new_ok = ((c128 < 8) & ((c128 // seq) == r_seq) & ((c128 % seq) <= r_step)
              & (((r_step - c128 % seq) % dil) == 0))
    pad = jnp.zeros((LANES - 8, DSW_GW), BF16)

    def by_head(x):
        out = x[(DSW_HEADS - 1) * 8:DSW_HEADS * 8]
        for h in range(DSW_HEADS - 2, -1, -1):
            out = jnp.where(head_of_lane == h, x[h * 8:(h + 1) * 8], out)
        return out

    lane_sq = _iota((LANES, LANES), 1)
    p_row = _iota((8, LANES), 0)
    for grp in range(q_ref.shape[0] // 8):
        r8 = slice(grp * 8, (grp + 1) * 8)
        q8 = q_ref[r8, :]
        qrows = jnp.concatenate([jnp.where(head_of_lane == h, q8, 0.0) for h in range(DSW_HEADS)],
                                axis=0).astype(BF16)
        kn8 = kn_ref[r8, :]
        vn8 = vn_ref[r8, :]
        kn_t = jnp.concatenate([kn8.astype(BF16), pad], axis=0)
        vn_t = jnp.concatenate([vn8.astype(BF16), pad], axis=0)
        scn = jnp.where(new_ok, _nt(qrows, kn_t), -jnp.inf)
        m_new = jnp.max(scn, axis=-1, keepdims=True)
        o_p = jnp.zeros((8, DSW_GW), F32)
        l_p = jnp.zeros((8, DSW_GW), F32)
        for j in range(per8):
            b = grp * per8 + j
            kt = c_ref[b, 0:DSW_GW, :].astype(BF16)
            vt = c_ref[b, DSW_GW:2 * DSW_GW, :].astype(BF16)
            sc = jnp.where(cache_ok, _nn(qrows, kt), -jnp.inf)
            m = jnp.maximum(jnp.max(sc, axis=-1, keepdims=True), m_new)
            e = jnp.exp(sc - m)
            en = jnp.exp(scn - m)
            den = jnp.sum(e, axis=-1, keepdims=True) + jnp.sum(en, axis=-1, keepdims=True)
            o = _nt((e / den).astype(BF16), vt) + _nn((en / den).astype(BF16), vn_t)
            lse = jnp.broadcast_to(m + jnp.log(den), (n_rows, DSW_GW))
            mine = (row8 // seq) == j
            o_p = jnp.where(mine, by_head(o), o_p)
            l_p = jnp.where(mine, by_head(lse), l_p)
        o_ref[r8, :] = o_p
        lse_ref[r8, :] = l_p

        hi, mid, lo = _split3(jnp.concatenate([kn8, vn8], axis=1))
        for j in range(per8):
            b = grp * per8 + j
            place = jnp.where(((p_row // seq) == j) & (lane == LANES - seq + p_row % seq), 1.0, 0.0).astype(BF16)
            new_cols = _tn(hi, place) + _tn(mid, place) + _tn(lo, place)
            for blk in range(2 * DSW_GW // LANES):
                rows = slice(blk * LANES, (blk + 1) * LANES)
                rolled = pltpu.roll(c_ref[b, rows, :], win - seq, 1)
                if win > LANES:
                    co_ref[b, rows, 0:win - LANES] = rolled[:, 0:win - LANES]
                co_ref[b, rows, win - LANES:win] = jnp.where(lane_sq < LANES - seq, rolled[:, win - LANES:win],
                                                             new_cols[rows])


def _dsw_sample_call(q32, k32, v32, cache, g, dil, seq, steps=None):
    n_seq, win = cache.shape[0], cache.shape[1]
    per8 = 8 // seq
    if steps is None:
        groups = max(1, min(8, 4 * 512 // win))
    else:
        groups = n_seq // (per8 * steps)
    n_blk = per8 * groups
    view = jnp.transpose(cache, (0, 2, 3, 4, 1)).reshape(n_seq, 2 * DSW_GW, win)
    tok = pl.BlockSpec((None, 8 * groups, DSW_GW), lambda i: (0, i, g))
    tok_out = pl.BlockSpec((None, 8 * groups, DSW_GW), lambda i: (0, i, 0))
    cspec = pl.BlockSpec((n_blk, 2 * DSW_GW, win), lambda i: (i, 0, 0))
    return dict(
        name=f"dsw_sample_d{dil}", steps=n_seq // n_blk,
        body=functools.partial(_dsw_sample_kernel, seq=seq, dil=dil), args=(q32, k32, v32, view),
        in_specs=[tok, tok, tok, cspec],
        out_specs=[tok_out, tok_out, cspec],
        out_shape=[jax.ShapeDtypeStruct((1, n_seq * seq, DSW_GW), F32)] * 2
                  + [jax.ShapeDtypeStruct(view.shape, F32)])


def _dsw_sample_finish(o, lse, new, cache_shape):
    n_seq, win = cache_shape[0], cache_shape[1]
    new = jnp.transpose(new.reshape(n_seq, 2, DSW_HEADS, DSW_HEAD_DIM, win), (0, 4, 1, 2, 3))
    n_slab = DSW_GW // LANES
    slabs = lambda a: jnp.transpose(a.reshape(1, -1, n_slab, LANES), (0, 2, 1, 3))
    return slabs(o), slabs(lse), new


def _merge_kernel(oa_ref, o0_ref, o1_ref, o2_ref, l0_ref, l1_ref, l2_ref, ga_ref, gb_ref, x_ref,
                  g1_ref, sc_ref, sh_ref, n2_ref, wpa_ref, wpb_ref, wo_ref, x1_ref, h2_ref):
    ob = []
    for slab in range(DSW_GW // LANES):
        l0, l1, l2 = l0_ref[slab], l1_ref[slab], l2_ref[slab]
        m = jnp.maximum(jnp.maximum(l0, l1), l2)
        w0, w1, w2 = jnp.exp(l0 - m), jnp.exp(l1 - m), jnp.exp(l2 - m)
        den = w0 + w1 + w2
        ob.append((w0 / den) * o0_ref[slab] + (w1 / den) * o1_ref[slab] + (w2 / den) * o2_ref[slab])
    ob = jnp.concatenate(ob, axis=1).astype(BF16)
    merged = (jax.nn.sigmoid(ga_ref[...].astype(F32)) * _nn(oa_ref[...], wpa_ref[...])
              + jax.nn.sigmoid(gb_ref[...].astype(F32)) * _nn(ob, wpb_ref[...]))
    x1 = x_ref[...] + g1_ref[...] * _nn(merged.astype(BF16), wo_ref[...])
    x1_ref[...] = x1
    h2_ref[...] = (_rms(x1, n2_ref[...]) * (1.0 + sc_ref[...]) + sh_ref[...]).astype(BF16)


def _flat_tok_spec(tm, tiles, n):
    return pl.BlockSpec((None, tm, n), lambda i: (i // tiles, i % tiles, 0))


def _flat_mod_spec(m, tm, tiles, d):
    arr, k = m
    if arr.ndim == 4:
        return pl.BlockSpec((None, None, 1, d), lambda i: (i // tiles, k, 0, 0))
    return pl.BlockSpec((tm, d), lambda i: (i % tiles, k))


def _merge_call(oa, og, lg, ga, gb, x, g1, sc2, sh2, n2, wpa, wpb, wo, tm):
    nb, t, d = x.shape
    tiles = t // tm
    tok = functools.partial(_flat_tok_spec, tm, tiles)
    mod = lambda a: _flat_mod_spec(a, tm, tiles, d)
    slab = pl.BlockSpec((None, DSW_GW // LANES, tm, LANES), lambda i: (i // tiles, 0, i % tiles, 0))
    return dict(
        name="merge_outproj", steps=nb * tiles, body=_merge_kernel,
        args=(oa, *og, *lg, ga, gb, x, g1[0], sc2[0], sh2[0], n2, wpa, wpb, wo),
        in_specs=[tok(1024)] + [slab] * 6 + [tok(d), tok(d), tok(d), mod(g1), mod(sc2), mod(sh2),
                  _resident((1, d)), _resident(wpa.shape), _resident(wpb.shape), _resident(wo.shape)],
        out_specs=[tok(d), tok(d)],
        out_shape=[jax.ShapeDtypeStruct((nb, t, d), F32), jax.ShapeDtypeStruct((nb, t, d), BF16)])


def _ffn_kernel(h_ref, x_ref, g2_ref, nf_ref, wu_ref, wd_ref, y_ref, *, final_norm, n_split):
    h = h_ref[...]
    d_ff = wd_ref.shape[0]
    step = d_ff // n_split
    acc = None
    for j in range(n_split):
        u1 = _nn(h, wu_ref[:, j * step:(j + 1) * step])
        u2 = _nn(h, wu_ref[:, d_ff + j * step:d_ff + (j + 1) * step])
        a = (u1 * jax.nn.sigmoid(u1) * u2).astype(BF16)
        part = _nn(a, wd_ref[j * step:(j + 1) * step, :])
        acc = part if acc is None else acc + part
    x2 = x_ref[...] + g2_ref[...] * acc
    y_ref[...] = _rms(x2, nf_ref[...]) if final_norm else x2


def _ffn_call(h2, x1, g2, nf, wu, wd, tm, final_norm):
    nb, t, d = x1.shape
    tiles = t // tm
    tok = functools.partial(_flat_tok_spec, tm, tiles)
    return dict(
        name="ffn", steps=nb * tiles,
        body=functools.partial(_ffn_kernel, final_norm=final_norm, n_split=wd.shape[0] // (2 * LANES)),
        args=(h2, x1, g2[0], nf, wu, wd),
        in_specs=[tok(d), tok(d), _flat_mod_spec(g2, tm, tiles, d), _resident((1, d)),
                  _resident(wu.shape), _resident(wd.shape)],
        out_specs=[tok(d)],
        out_shape=[jax.ShapeDtypeStruct((nb, t, d), F32)])


def _rope_tables(pos):
    half = DSW_HEAD_DIM // 2
    inv = ROPE_THETA ** (-np.arange(half, dtype=np.float64) / half)
    ang = np.asarray(pos, np.float64)[:, None] * inv[None, :]
    reps = LANES // half
    sign = np.tile(np.concatenate([-np.ones(half), np.ones(half)]), LANES // DSW_HEAD_DIM)
    cosf = np.tile(np.cos(ang), (1, reps))
    sins = np.tile(np.sin(ang), (1, reps)) * sign[None, :]
    return jnp.asarray(cosf, F32), jnp.asarray(sins, F32)


def _layer_weights(w_in, b_in, w_alpha2, b_alpha2):
    bf = lambda a: a.astype(BF16)
    row = lambda a: a.reshape(1, -1)
    o_glr, o_dq, o_ga = 3072, 3088, 5392
    pad_r = LANES - GLA_RANK
    return dict(
        wa=bf(w_in[:, :o_glr]), ba=row(b_in[:o_glr]),
        wg=bf(jnp.pad(w_in[:, o_glr:o_dq], ((0, 0), (0, pad_r)))), bg=row(jnp.pad(b_in[o_glr:o_dq], (0, pad_r))),
        w2=bf(jnp.pad(w_alpha2, ((0, pad_r), (0, 0)))), b2=row(b_alpha2),
        wb=bf(w_in[:, o_dq:o_ga]), bb=row(b_in[o_dq:o_ga]),
        wc=bf(w_in[:, o_ga:]), bc=row(b_in[o_ga:]))


def _kv_unstack(kvt, keep):
    nb, _, width = kvt.shape
    kv = kvt[:, :, width - keep:].reshape(nb, 2, DSW_HEADS, DSW_HEAD_DIM, keep)
    return jnp.transpose(kv, (0, 4, 1, 2, 3))


def kernel(x_prompt, x_sample, state_gla, cache_kv_w128, cache_kv_w512, cache_kv_w2048, c_prompt, c_sample,
           norm1_g, norm2_g, w_ada, b_ada, w_in, b_in, w_alpha2, b_alpha2, gla_norm_g, w_proj_a, w_proj_b,
           w_out, w_up, w_down, normf_g):
    depth = w_ada.shape[0]
    nb, t, d = x_prompt.shape
    n_seq, seq, _ = x_sample.shape
    assert 8 % seq == 0 and seq >= 3, "sample kernels pack whole sequences into 8-row groups"
    past = PAST_LEN
    caches = (cache_kv_w128, cache_kv_w512, cache_kv_w2048)

    cos_p, sin_p = _rope_tables(np.arange(t))
    cos_s, sin_s = _rope_tables(np.tile(past + np.arange(seq), n_seq))

    n_tok_s = n_seq * seq
    pad_c = (-(n_tok_s + nb)) % 8
    c_all = jnp.pad(jnp.concatenate([jnp.repeat(c_sample, seq, axis=0), c_prompt], axis=0), ((0, pad_c), (0, 0)))

    xp = x_prompt
    xs = x_sample.reshape(1, n_seq * seq, d)
    row = lambda a: a.reshape(1, -1)
    sp_l, kvp_l, ss_l, kvs_l = [], [], [], []
    for l in range(depth):
        w = _layer_weights(w_in[l], b_in[l], w_alpha2[l], b_alpha2[l])
        mod = _ada(c_all, w_ada[l], b_ada[l])
        mod_rows_p = mod[n_tok_s:n_tok_s + nb].reshape(nb, 6, 1, d)
        mod_p = [(mod_rows_p, k) for k in range(6)]
        mod_s = [(mod, k) for k in range(6)]
        last = l == depth - 1

        sh1_s, sc1_s, g1_s, sh2_s, sc2_s, g2_s = mod_s
        ((gq_s, gk_s, gv_s, gr_s, la_s, ga_s, gb_s, k32, v32, q32),) = _run_jobs(_inproj_call(
            xs, sc1_s, sh1_s, row(norm1_g[l]), cos_s, sin_s, w, tm=min(512, n_seq * seq), fold=False))
        sh1, sc1, g1, sh2, sc2, g2 = mod_p
        inproj_p = _inproj_call(xp, sc1, sh1, row(norm1_g[l]), cos_p, sin_p, w, tm=512, fold=True)
        late = (w_proj_a[l], w_proj_b[l], w_out[l], w_up[l], w_down[l])
        (gq, gk, gv, gr, la, ga, gb, *dsw_p), *cast = _run_jobs(
            inproj_p, *[_cast_call(a, inproj_p["steps"]) for a in late])
        w.update(zip(("wpa", "wpb", "wo", "wu", "wd"), (c[0] for c in cast)))
        qkv, kvt = dsw_p[:3 * len(DSW_GROUPS)], dsw_p[3 * len(DSW_GROUPS):]

        def with_sample_group(host, g):
            steps = host["steps"]
            fits = n_seq % ((8 // seq) * steps) == 0 and n_seq // ((8 // seq) * steps) <= 8
            job = _dsw_sample_call(q32, k32, v32, caches[g][l], g, DSW_GROUPS[g][1], seq, steps if fits else None)
            if fits:
                host_out, job_out = _run_jobs(host, job)
            else:
                (host_out,), (job_out,) = _run_jobs(host), _run_jobs(job)
            return host_out, _dsw_sample_finish(*job_out, caches[g][l].shape)

        sample_dsw = [None] * len(DSW_GROUPS)
        (oa, st), sample_dsw[2] = with_sample_group(
            _gla_prompt_call(gq, gk, gv, la, gr, row(gla_norm_g[l]), tt=256), 2)
        dsw_jobs = [_dsw_prompt_call(*qkv[3 * g:3 * g + 3], dil) for g, (_, dil) in enumerate(DSW_GROUPS)]
        gla_s_args = (gq_s, gk_s, gv_s, la_s, gr_s, row(gla_norm_g[l]), state_gla[l], seq)
        host = next((j for j in dsw_jobs if n_seq % j["steps"] == 0 and (n_seq // j["steps"]) * seq % 16 == 0), None)
        og, lg = [], []
        for job in dsw_jobs:
            if job is host:
                (o_g, l_g), (oa_s, s_new) = _run_jobs(job, _gla_sample_call(*gla_s_args, bb=n_seq // job["steps"]))
            else:
                ((o_g, l_g),) = _run_jobs(job)
            og.append(o_g)
            lg.append(l_g)
        if host is None:
            ((oa_s, s_new),) = _run_jobs(_gla_sample_call(*gla_s_args, bb=8))

        (x1, h2), sample_dsw[0] = with_sample_group(
            _merge_call(oa, og, lg, ga, gb, xp, g1, sc2, sh2, row(norm2_g[l]), w["wpa"], w["wpb"], w["wo"], tm=512), 0)
        (xp,), sample_dsw[1] = with_sample_group(
            _ffn_call(h2, x1, g2, row(normf_g), w["wu"], w["wd"], tm=512, final_norm=last), 1)
        og_s, lg_s, new_kv = zip(*sample_dsw)

        ((x1, h2),) = _run_jobs(_merge_call(oa_s, og_s, lg_s, ga_s, gb_s, xs, g1_s, sc2_s, sh2_s, row(norm2_g[l]),
                                            w["wpa"], w["wpb"], w["wo"], tm=min(512, n_seq * seq)))
        ((xs,),) = _run_jobs(_ffn_call(h2, x1, g2_s, row(normf_g), w["wu"], w["wd"], tm=min(512, n_seq * seq), final_norm=last))

        sp_l.append(jnp.swapaxes(st, 2, 3))
        kvp_l.append(tuple(_kv_unstack(kvt[g], min(win, t)) for g, (win, _) in enumerate(DSW_GROUPS)))
        ss_l.append(s_new)
        kvs_l.append(new_kv)

    y_prompt = xp
    y_sample = xs.reshape(n_seq, seq, d)
    stack = lambda items: jnp.stack(list(items))
    return (y_prompt, y_sample, stack(sp_l),
            stack(kv[0] for kv in kvp_l), stack(kv[1] for kv in kvp_l), stack(kv[2] for kv in kvp_l),
            stack(ss_l),
            stack(kv[0] for kv in kvs_l), stack(kv[1] for kv in kvs_l), stack(kv[2] for kv in kvs_l))
```

```python
import functools

import jax
import jax.numpy as jnp
import numpy as np
from jax import lax
from jax.experimental import pallas as pl
from jax.experimental.pallas import tpu as pltpu

F32 = jnp.float32
BF16 = jnp.bfloat16

EPS = 1e-6
GLA_HEADS = 4
GLA_DK = 128
GLA_DV = 256
GLA_RANK = 16
GLA_TAU = 16.0
GLA_CHUNK = 64
GLA_SCALE = GLA_DK ** -0.5
DSW_GROUPS = ((128, 1), (512, 4), (2048, 16))
DSW_HEADS = 4
DSW_HEAD_DIM = 64
DSW_SCALE = DSW_HEAD_DIM ** -0.5
DSW_GW = DSW_HEADS * DSW_HEAD_DIM
BAND = 128
ROPE_THETA = 10000.0
PAST_LEN = 8192
LANES = 128
VMEM_LIMIT = 56 * 1024 * 1024


def _nn(a, b):
    return jnp.dot(a, b, preferred_element_type=F32)


def _nt(a, b):
    return lax.dot_general(a, b, (((1,), (1,)), ((), ())), preferred_element_type=F32)


def _tn(a, b):
    return lax.dot_general(a, b, (((0,), (0,)), ((), ())), preferred_element_type=F32)


def _split3(x):
    hi = x.astype(BF16)
    r1 = x - hi.astype(F32)
    mid = r1.astype(BF16)
    lo = (r1 - mid.astype(F32)).astype(BF16)
    return hi, mid, lo


def _iota(shape, dim):
    return lax.broadcasted_iota(jnp.int32, shape, dim)


def _rms(x, g):
    return x * lax.rsqrt(jnp.mean(x * x, axis=-1, keepdims=True) + EPS) * g


def _params(n_parallel, n_arbitrary=0):
    sem = ("parallel",) * n_parallel + ("arbitrary",) * n_arbitrary
    return pltpu.CompilerParams(dimension_semantics=sem, vmem_limit_bytes=VMEM_LIMIT)


def _resident(shape):
    nd = len(shape)
    return pl.BlockSpec(shape, lambda *_: (0,) * nd, pipeline_mode=pl.Buffered(1))


def _ada_kernel(c_ref, w_ref, b_ref, o_ref):
    c = c_ref[...]
    a = (c * jax.nn.sigmoid(c)).astype(BF16)
    o_ref[...] = _nn(a, w_ref[...].astype(BF16)) + b_ref[...]


def _ada(c_all, w_ada, b_ada):
    n, d = c_all.shape
    ncol = w_ada.shape[1]
    tn = 1536
    return pl.pallas_call(
        _ada_kernel,
        grid=(ncol // tn,),
        in_specs=[pl.BlockSpec((n, d), lambda j: (0, 0)),
                  pl.BlockSpec((d, tn), lambda j: (0, j)),
                  pl.BlockSpec((1, tn), lambda j: (0, j))],
        out_specs=pl.BlockSpec((n, tn), lambda j: (0, j)),
        out_shape=jax.ShapeDtypeStruct((n, ncol), F32),
        compiler_params=_params(1),
        name="ada_mod",
    )(c_all, w_ada, b_ada.reshape(1, ncol))


def _rope(x, cosf, sins, first_half):
    rot = jnp.where(first_half, pltpu.roll(x, LANES - 32, 1), pltpu.roll(x, 32, 1))
    return x * cosf + rot * sins


def _inproj_kernel(x_ref, sc_ref, sh_ref, g_ref, cos_ref, sin_ref,
                   wa_ref, ba_ref, wg_ref, bg_ref, w2_ref, b2_ref, wb_ref, bb_ref, wc_ref, bc_ref,
                   gq_ref, gk_ref, gv_ref, gr_ref, la_ref, ga_ref, gb_ref, *rest, fold):
    x = x_ref[...]
    h = (_rms(x, g_ref[...]) * (1.0 + sc_ref[...]) + sh_ref[...]).astype(BF16)

    cosf = cos_ref[...]
    sins = sin_ref[...]
    first_half = (_iota(cosf.shape, 1) % DSW_HEAD_DIM) < (DSW_HEAD_DIM // 2)
    tm = x.shape[0]
    per_group = DSW_GW // LANES
    width = 3 * DSW_GW

    def proj(off, g):
        cols = slice(off + g * DSW_GW, off + (g + 1) * DSW_GW)
        full = _nt(h, wb_ref[cols, :]) + bb_ref[:, cols]
        return [full[:, s * LANES:(s + 1) * LANES] for s in range(per_group)]

    for g, (_, dil) in enumerate(DSW_GROUPS):
        qs = [_rope(a, cosf, sins, first_half) * DSW_SCALE for a in proj(0, g)]
        ks = [_rope(a, cosf, sins, first_half) for a in proj(width, g)]
        vs = proj(2 * width, g)
        if fold:
            kvt_ref = rest[3 * len(DSW_GROUPS) + g]
            kvt_ref[0:DSW_GW, :] = jnp.concatenate(ks, axis=1).T
            kvt_ref[DSW_GW:2 * DSW_GW, :] = jnp.concatenate(vs, axis=1).T

        for slab in range(per_group):
            cols = slice(g * DSW_GW + slab * LANES, g * DSW_GW + (slab + 1) * LANES)
            lanes = slice(slab * LANES, (slab + 1) * LANES)
            if not fold:
                rest[0][:, cols] = ks[slab]
                rest[1][:, cols] = vs[slab]
                rest[2][:, cols] = qs[slab]
                continue
            scratch = rest[-1]
            for which, val in enumerate((qs[slab], ks[slab], vs[slab])):
                out_ref = rest[3 * g + which]
                if dil == 1:
                    out_ref[:, lanes] = val.astype(BF16)
                else:
                    scratch[which] = val
                    for r in range(dil):
                        out_ref[r, :, lanes] = scratch[which, pl.ds(r, tm // dil, stride=dil), :].astype(BF16)

    glr = (_nt(h, wg_ref[...]) + bg_ref[...]).astype(BF16)
    z = _nn(glr, w2_ref[...]) + b2_ref[...]
    la_ref[...] = jax.nn.log_sigmoid(z) * (1.0 / GLA_TAU)

    gq_ref[...] = (_nt(h, wa_ref[0:512, :]) + ba_ref[:, 0:512]).astype(gq_ref.dtype)
    gk_ref[...] = (_nt(h, wa_ref[512:1024, :]) + ba_ref[:, 512:1024]).astype(gk_ref.dtype)
    gv_ref[...] = (_nt(h, wa_ref[1024:2048, :]) + ba_ref[:, 1024:2048]).astype(gv_ref.dtype)
    gr_ref[...] = (_nt(h, wa_ref[2048:3072, :]) + ba_ref[:, 2048:3072]).astype(gr_ref.dtype)

    ga_ref[...] = (_nt(h, wc_ref[0:1024, :]) + bc_ref[:, 0:1024]).astype(ga_ref.dtype)
    gb_ref[...] = (_nt(h, wc_ref[1024:2048, :]) + bc_ref[:, 1024:2048]).astype(gb_ref.dtype)


def _inproj_call(x, sc, sh, g, cos_t, sin_t, w, tm, fold):
    nb, t, d = x.shape
    tiles = t // tm
    tok = functools.partial(_flat_tok_spec, tm, tiles)
    mod = lambda a: _flat_mod_spec(a, tm, tiles, d)
    out_cols = (512, 512, 1024, 1024, 512, 1024, 1024)
    out_dt = (BF16, BF16, BF16, BF16, F32, BF16, BF16)
    out_specs = [tok(n) for n in out_cols]
    out_shape = [jax.ShapeDtypeStruct((nb, t, n), dt) for n, dt in zip(out_cols, out_dt)]
    scratch = []
    if fold:
        for _, dil in DSW_GROUPS:
            for _ in range(3):
                if dil == 1:
                    out_specs.append(tok(DSW_GW))
                    out_shape.append(jax.ShapeDtypeStruct((nb, t, DSW_GW), BF16))
                else:
                    out_specs.append(pl.BlockSpec((None, dil, tm // dil, DSW_GW),
                                                  lambda i: (i // tiles, 0, i % tiles, 0)))
                    out_shape.append(jax.ShapeDtypeStruct((nb, dil, t // dil, DSW_GW), BF16))
        for win, _ in DSW_GROUPS:
            width = min(max(win, tm), t)
            first = (t - width) // tm
            out_specs.append(pl.BlockSpec((None, 2 * DSW_GW, tm),
                                          lambda i, first=first: (i // tiles, 0, jnp.maximum(i % tiles - first, 0))))
            out_shape.append(jax.ShapeDtypeStruct((nb, 2 * DSW_GW, width), F32))
        scratch = [pltpu.VMEM((3, tm, LANES), F32)]
    else:
        for _ in range(3):
            out_specs.append(tok(3 * DSW_GW))
            out_shape.append(jax.ShapeDtypeStruct((nb, t, 3 * DSW_GW), F32))
    weights = (w["wa"], w["ba"], w["wg"], w["bg"], w["w2"], w["b2"], w["wb"], w["bb"], w["wc"], w["bc"])
    table = pl.BlockSpec((tm, LANES), lambda i: (i % tiles, 0))
    return dict(
        name="inproj", steps=nb * tiles, body=functools.partial(_inproj_kernel, fold=fold),
        args=(x, sc[0], sh[0], g, cos_t, sin_t, *weights),
        in_specs=[tok(d), mod(sc), mod(sh), _resident((1, d)), table, table] + [_resident(a.shape) for a in weights],
        out_specs=out_specs, out_shape=out_shape, scratch=scratch)


def _cast_kernel(x_ref, o_ref):
    o_ref[...] = x_ref[...].astype(o_ref.dtype)


def _cast_call(w, steps):
    rows, cols = w.shape
    blk = next(b for b in range(16, rows + 1, 16) if rows % b == 0 and b * steps >= rows)
    n_blk = rows // blk
    spec = pl.BlockSpec((blk, cols), lambda i: (jnp.minimum(i, n_blk - 1), 0))
    return dict(name="cast", steps=steps, body=_cast_kernel, args=(w,), in_specs=[spec], out_specs=[spec],
                out_shape=[jax.ShapeDtypeStruct(w.shape, BF16)])


def _gla_local(gq_ref, gk_ref, la_ref, chunk):
    la = la_ref[...]
    tt = la.shape[0]
    r = _iota((tt, tt), 0)
    c = _iota((tt, tt), 1)
    same = (r // chunk) == (c // chunk)
    tri = jnp.where(same & (c <= r), 1.0, 0.0).astype(BF16)
    hi, mid, lo = _split3(la)
    b = _nn(tri, hi) + _nn(tri, mid) + _nn(tri, lo)
    if chunk % 8 == 0:
        bl = jnp.concatenate([jnp.broadcast_to(b[e - 1:e, :], (chunk, b.shape[1]))
                              for e in range(chunk, tt + 1, chunk)], axis=0)
    else:
        ones = jnp.where(same, 1.0, 0.0).astype(BF16)
        bl = _nn(ones, hi) + _nn(ones, mid) + _nn(ones, lo)
    gq = gq_ref[...].astype(F32)
    gk = gk_ref[...].astype(F32)
    qg = (gq * GLA_SCALE * jnp.exp(b)).astype(BF16)
    kd = (gk * jnp.exp(-b)).astype(BF16)
    kl = (gk * jnp.exp(bl - b)).astype(BF16)
    causal = same & (c <= r)
    return qg, kd, kl, jnp.exp(bl), causal


def _gla_finish(o, gr, g):
    return (_rms(o, g) * (gr * jax.nn.sigmoid(gr))).astype(BF16)


def _gla_prompt_body(first_tile, gq_ref, gk_ref, gv_ref, la_ref, gr_ref, g_ref, o_ref, st_ref):
    @pl.when(first_tile)
    def _():
        st_ref[...] = jnp.zeros_like(st_ref)

    qg, kd, kl, dec, causal = _gla_local(gq_ref, gk_ref, la_ref, GLA_CHUNK)
    tt = qg.shape[0]
    chunk_of_row = _iota((tt, GLA_DK), 0) // GLA_CHUNK
    for h in range(GLA_HEADS):
        kc = slice(h * GLA_DK, (h + 1) * GLA_DK)
        vc = slice(h * GLA_DV, (h + 1) * GLA_DV)
        v = gv_ref[:, vc].astype(BF16)
        att = jnp.where(causal, _nt(qg[:, kc], kd[:, kc]), 0.0).astype(BF16)
        intra = _nn(att, v)
        st = st_ref[h]
        n_chunks = tt // GLA_CHUNK
        kl_h = kl[:, kc]
        kl_bd = jnp.concatenate([jnp.where(chunk_of_row == ci, kl_h, jnp.zeros_like(kl_h))
                                 for ci in range(n_chunks)], axis=1)
        upd = _tn(v, kl_bd)
        inter = []
        for ci in range(n_chunks):
            rows = slice(ci * GLA_CHUNK, (ci + 1) * GLA_CHUNK)
            inter.append(_nt(qg[rows, kc], st.astype(BF16)))
            st = dec[ci * GLA_CHUNK:ci * GLA_CHUNK + 1, kc] * st + upd[:, ci * GLA_DK:(ci + 1) * GLA_DK]
        st_ref[h] = st
        o = intra + jnp.concatenate(inter, axis=0)
        o_ref[:, vc] = _gla_finish(o, gr_ref[:, vc].astype(F32), g_ref[...])


def _gla_prompt_call(gq, gk, gv, la, gr, g, tt):
    nb, t, _ = gq.shape
    tiles = t // tt
    tok = lambda n: pl.BlockSpec((None, tt, n), lambda i: (i // tiles, i % tiles, 0))
    def body(*refs):
        _gla_prompt_body(pl.program_id(0) % tiles == 0, *refs)

    return dict(
        name="gla_prompt", steps=nb * tiles, body=body, args=(gq, gk, gv, la, gr, g),
        in_specs=[tok(512), tok(512), tok(1024), tok(512), tok(1024), _resident((1, GLA_DV))],
        out_specs=[tok(1024),
                   pl.BlockSpec((None, GLA_HEADS, GLA_DV, GLA_DK), lambda i: (i // tiles, 0, 0, 0))],
        out_shape=[jax.ShapeDtypeStruct((nb, t, 1024), BF16),
                   jax.ShapeDtypeStruct((nb, GLA_HEADS, GLA_DV, GLA_DK), F32)])


def _run_jobs(*jobs):
    steps = jobs[0]["steps"]
    assert all(j["steps"] == steps for j in jobs)
    n_in = [len(j["in_specs"]) for j in jobs]
    n_out = [len(j["out_specs"]) for j in jobs]
    n_scr = [len(j.get("scratch", ())) for j in jobs]

    def kern(*refs):
        i_pos, o_pos, s_pos = 0, sum(n_in), sum(n_in) + sum(n_out)
        for j, ni, no, ns in zip(jobs, n_in, n_out, n_scr):
            j["body"](*refs[i_pos:i_pos + ni], *refs[o_pos:o_pos + no], *refs[s_pos:s_pos + ns])
            i_pos, o_pos, s_pos = i_pos + ni, o_pos + no, s_pos + ns

    outs = pl.pallas_call(
        kern,
        grid=(steps,),
        in_specs=[s for j in jobs for s in j["in_specs"]],
        out_specs=[s for j in jobs for s in j["out_specs"]],
        out_shape=[s for j in jobs for s in j["out_shape"]],
        scratch_shapes=[s for j in jobs for s in j.get("scratch", ())],
        compiler_params=_params(0, 1),
        name="__".join(j["name"] for j in jobs),
    )(*[a for j in jobs for a in j["args"]])
    split, pos = [], 0
    for no in n_out:
        split.append(list(outs[pos:pos + no]))
        pos += no
    return split


def _gla_sample_kernel(gq_ref, gk_ref, gv_ref, la_ref, gr_ref, g_ref, s_ref, o_ref, so_ref, *, seq):
    qg, kd, kl, dec, causal = _gla_local(gq_ref, gk_ref, la_ref, seq)
    rows_total = qg.shape[0]
    per8 = 8 // seq
    row8 = _iota((8, 1), 0)
    for h in range(GLA_HEADS):
        kc = slice(h * GLA_DK, (h + 1) * GLA_DK)
        vc = slice(h * GLA_DV, (h + 1) * GLA_DV)
        v = gv_ref[:, vc].astype(BF16)
        att = jnp.where(causal, _nt(qg[:, kc], kd[:, kc]), 0.0).astype(BF16)
        intra = _nn(att, v)
        inter = []
        for p in range(rows_total // 8):
            rows = slice(p * 8, (p + 1) * 8)
            d_hi, d_mid, d_lo = _split3(dec[rows, kc])
            inter_p = jnp.zeros((8, GLA_DV), F32)
            for j in range(per8):
                b = p * per8 + j
                r0 = j * seq
                s0 = s_ref[b, h]
                mine = (row8 >= r0) & (row8 < r0 + seq)
                inter_p = jnp.where(mine, _nn(qg[rows, kc], s0.astype(BF16)), inter_p)
                dl = jnp.where(row8 == r0, d_hi, jnp.where(row8 == r0 + 1, d_mid,
                               jnp.where(row8 == r0 + 2, d_lo, jnp.zeros_like(d_lo))))
                e = jnp.where((row8 >= r0) & (row8 < r0 + 3), 1.0, 0.0).astype(BF16)
                dec_b = _tn(dl, jnp.broadcast_to(e, (8, GLA_DV)))
                upd = _tn(jnp.where(mine, kl[rows, kc], jnp.zeros_like(kl[rows, kc])), v[rows])
                so_ref[b, h] = dec_b * s0 + upd
            inter.append(inter_p)
        o = intra + jnp.concatenate(inter, axis=0)
        o_ref[:, vc] = _gla_finish(o, gr_ref[:, vc].astype(F32), g_ref[...])


def _gla_sample_call(gq, gk, gv, la, gr, g, s0, seq, bb):
    n_seq = s0.shape[0]
    rows = bb * seq
    tok = lambda n: pl.BlockSpec((None, rows, n), lambda i: (0, i, 0))
    st = pl.BlockSpec((bb, GLA_HEADS, GLA_DK, GLA_DV), lambda i: (i, 0, 0, 0))
    return dict(
        name="gla_sample", steps=n_seq // bb, body=functools.partial(_gla_sample_kernel, seq=seq),
        args=(gq, gk, gv, la, gr, g, s0),
        in_specs=[tok(512), tok(512), tok(1024), tok(512), tok(1024), _resident((1, GLA_DV)), st],
        out_specs=[tok(1024), st],
        out_shape=[jax.ShapeDtypeStruct((1, n_seq * seq, 1024), BF16),
                   jax.ShapeDtypeStruct(s0.shape, F32)])


def _dsw_prompt_kernel(q_ref, kp_ref, kc_ref, vp_ref, vc_ref, o_ref, lse_ref, *, dil, tile, res):
    qb = q_ref.shape[0]
    first_key = jnp.where(tile == 0, BAND, 0)
    qi = _iota((BAND, 2 * BAND), 0) + BAND
    ki = _iota((BAND, 2 * BAND), 1)
    band = (qi - ki >= 0) & (qi - ki <= BAND)
    lane = _iota((BAND, LANES), 1)
    for s in range(qb // BAND):
        rows = slice(s * BAND, (s + 1) * BAND)
        if s == 0:
            valid = band & (ki >= first_key)
        else:
            valid = band
        if dil == 1:
            tok_rows = rows
        else:
            tok_rows = pl.ds(s * BAND * dil + res, BAND, stride=dil)
        for hp in range(DSW_GW // LANES):
            cols = slice(hp * LANES, (hp + 1) * LANES)
            qp = q_ref[rows, cols]
            if s == 0:
                kcat = jnp.concatenate([kp_ref[:, cols], kc_ref[0:BAND, cols]], axis=0)
                vcat = jnp.concatenate([vp_ref[:, cols], vc_ref[0:BAND, cols]], axis=0)
            else:
                kcat = kc_ref[(s - 1) * BAND:(s + 1) * BAND, cols]
                vcat = vc_ref[(s - 1) * BAND:(s + 1) * BAND, cols]
            outs, lses = [], []
            for hh in range(LANES // DSW_HEAD_DIM):
                in_head = (lane // DSW_HEAD_DIM) == hh
                sc = _nt(jnp.where(in_head, qp, jnp.zeros_like(qp)), kcat)
                sc = jnp.where(valid, sc, -jnp.inf)
                m = jnp.max(sc, axis=-1, keepdims=True)
                e = jnp.exp(sc - m)
                den = jnp.sum(e, axis=-1, keepdims=True)
                outs.append(_nn((e / den).astype(BF16), vcat))
                lses.append(m + jnp.log(den))
            first = lane < DSW_HEAD_DIM
            o_ref[hp, tok_rows, :] = jnp.where(first, outs[0], outs[1])
            lse_ref[hp, tok_rows, :] = jnp.where(first, lses[0], jnp.broadcast_to(lses[1], (BAND, LANES)))


def _dsw_prompt_call(q, k, v, dil):
    nb = q.shape[0]
    seq_len = q.shape[-2]
    t = seq_len * dil
    tq = min(512, seq_len)
    sub = tq // BAND
    n_tiles = seq_len // tq
    bat = lambda i: i // (n_tiles * dil)
    til = lambda i: (i // dil) % n_tiles
    res = lambda i: i % dil
    prev_blk = lambda i: jnp.maximum(til(i) * sub - 1, 0)
    if dil == 1:
        cur = pl.BlockSpec((None, tq, DSW_GW), lambda i: (bat(i), til(i), 0))
        prev = pl.BlockSpec((None, BAND, DSW_GW), lambda i: (bat(i), prev_blk(i), 0))
    else:
        cur = pl.BlockSpec((None, None, tq, DSW_GW), lambda i: (bat(i), res(i), til(i), 0))
        prev = pl.BlockSpec((None, None, BAND, DSW_GW), lambda i: (bat(i), res(i), prev_blk(i), 0))
    n_slab = DSW_GW // LANES
    out = pl.BlockSpec((None, n_slab, tq * dil, LANES), lambda i: (bat(i), 0, til(i), 0))

    def body(*refs):
        i = pl.program_id(0)
        _dsw_prompt_kernel(*refs, dil=dil, tile=til(i), res=res(i))

    return dict(
        name=f"dsw_prompt_d{dil}", steps=nb * n_tiles * dil, body=body, args=(q, k, k, v, v),
        in_specs=[cur, prev, cur, prev, cur],
        out_specs=[out, out],
        out_shape=[jax.ShapeDtypeStruct((nb, n_slab, t, LANES), F32)] * 2)


def _dsw_sample_kernel(q_ref, kn_ref, vn_ref, c_ref, o_ref, lse_ref, co_ref, *, seq, dil):
    per8 = 8 // seq
    win = c_ref.shape[2]
    n_rows = DSW_HEADS * 8
    lane = _iota((8, LANES), 1)
    head_of_lane = _iota((8, DSW_GW), 1) // DSW_HEAD_DIM
    row8 = _iota((8, 1), 0)
    r = _iota((n_rows, 1), 0)
    r_step = r % seq
    r_seq = (r % 8) // seq
    key = _iota((n_rows, win), 1)
    cache_ok = ((key % dil) == (r_step % dil)) & (key >= r_step)
    c128 = _iota((n_rows, LANES), 1)
    new_ok = ((c128 < 8) & ((c128 // seq) == r_seq) & ((c128 % seq) <= r_step)
              & (((r_step - c128 % seq) % dil) == 0))
    pad = jnp.zeros((LANES - 8, DSW_GW), BF16)

    def by_head(x):
        out = x[(DSW_HEADS - 1) * 8:DSW_HEADS * 8]
        for h in range(DSW_HEADS - 2, -1, -1):
            out = jnp.where(head_of_lane == h, x[h * 8:(h + 1) * 8], out)
        return out

    lane_sq = _iota((LANES, LANES), 1)
    p_row = _iota((8, LANES), 0)
    for grp in range(q_ref.shape[0] // 8):
        r8 = slice(grp * 8, (grp + 1) * 8)
        q8 = q_ref[r8, :]
        qrows = jnp.concatenate([jnp.where(head_of_lane == h, q8, 0.0) for h in range(DSW_HEADS)],
                                axis=0).astype(BF16)
        kn8 = kn_ref[r8, :]
        vn8 = vn_ref[r8, :]
        kn_t = jnp.concatenate([kn8.astype(BF16), pad], axis=0)
        vn_t = jnp.concatenate([vn8.astype(BF16), pad], axis=0)
        scn = jnp.where(new_ok, _nt(qrows, kn_t), -jnp.inf)
        m_new = jnp.max(scn, axis=-1, keepdims=True)
        o_p = jnp.zeros((8, DSW_GW), F32)
        l_p = jnp.zeros((8, DSW_GW), F32)
        for j in range(per8):
            b = grp * per8 + j
            kt = c_ref[b, 0:DSW_GW, :].astype(BF16)
            vt = c_ref[b, DSW_GW:2 * DSW_GW, :].astype(BF16)
            sc = jnp.where(cache_ok, _nn(qrows, kt), -jnp.inf)
            m = jnp.maximum(jnp.max(sc, axis=-1, keepdims=True), m_new)
            e = jnp.exp(sc - m)
            en = jnp.exp(scn - m)
            den = jnp.sum(e, axis=-1, keepdims=True) + jnp.sum(en, axis=-1, keepdims=True)
            o = _nt((e / den).astype(BF16), vt) + _nn((en / den).astype(BF16), vn_t)
            lse = jnp.broadcast_to(m + jnp.log(den), (n_rows, DSW_GW))
            mine = (row8 // seq) == j
            o_p = jnp.where(mine, by_head(o), o_p)
            l_p = jnp.where(mine, by_head(lse), l_p)
        o_ref[r8, :] = o_p
        lse_ref[r8, :] = l_p

        hi, mid, lo = _split3(jnp.concatenate([kn8, vn8], axis=1))
        for j in range(per8):
            b = grp * per8 + j
            place = jnp.where(((p_row // seq) == j) & (lane == LANES - seq + p_row % seq), 1.0, 0.0).astype(BF16)
            new_cols = _tn(hi, place) + _tn(mid, place) + _tn(lo, place)
            for blk in range(2 * DSW_GW // LANES):
                rows = slice(blk * LANES, (blk + 1) * LANES)
                rolled = pltpu.roll(c_ref[b, rows, :], win - seq, 1)
                if win > LANES:
                    co_ref[b, rows, 0:win - LANES] = rolled[:, 0:win - LANES]
                co_ref[b, rows, win - LANES:win] = jnp.where(lane_sq < LANES - seq, rolled[:, win - LANES:win],
                                                             new_cols[rows])


def _dsw_sample_call(q32, k32, v32, cache, g, dil, seq, steps=None):
    n_seq, win = cache.shape[0], cache.shape[1]
    per8 = 8 // seq
    if steps is None:
        groups = max(1, min(8, 4 * 512 // win))
    else:
        groups = n_seq // (per8 * steps)
    n_blk = per8 * groups
    view = jnp.transpose(cache, (0, 2, 3, 4, 1)).reshape(n_seq, 2 * DSW_GW, win)
    tok = pl.BlockSpec((None, 8 * groups, DSW_GW), lambda i: (0, i, g))
    tok_out = pl.BlockSpec((None, 8 * groups, DSW_GW), lambda i: (0, i, 0))
    cspec = pl.BlockSpec((n_blk, 2 * DSW_GW, win), lambda i: (i, 0, 0))
    return dict(
        name=f"dsw_sample_d{dil}", steps=n_seq // n_blk,
        body=functools.partial(_dsw_sample_kernel, seq=seq, dil=dil), args=(q32, k32, v32, view),
        in_specs=[tok, tok, tok, cspec],
        out_specs=[tok_out, tok_out, cspec],
        out_shape=[jax.ShapeDtypeStruct((1, n_seq * seq, DSW_GW), F32)] * 2
                  + [jax.ShapeDtypeStruct(view.shape, F32)])


def _dsw_sample_finish(o, lse, new, cache_shape):
    n_seq, win = cache_shape[0], cache_shape[1]
    new = jnp.transpose(new.reshape(n_seq, 2, DSW_HEADS, DSW_HEAD_DIM, win), (0, 4, 1, 2, 3))
    n_slab = DSW_GW // LANES
    slabs = lambda a: jnp.transpose(a.reshape(1, -1, n_slab, LANES), (0, 2, 1, 3))
    return slabs(o), slabs(lse), new


def _merge_kernel(oa_ref, o0_ref, o1_ref, o2_ref, l0_ref, l1_ref, l2_ref, ga_ref, gb_ref, x_ref,
                  g1_ref, sc_ref, sh_ref, n2_ref, wpa_ref, wpb_ref, wo_ref, x1_ref, h2_ref):
    ob = []
    for slab in range(DSW_GW // LANES):
        l0, l1, l2 = l0_ref[slab], l1_ref[slab], l2_ref[slab]
        m = jnp.maximum(jnp.maximum(l0, l1), l2)
        w0, w1, w2 = jnp.exp(l0 - m), jnp.exp(l1 - m), jnp.exp(l2 - m)
        den = w0 + w1 + w2
        ob.append((w0 / den) * o0_ref[slab] + (w1 / den) * o1_ref[slab] + (w2 / den) * o2_ref[slab])
    ob = jnp.concatenate(ob, axis=1).astype(BF16)
    merged = (jax.nn.sigmoid(ga_ref[...].astype(F32)) * _nn(oa_ref[...], wpa_ref[...])
              + jax.nn.sigmoid(gb_ref[...].astype(F32)) * _nn(ob, wpb_ref[...]))
    x1 = x_ref[...] + g1_ref[...] * _nn(merged.astype(BF16), wo_ref[...])
    x1_ref[...] = x1
    h2_ref[...] = (_rms(x1, n2_ref[...]) * (1.0 + sc_ref[...]) + sh_ref[...]).astype(BF16)


def _flat_tok_spec(tm, tiles, n):
    return pl.BlockSpec((None, tm, n), lambda i: (i // tiles, i % tiles, 0))


def _flat_mod_spec(m, tm, tiles, d):
    arr, k = m
    if arr.ndim == 4:
        return pl.BlockSpec((None, None, 1, d), lambda i: (i // tiles, k, 0, 0))
    return pl.BlockSpec((tm, d), lambda i: (i % tiles, k))


def _merge_call(oa, og, lg, ga, gb, x, g1, sc2, sh2, n2, wpa, wpb, wo, tm):
    nb, t, d = x.shape
    tiles = t // tm
    tok = functools.partial(_flat_tok_spec, tm, tiles)
    mod = lambda a: _flat_mod_spec(a, tm, tiles, d)
    slab = pl.BlockSpec((None, DSW_GW // LANES, tm, LANES), lambda i: (i // tiles, 0, i % tiles, 0))
    return dict(
        name="merge_outproj", steps=nb * tiles, body=_merge_kernel,
        args=(oa, *og, *lg, ga, gb, x, g1[0], sc2[0], sh2[0], n2, wpa, wpb, wo),
        in_specs=[tok(1024)] + [slab] * 6 + [tok(d), tok(d), tok(d), mod(g1), mod(sc2), mod(sh2),
                  _resident((1, d)), _resident(wpa.shape), _resident(wpb.shape), _resident(wo.shape)],
        out_specs=[tok(d), tok(d)],
        out_shape=[jax.ShapeDtypeStruct((nb, t, d), F32), jax.ShapeDtypeStruct((nb, t, d), BF16)])


def _ffn_kernel(h_ref, x_ref, g2_ref, nf_ref, wu_ref, wd_ref, y_ref, *, final_norm, n_split):
    h = h_ref[...]
    d_ff = wd_ref.shape[0]
    step = d_ff // n_split
    acc = None
    for j in range(n_split):
        u1 = _nn(h, wu_ref[:, j * step:(j + 1) * step])
        u2 = _nn(h, wu_ref[:, d_ff + j * step:d_ff + (j + 1) * step])
        a = (u1 * jax.nn.sigmoid(u1) * u2).astype(BF16)
        part = _nn(a, wd_ref[j * step:(j + 1) * step, :])
        acc = part if acc is None else acc + part
    x2 = x_ref[...] + g2_ref[...] * acc
    y_ref[...] = _rms(x2, nf_ref[...]) if final_norm else x2


def _ffn_call(h2, x1, g2, nf, wu, wd, tm, final_norm):
    nb, t, d = x1.shape
    tiles = t // tm
    tok = functools.partial(_flat_tok_spec, tm, tiles)
    return dict(
        name="ffn", steps=nb * tiles,
        body=functools.partial(_ffn_kernel, final_norm=final_norm, n_split=wd.shape[0] // (2 * LANES)),
        args=(h2, x1, g2[0], nf, wu, wd),
        in_specs=[tok(d), tok(d), _flat_mod_spec(g2, tm, tiles, d), _resident((1, d)),
                  _resident(wu.shape), _resident(wd.shape)],
        out_specs=[tok(d)],
        out_shape=[jax.ShapeDtypeStruct((nb, t, d), F32)])


def _rope_tables(pos):
    half = DSW_HEAD_DIM // 2
    inv = ROPE_THETA ** (-np.arange(half, dtype=np.float64) / half)
    ang = np.asarray(pos, np.float64)[:, None] * inv[None, :]
    reps = LANES // half
    sign = np.tile(np.concatenate([-np.ones(half), np.ones(half)]), LANES // DSW_HEAD_DIM)
    cosf = np.tile(np.cos(ang), (1, reps))
    sins = np.tile(np.sin(ang), (1, reps)) * sign[None, :]
    return jnp.asarray(cosf, F32), jnp.asarray(sins, F32)


def _layer_weights(w_in, b_in, w_alpha2, b_alpha2):
    bf = lambda a: a.astype(BF16)
    row = lambda a: a.reshape(1, -1)
    o_glr, o_dq, o_ga = 3072, 3088, 5392
    pad_r = LANES - GLA_RANK
    wt = w_in.T
    return dict(
        wa=bf(wt[:o_glr]), ba=row(b_in[:o_glr]),
        wg=bf(jnp.pad(wt[o_glr:o_dq], ((0, pad_r), (0, 0)))), bg=row(jnp.pad(b_in[o_glr:o_dq], (0, pad_r))),
        w2=bf(jnp.pad(w_alpha2, ((0, pad_r), (0, 0)))), b2=row(b_alpha2),
        wb=bf(wt[o_dq:o_ga]), bb=row(b_in[o_dq:o_ga]),
        wc=bf(wt[o_ga:]), bc=row(b_in[o_ga:]))


def _kv_unstack(kvt, keep):
    nb, _, width = kvt.shape
    kv = kvt[:, :, width - keep:].reshape(nb, 2, DSW_HEADS, DSW_HEAD_DIM, keep)
    return jnp.transpose(kv, (0, 4, 1, 2, 3))


def kernel(x_prompt, x_sample, state_gla, cache_kv_w128, cache_kv_w512, cache_kv_w2048, c_prompt, c_sample,
           norm1_g, norm2_g, w_ada, b_ada, w_in, b_in, w_alpha2, b_alpha2, gla_norm_g, w_proj_a, w_proj_b,
           w_out, w_up, w_down, normf_g):
    depth = w_ada.shape[0]
    nb, t, d = x_prompt.shape
    n_seq, seq, _ = x_sample.shape
    assert 8 % seq == 0 and seq >= 3, "sample kernels pack whole sequences into 8-row groups"
    past = PAST_LEN
    caches = (cache_kv_w128, cache_kv_w512, cache_kv_w2048)

    cos_p, sin_p = _rope_tables(np.arange(t))
    cos_s, sin_s = _rope_tables(np.tile(past + np.arange(seq), n_seq))

    n_tok_s = n_seq * seq
    pad_c = (-(n_tok_s + nb)) % 8
    c_all = jnp.pad(jnp.concatenate([jnp.repeat(c_sample, seq, axis=0), c_prompt], axis=0), ((0, pad_c), (0, 0)))

    xp = x_prompt
    xs = x_sample.reshape(1, n_seq * seq, d)
    row = lambda a: a.reshape(1, -1)
    sp_l, kvp_l, ss_l, kvs_l = [], [], [], []
    for l in range(depth):
        w = _layer_weights(w_in[l], b_in[l], w_alpha2[l], b_alpha2[l])
        mod = _ada(c_all, w_ada[l], b_ada[l])
        mod_rows_p = mod[n_tok_s:n_tok_s + nb].reshape(nb, 6, 1, d)
        mod_p = [(mod_rows_p, k) for k in range(6)]
        mod_s = [(mod, k) for k in range(6)]
        last = l == depth - 1

        sh1_s, sc1_s, g1_s, sh2_s, sc2_s, g2_s = mod_s
        ((gq_s, gk_s, gv_s, gr_s, la_s, ga_s, gb_s, k32, v32, q32),) = _run_jobs(_inproj_call(
            xs, sc1_s, sh1_s, row(norm1_g[l]), cos_s, sin_s, w, tm=min(512, n_seq * seq), fold=False))
        sh1, sc1, g1, sh2, sc2, g2 = mod_p
        inproj_p = _inproj_call(xp, sc1, sh1, row(norm1_g[l]), cos_p, sin_p, w, tm=512, fold=True)
        late = (w_proj_a[l], w_proj_b[l], w_out[l], w_up[l], w_down[l])
        (gq, gk, gv, gr, la, ga, gb, *dsw_p), *cast = _run_jobs(
            inproj_p, *[_cast_call(a, inproj_p["steps"]) for a in late])
        w.update(zip(("wpa", "wpb", "wo", "wu", "wd"), (c[0] for c in cast)))
        qkv, kvt = dsw_p[:3 * len(DSW_GROUPS)], dsw_p[3 * len(DSW_GROUPS):]

        def with_sample_group(host, g):
            steps = host["steps"]
            fits = n_seq % ((8 // seq) * steps) == 0 and n_seq // ((8 // seq) * steps) <= 8
            job = _dsw_sample_call(q32, k32, v32, caches[g][l], g, DSW_GROUPS[g][1], seq, steps if fits else None)
            if fits:
                host_out, job_out = _run_jobs(host, job)
            else:
                (host_out,), (job_out,) = _run_jobs(host), _run_jobs(job)
            return host_out, _dsw_sample_finish(*job_out, caches[g][l].shape)

        sample_dsw = [None] * len(DSW_GROUPS)
        (oa, st), sample_dsw[2] = with_sample_group(
            _gla_prompt_call(gq, gk, gv, la, gr, row(gla_norm_g[l]), tt=256), 2)
        dsw_jobs = [_dsw_prompt_call(*qkv[3 * g:3 * g + 3], dil) for g, (_, dil) in enumerate(DSW_GROUPS)]
        gla_s_args = (gq_s, gk_s, gv_s, la_s, gr_s, row(gla_norm_g[l]), state_gla[l], seq)
        host = next((j for j in dsw_jobs if n_seq % j["steps"] == 0 and (n_seq // j["steps"]) * seq % 16 == 0), None)
        og, lg = [], []
        for job in dsw_jobs:
            if job is host:
                (o_g, l_g), (oa_s, s_new) = _run_jobs(job, _gla_sample_call(*gla_s_args, bb=n_seq // job["steps"]))
            else:
                ((o_g, l_g),) = _run_jobs(job)
            og.append(o_g)
            lg.append(l_g)
        if host is None:
            ((oa_s, s_new),) = _run_jobs(_gla_sample_call(*gla_s_args, bb=8))

        (x1, h2), sample_dsw[0] = with_sample_group(
            _merge_call(oa, og, lg, ga, gb, xp, g1, sc2, sh2, row(norm2_g[l]), w["wpa"], w["wpb"], w["wo"], tm=512), 0)
        (xp,), sample_dsw[1] = with_sample_group(
            _ffn_call(h2, x1, g2, row(normf_g), w["wu"], w["wd"], tm=512, final_norm=last), 1)
        og_s, lg_s, new_kv = zip(*sample_dsw)

        ((x1, h2),) = _run_jobs(_merge_call(oa_s, og_s, lg_s, ga_s, gb_s, xs, g1_s, sc2_s, sh2_s, row(norm2_g[l]),
                                            w["wpa"], w["wpb"], w["wo"], tm=min(512, n_seq * seq)))
        ((xs,),) = _run_jobs(_ffn_call(h2, x1, g2_s, row(normf_g), w["wu"], w["wd"], tm=min(512, n_seq * seq), final_norm=last))

        sp_l.append(jnp.swapaxes(st, 2, 3))
        kvp_l.append(tuple(_kv_unstack(kvt[g], min(win, t)) for g, (win, _) in enumerate(DSW_GROUPS)))
        ss_l.append(s_new)
        kvs_l.append(new_kv)

    y_prompt = xp
    y_sample = xs.reshape(n_seq, seq, d)
    stack = lambda items: jnp.stack(list(items))
    return (y_prompt, y_sample, stack(sp_l),
            stack(kv[0] for kv in kvp_l), stack(kv[1] for kv in kvp_l), stack(kv[2] for kv in kvp_l),
            stack(ss_l),
            stack(kv[0] for kv in kvs_l), stack(kv[1] for kv in kvs_l), stack(kv[2] for kv in kvs_l))
```

```python
import functools

import jax
import jax.numpy as jnp
import numpy as np
from jax import lax
from jax.experimental import pallas as pl
from jax.experimental.pallas import tpu as pltpu

F32 = jnp.float32
BF16 = jnp.bfloat16

EPS = 1e-6
GLA_HEADS = 4
GLA_DK = 128
GLA_DV = 256
GLA_RANK = 16
GLA_TAU = 16.0
GLA_CHUNK = 64
GLA_SCALE = GLA_DK ** -0.5
DSW_GROUPS = ((128, 1), (512, 4), (2048, 16))
DSW_HEADS = 4
DSW_HEAD_DIM = 64
DSW_SCALE = DSW_HEAD_DIM ** -0.5
DSW_GW = DSW_HEADS * DSW_HEAD_DIM
BAND = 128
ROPE_THETA = 10000.0
PAST_LEN = 8192
LANES = 128
VMEM_LIMIT = 56 * 1024 * 1024


def _nn(a, b):
    return jnp.dot(a, b, preferred_element_type=F32)


def _nt(a, b):
    return lax.dot_general(a, b, (((1,), (1,)), ((), ())), preferred_element_type=F32)


def _tn(a, b):
    return lax.dot_general(a, b, (((0,), (0,)), ((), ())), preferred_element_type=F32)


def _split3(x):
    hi = x.astype(BF16)
    r1 = x - hi.astype(F32)
    mid = r1.astype(BF16)
    lo = (r1 - mid.astype(F32)).astype(BF16)
    return hi, mid, lo


def _iota(shape, dim):
    return lax.broadcasted_iota(jnp.int32, shape, dim)


def _rms(x, g):
    return x * lax.rsqrt(jnp.mean(x * x, axis=-1, keepdims=True) + EPS) * g


def _params(n_parallel, n_arbitrary=0):
    sem = ("parallel",) * n_parallel + ("arbitrary",) * n_arbitrary
    return pltpu.CompilerParams(dimension_semantics=sem, vmem_limit_bytes=VMEM_LIMIT)


def _resident(shape):
    nd = len(shape)
    return pl.BlockSpec(shape, lambda *_: (0,) * nd, pipeline_mode=pl.Buffered(1))


def _ada_kernel(c_ref, w_ref, b_ref, o_ref):
    c = c_ref[...]
    a = (c * jax.nn.sigmoid(c)).astype(BF16)
    o_ref[...] = _nn(a, w_ref[...].astype(BF16)) + b_ref[...]


def _ada(c_all, w_ada, b_ada):
    n, d = c_all.shape
    ncol = w_ada.shape[1]
    tn = 1536
    return pl.pallas_call(
        _ada_kernel,
        grid=(ncol // tn,),
        in_specs=[pl.BlockSpec((n, d), lambda j: (0, 0)),
                  pl.BlockSpec((d, tn), lambda j: (0, j)),
                  pl.BlockSpec((1, tn), lambda j: (0, j))],
        out_specs=pl.BlockSpec((n, tn), lambda j: (0, j)),
        out_shape=jax.ShapeDtypeStruct((n, ncol), F32),
        compiler_params=_params(1),
        name="ada_mod",
    )(c_all, w_ada, b_ada.reshape(1, ncol))


def _rope(x, cosf, sins, first_half):
    rot = jnp.where(first_half, pltpu.roll(x, LANES - 32, 1), pltpu.roll(x, 32, 1))
    return x * cosf + rot * sins


def _inproj_kernel(x_ref, sc_ref, sh_ref, g_ref, cos_ref, sin_ref,
                   wa_ref, ba_ref, wg_ref, bg_ref, w2_ref, b2_ref, wb_ref, bb_ref, wc_ref, bc_ref,
                   gq_ref, gk_ref, gv_ref, gr_ref, la_ref, ga_ref, gb_ref, *rest, fold):
    x = x_ref[...]
    h = (_rms(x, g_ref[...]) * (1.0 + sc_ref[...]) + sh_ref[...]).astype(BF16)

    cosf = cos_ref[...]
    sins = sin_ref[...]
    first_half = (_iota(cosf.shape, 1) % DSW_HEAD_DIM) < (DSW_HEAD_DIM // 2)
    tm = x.shape[0]
    per_group = DSW_GW // LANES
    width = 3 * DSW_GW

    def proj(off, g):
        cols = slice(off + g * DSW_GW, off + (g + 1) * DSW_GW)
        full = _nt(h, wb_ref[cols, :]) + bb_ref[:, cols]
        return [full[:, s * LANES:(s + 1) * LANES] for s in range(per_group)]

    for g, (_, dil) in enumerate(DSW_GROUPS):
        qs = [_rope(a, cosf, sins, first_half) * DSW_SCALE for a in proj(0, g)]
        ks = [_rope(a, cosf, sins, first_half) for a in proj(width, g)]
        vs = proj(2 * width, g)
        if fold:
            kvt_ref = rest[3 * len(DSW_GROUPS) + g]
            kvt_ref[0:DSW_GW, :] = jnp.concatenate(ks, axis=1).T
            kvt_ref[DSW_GW:2 * DSW_GW, :] = jnp.concatenate(vs, axis=1).T

        for slab in range(per_group):
            cols = slice(g * DSW_GW + slab * LANES, g * DSW_GW + (slab + 1) * LANES)
            lanes = slice(slab * LANES, (slab + 1) * LANES)
            if not fold:
                rest[0][:, cols] = ks[slab]
                rest[1][:, cols] = vs[slab]
                rest[2][:, cols] = qs[slab]
                continue
            scratch = rest[-1]
            for which, val in enumerate((qs[slab], ks[slab], vs[slab])):
                out_ref = rest[3 * g + which]
                if dil == 1:
                    out_ref[:, lanes] = val.astype(BF16)
                else:
                    scratch[which] = val
                    for r in range(dil):
                        out_ref[r, :, lanes] = scratch[which, pl.ds(r, tm // dil, stride=dil), :].astype(BF16)

    glr = (_nt(h, wg_ref[...]) + bg_ref[...]).astype(BF16)
    z = _nn(glr, w2_ref[...]) + b2_ref[...]
    la_ref[...] = jax.nn.log_sigmoid(z) * (1.0 / GLA_TAU)

    gq_ref[...] = (_nt(h, wa_ref[0:512, :]) + ba_ref[:, 0:512]).astype(gq_ref.dtype)
    gk_ref[...] = (_nt(h, wa_ref[512:1024, :]) + ba_ref[:, 512:1024]).astype(gk_ref.dtype)
    gv_ref[...] = (_nt(h, wa_ref[1024:2048, :]) + ba_ref[:, 1024:2048]).astype(gv_ref.dtype)
    gr_ref[...] = (_nt(h, wa_ref[2048:3072, :]) + ba_ref[:, 2048:3072]).astype(gr_ref.dtype)

    ga_ref[...] = (_nt(h, wc_ref[0:1024, :]) + bc_ref[:, 0:1024]).astype(ga_ref.dtype)
    gb_ref[...] = (_nt(h, wc_ref[1024:2048, :]) + bc_ref[:, 1024:2048]).astype(gb_ref.dtype)


def _inproj_call(x, sc, sh, g, cos_t, sin_t, w, tm, fold):
    nb, t, d = x.shape
    tiles = t // tm
    tok = functools.partial(_flat_tok_spec, tm, tiles)
    mod = lambda a: _flat_mod_spec(a, tm, tiles, d)
    out_cols = (512, 512, 1024, 1024, 512, 1024, 1024)
    out_dt = (BF16, BF16, BF16, BF16, F32, BF16, BF16)
    out_specs = [tok(n) for n in out_cols]
    out_shape = [jax.ShapeDtypeStruct((nb, t, n), dt) for n, dt in zip(out_cols, out_dt)]
    scratch = []
    if fold:
        for _, dil in DSW_GROUPS:
            for _ in range(3):
                if dil == 1:
                    out_specs.append(tok(DSW_GW))
                    out_shape.append(jax.ShapeDtypeStruct((nb, t, DSW_GW), BF16))
                else:
                    out_specs.append(pl.BlockSpec((None, dil, tm // dil, DSW_GW),
                                                  lambda i: (i // tiles, 0, i % tiles, 0)))
                    out_shape.append(jax.ShapeDtypeStruct((nb, dil, t // dil, DSW_GW), BF16))
        for win, _ in DSW_GROUPS:
            width = min(max(win, tm), t)
            first = (t - width) // tm
            out_specs.append(pl.BlockSpec((None, 2 * DSW_GW, tm),
                                          lambda i, first=first: (i // tiles, 0, jnp.maximum(i % tiles - first, 0))))
            out_shape.append(jax.ShapeDtypeStruct((nb, 2 * DSW_GW, width), F32))
        scratch = [pltpu.VMEM((3, tm, LANES), F32)]
    else:
        for _ in range(3):
            out_specs.append(tok(3 * DSW_GW))
            out_shape.append(jax.ShapeDtypeStruct((nb, t, 3 * DSW_GW), F32))
    weights = (w["wa"], w["ba"], w["wg"], w["bg"], w["w2"], w["b2"], w["wb"], w["bb"], w["wc"], w["bc"])
    table = pl.BlockSpec((tm, LANES), lambda i: (i % tiles, 0))
    return dict(
        name="inproj", steps=nb * tiles, body=functools.partial(_inproj_kernel, fold=fold),
        args=(x, sc[0], sh[0], g, cos_t, sin_t, *weights),
        in_specs=[tok(d), mod(sc), mod(sh), _resident((1, d)), table, table] + [_resident(a.shape) for a in weights],
        out_specs=out_specs, out_shape=out_shape, scratch=scratch)


def _cast_kernel(x_ref, o_ref):
    o_ref[...] = x_ref[...].astype(o_ref.dtype)


def _cast_call(w, steps):
    rows, cols = w.shape
    blk = next(b for b in range(16, rows + 1, 16) if rows % b == 0 and b * steps >= rows)
    n_blk = rows // blk
    spec = pl.BlockSpec((blk, cols), lambda i: (jnp.minimum(i, n_blk - 1), 0))
    return dict(name="cast", steps=steps, body=_cast_kernel, args=(w,), in_specs=[spec], out_specs=[spec],
                out_shape=[jax.ShapeDtypeStruct(w.shape, BF16)])


def _gla_local(gq_ref, gk_ref, la_ref, chunk):
    la = la_ref[...]
    tt = la.shape[0]
    r = _iota((tt, tt), 0)
    c = _iota((tt, tt), 1)
    same = (r // chunk) == (c // chunk)
    tri = jnp.where(same & (c <= r), 1.0, 0.0).astype(BF16)
    hi, mid, lo = _split3(la)
    b = _nn(tri, hi) + _nn(tri, mid) + _nn(tri, lo)
    if chunk % 8 == 0:
        bl = jnp.concatenate([jnp.broadcast_to(b[e - 1:e, :], (chunk, b.shape[1]))
                              for e in range(chunk, tt + 1, chunk)], axis=0)
    else:
        ones = jnp.where(same, 1.0, 0.0).astype(BF16)
        bl = _nn(ones, hi) + _nn(ones, mid) + _nn(ones, lo)
    gq = gq_ref[...].astype(F32)
    gk = gk_ref[...].astype(F32)
    qg = (gq * GLA_SCALE * jnp.exp(b)).astype(BF16)
    kd = (gk * jnp.exp(-b)).astype(BF16)
    kl = (gk * jnp.exp(bl - b)).astype(BF16)
    causal = same & (c <= r)
    return qg, kd, kl, jnp.exp(bl), causal


def _gla_finish(o, gr, g):
    return (_rms(o, g) * (gr * jax.nn.sigmoid(gr))).astype(BF16)


def _gla_prompt_body(first_tile, gq_ref, gk_ref, gv_ref, la_ref, gr_ref, g_ref, o_ref, st_ref):
    @pl.when(first_tile)
    def _():
        st_ref[...] = jnp.zeros_like(st_ref)

    qg, kd, kl, dec, causal = _gla_local(gq_ref, gk_ref, la_ref, GLA_CHUNK)
    tt = qg.shape[0]
    chunk_of_row = _iota((tt, GLA_DK), 0) // GLA_CHUNK
    for h in range(GLA_HEADS):
        kc = slice(h * GLA_DK, (h + 1) * GLA_DK)
        vc = slice(h * GLA_DV, (h + 1) * GLA_DV)
        v = gv_ref[:, vc].astype(BF16)
        att = jnp.where(causal, _nt(qg[:, kc], kd[:, kc]), 0.0).astype(BF16)
        intra = _nn(att, v)
        st = st_ref[h]
        n_chunks = tt // GLA_CHUNK
        kl_h = kl[:, kc]
        kl_bd = jnp.concatenate([jnp.where(chunk_of_row == ci, kl_h, jnp.zeros_like(kl_h))
                                 for ci in range(n_chunks)], axis=1)
        upd = _tn(v, kl_bd)
        inter = []
        for ci in range(n_chunks):
            rows = slice(ci * GLA_CHUNK, (ci + 1) * GLA_CHUNK)
            inter.append(_nt(qg[rows, kc], st.astype(BF16)))
            st = dec[ci * GLA_CHUNK:ci * GLA_CHUNK + 1, kc] * st + upd[:, ci * GLA_DK:(ci + 1) * GLA_DK]
        st_ref[h] = st
        o = intra + jnp.concatenate(inter, axis=0)
        o_ref[:, vc] = _gla_finish(o, gr_ref[:, vc].astype(F32), g_ref[...])


def _gla_prompt_call(gq, gk, gv, la, gr, g, tt):
    nb, t, _ = gq.shape
    tiles = t // tt
    tok = lambda n: pl.BlockSpec((None, tt, n), lambda i: (i // tiles, i % tiles, 0))
    def body(*refs):
        _gla_prompt_body(pl.program_id(0) % tiles == 0, *refs)

    return dict(
        name="gla_prompt", steps=nb * tiles, body=body, args=(gq, gk, gv, la, gr, g),
        in_specs=[tok(512), tok(512), tok(1024), tok(512), tok(1024), _resident((1, GLA_DV))],
        out_specs=[tok(1024),
                   pl.BlockSpec((None, GLA_HEADS, GLA_DV, GLA_DK), lambda i: (i // tiles, 0, 0, 0))],
        out_shape=[jax.ShapeDtypeStruct((nb, t, 1024), BF16),
                   jax.ShapeDtypeStruct((nb, GLA_HEADS, GLA_DV, GLA_DK), F32)])


def _run_jobs(*jobs):
    steps = jobs[0]["steps"]
    assert all(j["steps"] == steps for j in jobs)
    n_in = [len(j["in_specs"]) for j in jobs]
    n_out = [len(j["out_specs"]) for j in jobs]
    n_scr = [len(j.get("scratch", ())) for j in jobs]

    def kern(*refs):
        i_pos, o_pos, s_pos = 0, sum(n_in), sum(n_in) + sum(n_out)
        for j, ni, no, ns in zip(jobs, n_in, n_out, n_scr):
            j["body"](*refs[i_pos:i_pos + ni], *refs[o_pos:o_pos + no], *refs[s_pos:s_pos + ns])
            i_pos, o_pos, s_pos = i_pos + ni, o_pos + no, s_pos + ns

    outs = pl.pallas_call(
        kern,
        grid=(steps,),
        in_specs=[s for j in jobs for s in j["in_specs"]],
        out_specs=[s for j in jobs for s in j["out_specs"]],
        out_shape=[s for j in jobs for s in j["out_shape"]],
        scratch_shapes=[s for j in jobs for s in j.get("scratch", ())],
        compiler_params=_params(0, 1),
        name="__".join(j["name"] for j in jobs),
    )(*[a for j in jobs for a in j["args"]])
    split, pos = [], 0
    for no in n_out:
        split.append(list(outs[pos:pos + no]))
        pos += no
    return split


def _gla_sample_kernel(gq_ref, gk_ref, gv_ref, la_ref, gr_ref, g_ref, s_ref, o_ref, so_ref, *, seq):
    qg, kd, kl, dec, causal = _gla_local(gq_ref, gk_ref, la_ref, seq)
    rows_total = qg.shape[0]
    per8 = 8 // seq
    row8 = _iota((8, 1), 0)
    for h in range(GLA_HEADS):
        kc = slice(h * GLA_DK, (h + 1) * GLA_DK)
        vc = slice(h * GLA_DV, (h + 1) * GLA_DV)
        v = gv_ref[:, vc].astype(BF16)
        att = jnp.where(causal, _nt(qg[:, kc], kd[:, kc]), 0.0).astype(BF16)
        intra = _nn(att, v)
        inter = []
        for p in range(rows_total // 8):
            rows = slice(p * 8, (p + 1) * 8)
            d_hi, d_mid, d_lo = _split3(dec[rows, kc])
            inter_p = jnp.zeros((8, GLA_DV), F32)
            for j in range(per8):
                b = p * per8 + j
                r0 = j * seq
                s0 = s_ref[b, h]
                mine = (row8 >= r0) & (row8 < r0 + seq)
                inter_p = jnp.where(mine, _nn(qg[rows, kc], s0.astype(BF16)), inter_p)
                dl = jnp.where(row8 == r0, d_hi, jnp.where(row8 == r0 + 1, d_mid,
                               jnp.where(row8 == r0 + 2, d_lo, jnp.zeros_like(d_lo))))
                e = jnp.where((row8 >= r0) & (row8 < r0 + 3), 1.0, 0.0).astype(BF16)
                dec_b = _tn(dl, jnp.broadcast_to(e, (8, GLA_DV)))
                upd = _tn(jnp.where(mine, kl[rows, kc], jnp.zeros_like(kl[rows, kc])), v[rows])
                so_ref[b, h] = dec_b * s0 + upd
            inter.append(inter_p)
        o = intra + jnp.concatenate(inter, axis=0)
        o_ref[:, vc] = _gla_finish(o, gr_ref[:, vc].astype(F32), g_ref[...])


def _gla_sample_call(gq, gk, gv, la, gr, g, s0, seq, bb):
    n_seq = s0.shape[0]
    rows = bb * seq
    tok = lambda n: pl.BlockSpec((None, rows, n), lambda i: (0, i, 0))
    st = pl.BlockSpec((bb, GLA_HEADS, GLA_DK, GLA_DV), lambda i: (i, 0, 0, 0))
    return dict(
        name="gla_sample", steps=n_seq // bb, body=functools.partial(_gla_sample_kernel, seq=seq),
        args=(gq, gk, gv, la, gr, g, s0),
        in_specs=[tok(512), tok(512), tok(1024), tok(512), tok(1024), _resident((1, GLA_DV)), st],
        out_specs=[tok(1024), st],
        out_shape=[jax.ShapeDtypeStruct((1, n_seq * seq, 1024), BF16),
                   jax.ShapeDtypeStruct(s0.shape, F32)])


def _dsw_prompt_kernel(q_ref, kp_ref, kc_ref, vp_ref, vc_ref, o_ref, lse_ref, *, dil, tile, res):
    qb = q_ref.shape[0]
    first_key = jnp.where(tile == 0, BAND, 0)
    qi = _iota((BAND, 2 * BAND), 0) + BAND
    ki = _iota((BAND, 2 * BAND), 1)
    band = (qi - ki >= 0) & (qi - ki <= BAND)
    lane = _iota((BAND, LANES), 1)
    for s in range(qb // BAND):
        rows = slice(s * BAND, (s + 1) * BAND)
        if s == 0:
            valid = band & (ki >= first_key)
        else:
            valid = band
        if dil == 1:
            tok_rows = rows
        else:
            tok_rows = pl.ds(s * BAND * dil + res, BAND, stride=dil)
        for hp in range(DSW_GW // LANES):
            cols = slice(hp * LANES, (hp + 1) * LANES)
            qp = q_ref[rows, cols]
            if s == 0:
                kcat = jnp.concatenate([kp_ref[:, cols], kc_ref[0:BAND, cols]], axis=0)
                vcat = jnp.concatenate([vp_ref[:, cols], vc_ref[0:BAND, cols]], axis=0)
            else:
                kcat = kc_ref[(s - 1) * BAND:(s + 1) * BAND, cols]
                vcat = vc_ref[(s - 1) * BAND:(s + 1) * BAND, cols]
            outs, lses = [], []
            for hh in range(LANES // DSW_HEAD_DIM):
                in_head = (lane // DSW_HEAD_DIM) == hh
                sc = _nt(jnp.where(in_head, qp, jnp.zeros_like(qp)), kcat)
                sc = jnp.where(valid, sc, -jnp.inf)
                m = jnp.max(sc, axis=-1, keepdims=True)
                e = jnp.exp(sc - m)
                den = jnp.sum(e, axis=-1, keepdims=True)
                outs.append(_nn((e / den).astype(BF16), vcat))
                lses.append(m + jnp.log(den))
            first = lane < DSW_HEAD_DIM
            o_ref[hp, tok_rows, :] = jnp.where(first, outs[0], outs[1])
            lse_ref[hp, tok_rows, :] = jnp.where(first, lses[0], jnp.broadcast_to(lses[1], (BAND, LANES)))


def _dsw_prompt_call(q, k, v, dil):
    nb = q.shape[0]
    seq_len = q.shape[-2]
    t = seq_len * dil
    tq = min(1024, seq_len)
    sub = tq // BAND
    n_tiles = seq_len // tq
    bat = lambda i: i // (n_tiles * dil)
    til = lambda i: (i // dil) % n_tiles
    res = lambda i: i % dil
    prev_blk = lambda i: jnp.maximum(til(i) * sub - 1, 0)
    if dil == 1:
        cur = pl.BlockSpec((None, tq, DSW_GW), lambda i: (bat(i), til(i), 0))
        prev = pl.BlockSpec((None, BAND, DSW_GW), lambda i: (bat(i), prev_blk(i), 0))
    else:
        cur = pl.BlockSpec((None, None, tq, DSW_GW), lambda i: (bat(i), res(i), til(i), 0))
        prev = pl.BlockSpec((None, None, BAND, DSW_GW), lambda i: (bat(i), res(i), prev_blk(i), 0))
    n_slab = DSW_GW // LANES
    out = pl.BlockSpec((None, n_slab, tq * dil, LANES), lambda i: (bat(i), 0, til(i), 0))

    def body(*refs):
        i = pl.program_id(0)
        _dsw_prompt_kernel(*refs, dil=dil, tile=til(i), res=res(i))

    return dict(
        name=f"dsw_prompt_d{dil}", steps=nb * n_tiles * dil, body=body, args=(q, k, k, v, v),
        in_specs=[cur, prev, cur, prev, cur],
        out_specs=[out, out],
        out_shape=[jax.ShapeDtypeStruct((nb, n_slab, t, LANES), F32)] * 2)


def _dsw_sample_kernel(q_ref, kn_ref, vn_ref, c_ref, o_ref, lse_ref, co_ref, *, seq, dil):
    per8 = 8 // seq
    win = c_ref.shape[2]
    n_rows = DSW_HEADS * 8
    lane = _iota((8, LANES), 1)
    head_of_lane = _iota((8, DSW_GW), 1) // DSW_HEAD_DIM
    row8 = _iota((8, 1), 0)
    r = _iota((n_rows, 1), 0)
    r_step = r % seq
    r_seq = (r % 8) // seq
    key = _iota((n_rows, win), 1)
    cache_ok = ((key % dil) == (r_step % dil)) & (key >= r_step)
    c128 = _iota((n_rows, LANES), 1)
    new_ok = ((c128 < 8) & ((c128 // seq) == r_seq) & ((c128 % seq) <= r_step)
              & (((r_step - c128 % seq) % dil) == 0))
    pad = jnp.zeros((LANES - 8, DSW_GW), BF16)

    def by_head(x):
        out = x[(DSW_HEADS - 1) * 8:DSW_HEADS * 8]
        for h in range(DSW_HEADS - 2, -1, -1):
            out = jnp.where(head_of_lane == h, x[h * 8:(h + 1) * 8], out)
        return out

    lane_sq = _iota((LANES, LANES), 1)
    p_row = _iota((8, LANES), 0)
    for grp in range(q_ref.shape[0] // 8):
        r8 = slice(grp * 8, (grp + 1) * 8)
        q8 = q_ref[r8, :]
        qrows = jnp.concatenate([jnp.where(head_of_lane == h, q8, 0.0) for h in range(DSW_HEADS)],
                                axis=0).astype(BF16)
        kn8 = kn_ref[r8, :]
        vn8 = vn_ref[r8, :]
        kn_t = jnp.concatenate([kn8.astype(BF16), pad], axis=0)
        vn_t = jnp.concatenate([vn8.astype(BF16), pad], axis=0)
        scn = jnp.where(new_ok, _nt(qrows, kn_t), -jnp.inf)
        m_new = jnp.max(scn, axis=-1, keepdims=True)
        o_p = jnp.zeros((8, DSW_GW), F32)
        l_p = jnp.zeros((8, DSW_GW), F32)
        for j in range(per8):
            b = grp * per8 + j
            kt = c_ref[b, 0:DSW_GW, :].astype(BF16)
            vt = c_ref[b, DSW_GW:2 * DSW_GW, :].astype(BF16)
            sc = jnp.where(cache_ok, _nn(qrows, kt), -jnp.inf)
            m = jnp.maximum(jnp.max(sc, axis=-1, keepdims=True), m_new)
            e = jnp.exp(sc - m)
            en = jnp.exp(scn - m)
            den = jnp.sum(e, axis=-1, keepdims=True) + jnp.sum(en, axis=-1, keepdims=True)
            o = _nt((e / den).astype(BF16), vt) + _nn((en / den).astype(BF16), vn_t)
            lse = jnp.broadcast_to(m + jnp.log(den), (n_rows, DSW_GW))
            mine = (row8 // seq) == j
            o_p = jnp.where(mine, by_head(o), o_p)
            l_p = jnp.where(mine, by_head(lse), l_p)
        o_ref[r8, :] = o_p
        lse_ref[r8, :] = l_p

        hi, mid, lo = _split3(jnp.concatenate([kn8, vn8], axis=1))
        for j in range(per8):
            b = grp * per8 + j
            place = jnp.where(((p_row // seq) == j) & (lane == LANES - seq + p_row % seq), 1.0, 0.0).astype(BF16)
            new_cols = _tn(hi, place) + _tn(mid, place) + _tn(lo, place)
            for blk in range(2 * DSW_GW // LANES):
                rows = slice(blk * LANES, (blk + 1) * LANES)
                rolled = pltpu.roll(c_ref[b, rows, :], win - seq, 1)
                if win > LANES:
                    co_ref[b, rows, 0:win - LANES] = rolled[:, 0:win - LANES]
                co_ref[b, rows, win - LANES:win] = jnp.where(lane_sq < LANES - seq, rolled[:, win - LANES:win],
                                                             new_cols[rows])


def _dsw_sample_call(q32, k32, v32, cache, g, dil, seq, steps=None):
    n_seq, win = cache.shape[0], cache.shape[1]
    per8 = 8 // seq
    if steps is None:
        groups = max(1, min(8, 4 * 512 // win))
    else:
        groups = n_seq // (per8 * steps)
    n_blk = per8 * groups
    view = jnp.transpose(cache, (0, 2, 3, 4, 1)).reshape(n_seq, 2 * DSW_GW, win)
    tok = pl.BlockSpec((None, 8 * groups, DSW_GW), lambda i: (0, i, g))
    tok_out = pl.BlockSpec((None, 8 * groups, DSW_GW), lambda i: (0, i, 0))
    cspec = pl.BlockSpec((n_blk, 2 * DSW_GW, win), lambda i: (i, 0, 0))
    return dict(
        name=f"dsw_sample_d{dil}", steps=n_seq // n_blk,
        body=functools.partial(_dsw_sample_kernel, seq=seq, dil=dil), args=(q32, k32, v32, view),
        in_specs=[tok, tok, tok, cspec],
        out_specs=[tok_out, tok_out, cspec],
        out_shape=[jax.ShapeDtypeStruct((1, n_seq * seq, DSW_GW), F32)] * 2
                  + [jax.ShapeDtypeStruct(view.shape, F32)])


def _dsw_sample_finish(o, lse, new, cache_shape):
    n_seq, win = cache_shape[0], cache_shape[1]
    new = jnp.transpose(new.reshape(n_seq, 2, DSW_HEADS, DSW_HEAD_DIM, win), (0, 4, 1, 2, 3))
    n_slab = DSW_GW // LANES
    slabs = lambda a: jnp.transpose(a.reshape(1, -1, n_slab, LANES), (0, 2, 1, 3))
    return slabs(o), slabs(lse), new


def _merge_kernel(oa_ref, o0_ref, o1_ref, o2_ref, l0_ref, l1_ref, l2_ref, ga_ref, gb_ref, x_ref,
                  g1_ref, sc_ref, sh_ref, n2_ref, wpa_ref, wpb_ref, wo_ref, x1_ref, h2_ref):
    ob = []
    for slab in range(DSW_GW // LANES):
        l0, l1, l2 = l0_ref[slab], l1_ref[slab], l2_ref[slab]
        m = jnp.maximum(jnp.maximum(l0, l1), l2)
        w0, w1, w2 = jnp.exp(l0 - m), jnp.exp(l1 - m), jnp.exp(l2 - m)
        den = w0 + w1 + w2
        ob.append((w0 / den) * o0_ref[slab] + (w1 / den) * o1_ref[slab] + (w2 / den) * o2_ref[slab])
    ob = jnp.concatenate(ob, axis=1).astype(BF16)
    merged = (jax.nn.sigmoid(ga_ref[...].astype(F32)) * _nn(oa_ref[...], wpa_ref[...])
              + jax.nn.sigmoid(gb_ref[...].astype(F32)) * _nn(ob, wpb_ref[...]))
    x1 = x_ref[...] + g1_ref[...] * _nn(merged.astype(BF16), wo_ref[...])
    x1_ref[...] = x1
    h2_ref[...] = (_rms(x1, n2_ref[...]) * (1.0 + sc_ref[...]) + sh_ref[...]).astype(BF16)


def _flat_tok_spec(tm, tiles, n):
    return pl.BlockSpec((None, tm, n), lambda i: (i // tiles, i % tiles, 0))


def _flat_mod_spec(m, tm, tiles, d):
    arr, k = m
    if arr.ndim == 4:
        return pl.BlockSpec((None, None, 1, d), lambda i: (i // tiles, k, 0, 0))
    return pl.BlockSpec((tm, d), lambda i: (i % tiles, k))


def _merge_call(oa, og, lg, ga, gb, x, g1, sc2, sh2, n2, wpa, wpb, wo, tm):
    nb, t, d = x.shape
    tiles = t // tm
    tok = functools.partial(_flat_tok_spec, tm, tiles)
    mod = lambda a: _flat_mod_spec(a, tm, tiles, d)
    slab = pl.BlockSpec((None, DSW_GW // LANES, tm, LANES), lambda i: (i // tiles, 0, i % tiles, 0))
    return dict(
        name="merge_outproj", steps=nb * tiles, body=_merge_kernel,
        args=(oa, *og, *lg, ga, gb, x, g1[0], sc2[0], sh2[0], n2, wpa, wpb, wo),
        in_specs=[tok(1024)] + [slab] * 6 + [tok(d), tok(d), tok(d), mod(g1), mod(sc2), mod(sh2),
                  _resident((1, d)), _resident(wpa.shape), _resident(wpb.shape), _resident(wo.shape)],
        out_specs=[tok(d), tok(d)],
        out_shape=[jax.ShapeDtypeStruct((nb, t, d), F32), jax.ShapeDtypeStruct((nb, t, d), BF16)])


def _ffn_kernel(h_ref, x_ref, g2_ref, nf_ref, wu_ref, wd_ref, y_ref, *, final_norm, n_split):
    h = h_ref[...]
    d_ff = wd_ref.shape[0]
    step = d_ff // n_split
    acc = None
    for j in range(n_split):
        u1 = _nn(h, wu_ref[:, j * step:(j + 1) * step])
        u2 = _nn(h, wu_ref[:, d_ff + j * step:d_ff + (j + 1) * step])
        a = (u1 * jax.nn.sigmoid(u1) * u2).astype(BF16)
        part = _nn(a, wd_ref[j * step:(j + 1) * step, :])
        acc = part if acc is None else acc + part
    x2 = x_ref[...] + g2_ref[...] * acc
    y_ref[...] = _rms(x2, nf_ref[...]) if final_norm else x2


def _ffn_call(h2, x1, g2, nf, wu, wd, tm, final_norm):
    nb, t, d = x1.shape
    tiles = t // tm
    tok = functools.partial(_flat_tok_spec, tm, tiles)
    return dict(
        name="ffn", steps=nb * tiles,
        body=functools.partial(_ffn_kernel, final_norm=final_norm, n_split=wd.shape[0] // (2 * LANES)),
        args=(h2, x1, g2[0], nf, wu, wd),
        in_specs=[tok(d), tok(d), _flat_mod_spec(g2, tm, tiles, d), _resident((1, d)),
                  _resident(wu.shape), _resident(wd.shape)],
        out_specs=[tok(d)],
        out_shape=[jax.ShapeDtypeStruct((nb, t, d), F32)])


def _rope_tables(pos):
    half = DSW_HEAD_DIM // 2
    inv = ROPE_THETA ** (-np.arange(half, dtype=np.float64) / half)
    ang = np.asarray(pos, np.float64)[:, None] * inv[None, :]
    reps = LANES // half
    sign = np.tile(np.concatenate([-np.ones(half), np.ones(half)]), LANES // DSW_HEAD_DIM)
    cosf = np.tile(np.cos(ang), (1, reps))
    sins = np.tile(np.sin(ang), (1, reps)) * sign[None, :]
    return jnp.asarray(cosf, F32), jnp.asarray(sins, F32)


def _layer_weights(w_in, b_in, w_alpha2, b_alpha2):
    bf = lambda a: a.astype(BF16)
    row = lambda a: a.reshape(1, -1)
    o_glr, o_dq, o_ga = 3072, 3088, 5392
    pad_r = LANES - GLA_RANK
    wt = w_in.T
    return dict(
        wa=bf(wt[:o_glr]), ba=row(b_in[:o_glr]),
        wg=bf(jnp.pad(wt[o_glr:o_dq], ((0, pad_r), (0, 0)))), bg=row(jnp.pad(b_in[o_glr:o_dq], (0, pad_r))),
        w2=bf(jnp.pad(w_alpha2, ((0, pad_r), (0, 0)))), b2=row(b_alpha2),
        wb=bf(wt[o_dq:o_ga]), bb=row(b_in[o_dq:o_ga]),
        wc=bf(wt[o_ga:]), bc=row(b_in[o_ga:]))


def _kv_unstack(kvt, keep):
    nb, _, width = kvt.shape
    kv = kvt[:, :, width - keep:].reshape(nb, 2, DSW_HEADS, DSW_HEAD_DIM, keep)
    return jnp.transpose(kv, (0, 4, 1, 2, 3))


def kernel(x_prompt, x_sample, state_gla, cache_kv_w128, cache_kv_w512, cache_kv_w2048, c_prompt, c_sample,
           norm1_g, norm2_g, w_ada, b_ada, w_in, b_in, w_alpha2, b_alpha2, gla_norm_g, w_proj_a, w_proj_b,
           w_out, w_up, w_down, normf_g):
    depth = w_ada.shape[0]
    nb, t, d = x_prompt.shape
    n_seq, seq, _ = x_sample.shape
    assert 8 % seq == 0 and seq >= 3, "sample kernels pack whole sequences into 8-row groups"
    past = PAST_LEN
    caches = (cache_kv_w128, cache_kv_w512, cache_kv_w2048)

    cos_p, sin_p = _rope_tables(np.arange(t))
    cos_s, sin_s = _rope_tables(np.tile(past + np.arange(seq), n_seq))

    n_tok_s = n_seq * seq
    pad_c = (-(n_tok_s + nb)) % 8
    c_all = jnp.pad(jnp.concatenate([jnp.repeat(c_sample, seq, axis=0), c_prompt], axis=0), ((0, pad_c), (0, 0)))

    xp = x_prompt
    xs = x_sample.reshape(1, n_seq * seq, d)
    row = lambda a: a.reshape(1, -1)
    sp_l, kvp_l, ss_l, kvs_l = [], [], [], []
    for l in range(depth):
        w = _layer_weights(w_in[l], b_in[l], w_alpha2[l], b_alpha2[l])
        mod = _ada(c_all, w_ada[l], b_ada[l])
        mod_rows_p = mod[n_tok_s:n_tok_s + nb].reshape(nb, 6, 1, d)
        mod_p = [(mod_rows_p, k) for k in range(6)]
        mod_s = [(mod, k) for k in range(6)]
        last = l == depth - 1

        sh1_s, sc1_s, g1_s, sh2_s, sc2_s, g2_s = mod_s
        ((gq_s, gk_s, gv_s, gr_s, la_s, ga_s, gb_s, k32, v32, q32),) = _run_jobs(_inproj_call(
            xs, sc1_s, sh1_s, row(norm1_g[l]), cos_s, sin_s, w, tm=min(512, n_seq * seq), fold=False))
        sh1, sc1, g1, sh2, sc2, g2 = mod_p
        inproj_p = _inproj_call(xp, sc1, sh1, row(norm1_g[l]), cos_p, sin_p, w, tm=512, fold=True)
        late = (w_proj_a[l], w_proj_b[l], w_out[l], w_up[l], w_down[l])
        (gq, gk, gv, gr, la, ga, gb, *dsw_p), *cast = _run_jobs(
            inproj_p, *[_cast_call(a, inproj_p["steps"]) for a in late])
        w.update(zip(("wpa", "wpb", "wo", "wu", "wd"), (c[0] for c in cast)))
        qkv, kvt = dsw_p[:3 * len(DSW_GROUPS)], dsw_p[3 * len(DSW_GROUPS):]

        def with_sample_group(host, g):
            steps = host["steps"]
            fits = n_seq % ((8 // seq) * steps) == 0 and n_seq // ((8 // seq) * steps) <= 8
            job = _dsw_sample_call(q32, k32, v32, caches[g][l], g, DSW_GROUPS[g][1], seq, steps if fits else None)
            if fits:
                host_out, job_out = _run_jobs(host, job)
            else:
                (host_out,), (job_out,) = _run_jobs(host), _run_jobs(job)
            return host_out, _dsw_sample_finish(*job_out, caches[g][l].shape)

        sample_dsw = [None] * len(DSW_GROUPS)
        (oa, st), sample_dsw[2] = with_sample_group(
            _gla_prompt_call(gq, gk, gv, la, gr, row(gla_norm_g[l]), tt=256), 2)
        dsw_jobs = [_dsw_prompt_call(*qkv[3 * g:3 * g + 3], dil) for g, (_, dil) in enumerate(DSW_GROUPS)]
        gla_s_args = (gq_s, gk_s, gv_s, la_s, gr_s, row(gla_norm_g[l]), state_gla[l], seq)
        host = next((j for j in dsw_jobs if n_seq % j["steps"] == 0 and (n_seq // j["steps"]) * seq % 16 == 0), None)
        og, lg = [], []
        for job in dsw_jobs:
            if job is host:
                (o_g, l_g), (oa_s, s_new) = _run_jobs(job, _gla_sample_call(*gla_s_args, bb=n_seq // job["steps"]))
            else:
                ((o_g, l_g),) = _run_jobs(job)
            og.append(o_g)
            lg.append(l_g)
        if host is None:
            ((oa_s, s_new),) = _run_jobs(_gla_sample_call(*gla_s_args, bb=8))

        (x1, h2), sample_dsw[0] = with_sample_group(
            _merge_call(oa, og, lg, ga, gb, xp, g1, sc2, sh2, row(norm2_g[l]), w["wpa"], w["wpb"], w["wo"], tm=512), 0)
        (xp,), sample_dsw[1] = with_sample_group(
            _ffn_call(h2, x1, g2, row(normf_g), w["wu"], w["wd"], tm=512, final_norm=last), 1)
        og_s, lg_s, new_kv = zip(*sample_dsw)

        ((x1, h2),) = _run_jobs(_merge_call(oa_s, og_s, lg_s, ga_s, gb_s, xs, g1_s, sc2_s, sh2_s, row(norm2_g[l]),
                                            w["wpa"], w["wpb"], w["wo"], tm=min(512, n_seq * seq)))
        ((xs,),) = _run_jobs(_ffn_call(h2, x1, g2_s, row(normf_g), w["wu"], w["wd"], tm=min(512, n_seq * seq), final_norm=last))

        sp_l.append(jnp.swapaxes(st, 2, 3))
        kvp_l.append(tuple(_kv_unstack(kvt[g], min(win, t)) for g, (win, _) in enumerate(DSW_GROUPS)))
        ss_l.append(s_new)
        kvs_l.append(new_kv)

    y_prompt = xp
    y_sample = xs.reshape(n_seq, seq, d)
    stack = lambda items: jnp.stack(list(items))
    return (y_prompt, y_sample, stack(sp_l),
            stack(kv[0] for kv in kvp_l), stack(kv[1] for kv in kvp_l), stack(kv[2] for kv in kvp_l),
            stack(ss_l),
            stack(kv[0] for kv in kvs_l), stack(kv[1] for kv in kvs_l), stack(kv[2] for kv in kvs_l))
```

```python
import functools

import jax
import jax.numpy as jnp
import numpy as np
from jax import lax
from jax.experimental import pallas as pl
from jax.experimental.pallas import tpu as pltpu

F32 = jnp.float32
BF16 = jnp.bfloat16

EPS = 1e-6
GLA_HEADS = 4
GLA_DK = 128
GLA_DV = 256
GLA_RANK = 16
GLA_TAU = 16.0
GLA_CHUNK = 64
GLA_SCALE = GLA_DK ** -0.5
DSW_GROUPS = ((128, 1), (512, 4), (2048, 16))
DSW_HEADS = 4
DSW_HEAD_DIM = 64
DSW_SCALE = DSW_HEAD_DIM ** -0.5
DSW_GW = DSW_HEADS * DSW_HEAD_DIM
BAND = 128
ROPE_THETA = 10000.0
PAST_LEN = 8192
LANES = 128
VMEM_LIMIT = 56 * 1024 * 1024


def _nn(a, b):
    return jnp.dot(a, b, preferred_element_type=F32)


def _nt(a, b):
    return lax.dot_general(a, b, (((1,), (1,)), ((), ())), preferred_element_type=F32)


def _tn(a, b):
    return lax.dot_general(a, b, (((0,), (0,)), ((), ())), preferred_element_type=F32)


def _split3(x):
    hi = x.astype(BF16)
    r1 = x - hi.astype(F32)
    mid = r1.astype(BF16)
    lo = (r1 - mid.astype(F32)).astype(BF16)
    return hi, mid, lo


def _iota(shape, dim):
    return lax.broadcasted_iota(jnp.int32, shape, dim)


def _rms(x, g):
    return x * lax.rsqrt(jnp.mean(x * x, axis=-1, keepdims=True) + EPS) * g


def _params(n_parallel, n_arbitrary=0):
    sem = ("parallel",) * n_parallel + ("arbitrary",) * n_arbitrary
    return pltpu.CompilerParams(dimension_semantics=sem, vmem_limit_bytes=VMEM_LIMIT)


def _resident(shape):
    nd = len(shape)
    return pl.BlockSpec(shape, lambda *_: (0,) * nd, pipeline_mode=pl.Buffered(1))


def _ada_kernel(c_ref, w_ref, b_ref, o_ref):
    c = c_ref[...]
    a = (c * jax.nn.sigmoid(c)).astype(BF16)
    o_ref[...] = _nn(a, w_ref[...].astype(BF16)) + b_ref[...]


def _ada(c_all, w_ada, b_ada):
    n, d = c_all.shape
    ncol = w_ada.shape[1]
    tn = 1536
    return pl.pallas_call(
        _ada_kernel,
        grid=(ncol // tn,),
        in_specs=[pl.BlockSpec((n, d), lambda j: (0, 0)),
                  pl.BlockSpec((d, tn), lambda j: (0, j)),
                  pl.BlockSpec((1, tn), lambda j: (0, j))],
        out_specs=pl.BlockSpec((n, tn), lambda j: (0, j)),
        out_shape=jax.ShapeDtypeStruct((n, ncol), F32),
        compiler_params=_params(1),
        name="ada_mod",
    )(c_all, w_ada, b_ada.reshape(1, ncol))


def _rope(x, cosf, sins, first_half):
    rot = jnp.where(first_half, pltpu.roll(x, LANES - 32, 1), pltpu.roll(x, 32, 1))
    return x * cosf + rot * sins


def _inproj_kernel(x_ref, sc_ref, sh_ref, g_ref, cos_ref, sin_ref,
                   wa_ref, ba_ref, wg_ref, bg_ref, w2_ref, b2_ref, wb_ref, bb_ref, wc_ref, bc_ref,
                   gq_ref, gk_ref, gv_ref, gr_ref, la_ref, ga_ref, gb_ref, *rest, fold):
    x = x_ref[...]
    h = (_rms(x, g_ref[...]) * (1.0 + sc_ref[...]) + sh_ref[...]).astype(BF16)

    cosf = cos_ref[...]
    sins = sin_ref[...]
    first_half = (_iota(cosf.shape, 1) % DSW_HEAD_DIM) < (DSW_HEAD_DIM // 2)
    tm = x.shape[0]
    per_group = DSW_GW // LANES
    width = 3 * DSW_GW

    def proj(off, g):
        cols = slice(off + g * DSW_GW, off + (g + 1) * DSW_GW)
        full = _nt(h, wb_ref[cols, :]) + bb_ref[:, cols]
        return [full[:, s * LANES:(s + 1) * LANES] for s in range(per_group)]

    for g, (_, dil) in enumerate(DSW_GROUPS):
        qs = [_rope(a, cosf, sins, first_half) * DSW_SCALE for a in proj(0, g)]
        ks = [_rope(a, cosf, sins, first_half) for a in proj(width, g)]
        vs = proj(2 * width, g)
        if fold:
            kvt_ref = rest[3 * len(DSW_GROUPS) + g]
            kvt_ref[0:DSW_GW, :] = jnp.concatenate(ks, axis=1).T
            kvt_ref[DSW_GW:2 * DSW_GW, :] = jnp.concatenate(vs, axis=1).T

        for slab in range(per_group):
            cols = slice(g * DSW_GW + slab * LANES, g * DSW_GW + (slab + 1) * LANES)
            lanes = slice(slab * LANES, (slab + 1) * LANES)
            if not fold:
                rest[0][:, cols] = ks[slab]
                rest[1][:, cols] = vs[slab]
                rest[2][:, cols] = qs[slab]
                continue
            scratch = rest[-1]
            for which, val in enumerate((qs[slab], ks[slab], vs[slab])):
                out_ref = rest[3 * g + which]
                if dil == 1:
                    out_ref[:, lanes] = val.astype(BF16)
                else:
                    scratch[which] = val
                    for r in range(dil):
                        out_ref[r, :, lanes] = scratch[which, pl.ds(r, tm // dil, stride=dil), :].astype(BF16)

    glr = (_nt(h, wg_ref[...]) + bg_ref[...]).astype(BF16)
    z = _nn(glr, w2_ref[...]) + b2_ref[...]
    la_ref[...] = jax.nn.log_sigmoid(z) * (1.0 / GLA_TAU)

    gq_ref[...] = (_nt(h, wa_ref[0:512, :]) + ba_ref[:, 0:512]).astype(gq_ref.dtype)
    gk_ref[...] = (_nt(h, wa_ref[512:1024, :]) + ba_ref[:, 512:1024]).astype(gk_ref.dtype)
    gv_ref[...] = (_nt(h, wa_ref[1024:2048, :]) + ba_ref[:, 1024:2048]).astype(gv_ref.dtype)
    gr_ref[...] = (_nt(h, wa_ref[2048:3072, :]) + ba_ref[:, 2048:3072]).astype(gr_ref.dtype)

    ga_ref[...] = (_nt(h, wc_ref[0:1024, :]) + bc_ref[:, 0:1024]).astype(ga_ref.dtype)
    gb_ref[...] = (_nt(h, wc_ref[1024:2048, :]) + bc_ref[:, 1024:2048]).astype(gb_ref.dtype)


def _inproj_call(x, sc, sh, g, cos_t, sin_t, w, tm, fold):
    nb, t, d = x.shape
    tiles = t // tm
    tok = functools.partial(_flat_tok_spec, tm, tiles)
    mod = lambda a: _flat_mod_spec(a, tm, tiles, d)
    out_cols = (512, 512, 1024, 1024, 512, 1024, 1024)
    out_dt = (BF16, BF16, BF16, BF16, F32, BF16, BF16)
    out_specs = [tok(n) for n in out_cols]
    out_shape = [jax.ShapeDtypeStruct((nb, t, n), dt) for n, dt in zip(out_cols, out_dt)]
    scratch = []
    if fold:
        for _, dil in DSW_GROUPS:
            for _ in range(3):
                if dil == 1:
                    out_specs.append(tok(DSW_GW))
                    out_shape.append(jax.ShapeDtypeStruct((nb, t, DSW_GW), BF16))
                else:
                    out_specs.append(pl.BlockSpec((None, dil, tm // dil, DSW_GW),
                                                  lambda i: (i // tiles, 0, i % tiles, 0)))
                    out_shape.append(jax.ShapeDtypeStruct((nb, dil, t // dil, DSW_GW), BF16))
        for win, _ in DSW_GROUPS:
            width = min(max(win, tm), t)
            first = (t - width) // tm
            out_specs.append(pl.BlockSpec((None, 2 * DSW_GW, tm),
                                          lambda i, first=first: (i // tiles, 0, jnp.maximum(i % tiles - first, 0))))
            out_shape.append(jax.ShapeDtypeStruct((nb, 2 * DSW_GW, width), F32))
        scratch = [pltpu.VMEM((3, tm, LANES), F32)]
    else:
        for _ in range(3):
            out_specs.append(tok(3 * DSW_GW))
            out_shape.append(jax.ShapeDtypeStruct((nb, t, 3 * DSW_GW), F32))
    weights = (w["wa"], w["ba"], w["wg"], w["bg"], w["w2"], w["b2"], w["wb"], w["bb"], w["wc"], w["bc"])
    table = pl.BlockSpec((tm, LANES), lambda i: (i % tiles, 0))
    return dict(
        name="inproj", steps=nb * tiles, body=functools.partial(_inproj_kernel, fold=fold),
        args=(x, sc[0], sh[0], g, cos_t, sin_t, *weights),
        in_specs=[tok(d), mod(sc), mod(sh), _resident((1, d)), table, table] + [_resident(a.shape) for a in weights],
        out_specs=out_specs, out_shape=out_shape, scratch=scratch)


def _cast_kernel(x_ref, o_ref):
    o_ref[...] = x_ref[...].astype(o_ref.dtype)


def _cast_call(w, steps):
    rows, cols = w.shape
    blk = next(b for b in range(16, rows + 1, 16) if rows % b == 0 and b * steps >= rows)
    n_blk = rows // blk
    spec = pl.BlockSpec((blk, cols), lambda i: (jnp.minimum(i, n_blk - 1), 0))
    return dict(name="cast", steps=steps, body=_cast_kernel, args=(w,), in_specs=[spec], out_specs=[spec],
                out_shape=[jax.ShapeDtypeStruct(w.shape, BF16)])


def _gla_local(gq_ref, gk_ref, la_ref, chunk):
    la = la_ref[...]
    tt = la.shape[0]
    r = _iota((tt, tt), 0)
    c = _iota((tt, tt), 1)
    same = (r // chunk) == (c // chunk)
    tri = jnp.where(same & (c <= r), 1.0, 0.0).astype(BF16)
    hi, mid, lo = _split3(la)
    b = _nn(tri, hi) + _nn(tri, mid) + _nn(tri, lo)
    if chunk % 8 == 0:
        bl = jnp.concatenate([jnp.broadcast_to(b[e - 1:e, :], (chunk, b.shape[1]))
                              for e in range(chunk, tt + 1, chunk)], axis=0)
    else:
        ones = jnp.where(same, 1.0, 0.0).astype(BF16)
        bl = _nn(ones, hi) + _nn(ones, mid) + _nn(ones, lo)
    gq = gq_ref[...].astype(F32)
    gk = gk_ref[...].astype(F32)
    qg = (gq * GLA_SCALE * jnp.exp(b)).astype(BF16)
    kd = (gk * jnp.exp(-b)).astype(BF16)
    kl = (gk * jnp.exp(bl - b)).astype(BF16)
    causal = same & (c <= r)
    return qg, kd, kl, jnp.exp(bl), causal


def _gla_finish(o, gr, g):
    return (_rms(o, g) * (gr * jax.nn.sigmoid(gr))).astype(BF16)


def _gla_prompt_body(first_tile, gq_ref, gk_ref, gv_ref, la_ref, gr_ref, g_ref, o_ref, st_ref):
    @pl.when(first_tile)
    def _():
        st_ref[...] = jnp.zeros_like(st_ref)

    qg, kd, kl, dec, causal = _gla_local(gq_ref, gk_ref, la_ref, GLA_CHUNK)
    tt = qg.shape[0]
    chunk_of_row = _iota((tt, GLA_DK), 0) // GLA_CHUNK
    for h in range(GLA_HEADS):
        kc = slice(h * GLA_DK, (h + 1) * GLA_DK)
        vc = slice(h * GLA_DV, (h + 1) * GLA_DV)
        v = gv_ref[:, vc].astype(BF16)
        att = jnp.where(causal, _nt(qg[:, kc], kd[:, kc]), 0.0).astype(BF16)
        intra = _nn(att, v)
        st = st_ref[h]
        n_chunks = tt // GLA_CHUNK
        kl_h = kl[:, kc]
        kl_bd = jnp.concatenate([jnp.where(chunk_of_row == ci, kl_h, jnp.zeros_like(kl_h))
                                 for ci in range(n_chunks)], axis=1)
        upd = _tn(v, kl_bd)
        inter = []
        for ci in range(n_chunks):
            rows = slice(ci * GLA_CHUNK, (ci + 1) * GLA_CHUNK)
            inter.append(_nt(qg[rows, kc], st.astype(BF16)))
            st = dec[ci * GLA_CHUNK:ci * GLA_CHUNK + 1, kc] * st + upd[:, ci * GLA_DK:(ci + 1) * GLA_DK]
        st_ref[h] = st
        o = intra + jnp.concatenate(inter, axis=0)
        o_ref[:, vc] = _gla_finish(o, gr_ref[:, vc].astype(F32), g_ref[...])


def _gla_prompt_call(gq, gk, gv, la, gr, g, tt):
    nb, t, _ = gq.shape
    tiles = t // tt
    tok = lambda n: pl.BlockSpec((None, tt, n), lambda i: (i // tiles, i % tiles, 0))
    def body(*refs):
        _gla_prompt_body(pl.program_id(0) % tiles == 0, *refs)

    return dict(
        name="gla_prompt", steps=nb * tiles, body=body, args=(gq, gk, gv, la, gr, g),
        in_specs=[tok(512), tok(512), tok(1024), tok(512), tok(1024), _resident((1, GLA_DV))],
        out_specs=[tok(1024),
                   pl.BlockSpec((None, GLA_HEADS, GLA_DV, GLA_DK), lambda i: (i // tiles, 0, 0, 0))],
        out_shape=[jax.ShapeDtypeStruct((nb, t, 1024), BF16),
                   jax.ShapeDtypeStruct((nb, GLA_HEADS, GLA_DV, GLA_DK), F32)])


def _run_jobs(*jobs):
    steps = jobs[0]["steps"]
    assert all(j["steps"] == steps for j in jobs)
    n_in = [len(j["in_specs"]) for j in jobs]
    n_out = [len(j["out_specs"]) for j in jobs]
    n_scr = [len(j.get("scratch", ())) for j in jobs]

    def kern(*refs):
        i_pos, o_pos, s_pos = 0, sum(n_in), sum(n_in) + sum(n_out)
        for j, ni, no, ns in zip(jobs, n_in, n_out, n_scr):
            j["body"](*refs[i_pos:i_pos + ni], *refs[o_pos:o_pos + no], *refs[s_pos:s_pos + ns])
            i_pos, o_pos, s_pos = i_pos + ni, o_pos + no, s_pos + ns

    outs = pl.pallas_call(
        kern,
        grid=(steps,),
        in_specs=[s for j in jobs for s in j["in_specs"]],
        out_specs=[s for j in jobs for s in j["out_specs"]],
        out_shape=[s for j in jobs for s in j["out_shape"]],
        scratch_shapes=[s for j in jobs for s in j.get("scratch", ())],
        compiler_params=_params(0, 1),
        name="__".join(j["name"] for j in jobs),
    )(*[a for j in jobs for a in j["args"]])
    split, pos = [], 0
    for no in n_out:
        split.append(list(outs[pos:pos + no]))
        pos += no
    return split


def _gla_sample_kernel(gq_ref, gk_ref, gv_ref, la_ref, gr_ref, g_ref, s_ref, o_ref, so_ref, *, seq):
    qg, kd, kl, dec, causal = _gla_local(gq_ref, gk_ref, la_ref, seq)
    rows_total = qg.shape[0]
    per8 = 8 // seq
    row8 = _iota((8, 1), 0)
    for h in range(GLA_HEADS):
        kc = slice(h * GLA_DK, (h + 1) * GLA_DK)
        vc = slice(h * GLA_DV, (h + 1) * GLA_DV)
        v = gv_ref[:, vc].astype(BF16)
        att = jnp.where(causal, _nt(qg[:, kc], kd[:, kc]), 0.0).astype(BF16)
        intra = _nn(att, v)
        inter = []
        for p in range(rows_total // 8):
            rows = slice(p * 8, (p + 1) * 8)
            d_hi, d_mid, d_lo = _split3(dec[rows, kc])
            inter_p = jnp.zeros((8, GLA_DV), F32)
            for j in range(per8):
                b = p * per8 + j
                r0 = j * seq
                s0 = s_ref[b, h]
                mine = (row8 >= r0) & (row8 < r0 + seq)
                inter_p = jnp.where(mine, _nn(qg[rows, kc], s0.astype(BF16)), inter_p)
                dl = jnp.where(row8 == r0, d_hi, jnp.where(row8 == r0 + 1, d_mid,
                               jnp.where(row8 == r0 + 2, d_lo, jnp.zeros_like(d_lo))))
                e = jnp.where((row8 >= r0) & (row8 < r0 + 3), 1.0, 0.0).astype(BF16)
                dec_b = _tn(dl, jnp.broadcast_to(e, (8, GLA_DV)))
                upd = _tn(jnp.where(mine, kl[rows, kc], jnp.zeros_like(kl[rows, kc])), v[rows])
                so_ref[b, h] = dec_b * s0 + upd
            inter.append(inter_p)
        o = intra + jnp.concatenate(inter, axis=0)
        o_ref[:, vc] = _gla_finish(o, gr_ref[:, vc].astype(F32), g_ref[...])


def _gla_sample_call(gq, gk, gv, la, gr, g, s0, seq, bb):
    n_seq = s0.shape[0]
    rows = bb * seq
    tok = lambda n: pl.BlockSpec((None, rows, n), lambda i: (0, i, 0))
    st = pl.BlockSpec((bb, GLA_HEADS, GLA_DK, GLA_DV), lambda i: (i, 0, 0, 0))
    return dict(
        name="gla_sample", steps=n_seq // bb, body=functools.partial(_gla_sample_kernel, seq=seq),
        args=(gq, gk, gv, la, gr, g, s0),
        in_specs=[tok(512), tok(512), tok(1024), tok(512), tok(1024), _resident((1, GLA_DV)), st],
        out_specs=[tok(1024), st],
        out_shape=[jax.ShapeDtypeStruct((1, n_seq * seq, 1024), BF16),
                   jax.ShapeDtypeStruct(s0.shape, F32)])


def _dsw_prompt_kernel(q_ref, kp_ref, kc_ref, vp_ref, vc_ref, o_ref, lse_ref, *, dil, tile, res):
    qb = q_ref.shape[0]
    first_key = jnp.where(tile == 0, BAND, 0)
    qi = _iota((BAND, 2 * BAND), 0) + BAND
    ki = _iota((BAND, 2 * BAND), 1)
    band = (qi - ki >= 0) & (qi - ki <= BAND)
    lane = _iota((BAND, LANES), 1)
    for s in range(qb // BAND):
        rows = slice(s * BAND, (s + 1) * BAND)
        if s == 0:
            valid = band & (ki >= first_key)
        else:
            valid = band
        if dil == 1:
            tok_rows = rows
        else:
            tok_rows = pl.ds(s * BAND * dil + res, BAND, stride=dil)
        for hp in range(DSW_GW // LANES):
            cols = slice(hp * LANES, (hp + 1) * LANES)
            qp = q_ref[rows, cols]
            if s == 0:
                kcat = jnp.concatenate([kp_ref[:, cols], kc_ref[0:BAND, cols]], axis=0)
                vcat = jnp.concatenate([vp_ref[:, cols], vc_ref[0:BAND, cols]], axis=0)
            else:
                kcat = kc_ref[(s - 1) * BAND:(s + 1) * BAND, cols]
                vcat = vc_ref[(s - 1) * BAND:(s + 1) * BAND, cols]
            outs, lses = [], []
            for hh in range(LANES // DSW_HEAD_DIM):
                in_head = (lane // DSW_HEAD_DIM) == hh
                sc = _nt(jnp.where(in_head, qp, jnp.zeros_like(qp)), kcat)
                sc = jnp.where(valid, sc, -jnp.inf)
                m = jnp.max(sc, axis=-1, keepdims=True)
                e = jnp.exp(sc - m)
                den = jnp.sum(e, axis=-1, keepdims=True)
                outs.append(_nn((e / den).astype(BF16), vcat))
                lses.append(m + jnp.log(den))
            first = lane < DSW_HEAD_DIM
            o_ref[hp, tok_rows, :] = jnp.where(first, outs[0], outs[1])
            lse_ref[hp, tok_rows, :] = jnp.where(first, lses[0], jnp.broadcast_to(lses[1], (BAND, LANES)))


def _dsw_prompt_call(q, k, v, dil):
    nb = q.shape[0]
    seq_len = q.shape[-2]
    t = seq_len * dil
    tq = min(1024, seq_len)
    sub = tq // BAND
    n_tiles = seq_len // tq
    rpb = max(1, min(dil, 1024 // tq))
    assert dil % rpb == 0
    n_res = dil // rpb
    bat = lambda i: i // (n_tiles * n_res)
    til = lambda i: (i // n_res) % n_tiles
    grp = lambda i: i % n_res
    prev_blk = lambda i: jnp.maximum(til(i) * sub - 1, 0)
    if dil == 1:
        cur = pl.BlockSpec((None, tq, DSW_GW), lambda i: (bat(i), til(i), 0))
        prev = pl.BlockSpec((None, BAND, DSW_GW), lambda i: (bat(i), prev_blk(i), 0))
    else:
        cur = pl.BlockSpec((None, rpb, tq, DSW_GW), lambda i: (bat(i), grp(i), til(i), 0))
        prev = pl.BlockSpec((None, rpb, BAND, DSW_GW), lambda i: (bat(i), grp(i), prev_blk(i), 0))
    n_slab = DSW_GW // LANES
    out = pl.BlockSpec((None, n_slab, tq * dil, LANES), lambda i: (bat(i), 0, til(i), 0))

    def body(*refs):
        i = pl.program_id(0)
        if dil == 1:
            _dsw_prompt_kernel(*refs, dil=dil, tile=til(i), res=0)
        else:
            for rr in range(rpb):
                _dsw_prompt_kernel(*[r.at[rr] for r in refs[:5]], *refs[5:], dil=dil, tile=til(i),
                                   res=grp(i) * rpb + rr)

    return dict(
        name=f"dsw_prompt_d{dil}", steps=nb * n_tiles * n_res, body=body, args=(q, k, k, v, v),
        in_specs=[cur, prev, cur, prev, cur],
        out_specs=[out, out],
        out_shape=[jax.ShapeDtypeStruct((nb, n_slab, t, LANES), F32)] * 2)


def _dsw_sample_kernel(q_ref, kn_ref, vn_ref, c_ref, o_ref, lse_ref, co_ref, *, seq, dil):
    per8 = 8 // seq
    win = c_ref.shape[2]
    n_rows = DSW_HEADS * 8
    lane = _iota((8, LANES), 1)
    head_of_lane = _iota((8, DSW_GW), 1) // DSW_HEAD_DIM
    row8 = _iota((8, 1), 0)
    r = _iota((n_rows, 1), 0)
    r_step = r % seq
    r_seq = (r % 8) // seq
    key = _iota((n_rows, win), 1)
    cache_ok = ((key % dil) == (r_step % dil)) & (key >= r_step)
    c128 = _iota((n_rows, LANES), 1)
    new_ok = ((c128 < 8) & ((c128 // seq) == r_seq) & ((c128 % seq) <= r_step)
              & (((r_step - c128 % seq) % dil) == 0))
    pad = jnp.zeros((LANES - 8, DSW_GW), BF16)

    def by_head(x):
        out = x[(DSW_HEADS - 1) * 8:DSW_HEADS * 8]
        for h in range(DSW_HEADS - 2, -1, -1):
            out = jnp.where(head_of_lane == h, x[h * 8:(h + 1) * 8], out)
        return out

    lane_sq = _iota((LANES, LANES), 1)
    p_row = _iota((8, LANES), 0)
    for grp in range(q_ref.shape[0] // 8):
        r8 = slice(grp * 8, (grp + 1) * 8)
        q8 = q_ref[r8, :]
        qrows = jnp.concatenate([jnp.where(head_of_lane == h, q8, 0.0) for h in range(DSW_HEADS)],
                                axis=0).astype(BF16)
        kn8 = kn_ref[r8, :]
        vn8 = vn_ref[r8, :]
        kn_t = jnp.concatenate([kn8.astype(BF16), pad], axis=0)
        vn_t = jnp.concatenate([vn8.astype(BF16), pad], axis=0)
        scn = jnp.where(new_ok, _nt(qrows, kn_t), -jnp.inf)
        m_new = jnp.max(scn, axis=-1, keepdims=True)
        o_p = jnp.zeros((8, DSW_GW), F32)
        l_p = jnp.zeros((8, DSW_GW), F32)
        for j in range(per8):
            b = grp * per8 + j
            kt = c_ref[b, 0:DSW_GW, :].astype(BF16)
            vt = c_ref[b, DSW_GW:2 * DSW_GW, :].astype(BF16)
            sc = jnp.where(cache_ok, _nn(qrows, kt), -jnp.inf)
            m = jnp.maximum(jnp.max(sc, axis=-1, keepdims=True), m_new)
            e = jnp.exp(sc - m)
            en = jnp.exp(scn - m)
            den = jnp.sum(e, axis=-1, keepdims=True) + jnp.sum(en, axis=-1, keepdims=True)
            o = _nt((e / den).astype(BF16), vt) + _nn((en / den).astype(BF16), vn_t)
            lse = jnp.broadcast_to(m + jnp.log(den), (n_rows, DSW_GW))
            mine = (row8 // seq) == j
            o_p = jnp.where(mine, by_head(o), o_p)
            l_p = jnp.where(mine, by_head(lse), l_p)
        o_ref[r8, :] = o_p
        lse_ref[r8, :] = l_p

        hi, mid, lo = _split3(jnp.concatenate([kn8, vn8], axis=1))
        for j in range(per8):
            b = grp * per8 + j
            place = jnp.where(((p_row // seq) == j) & (lane == LANES - seq + p_row % seq), 1.0, 0.0).astype(BF16)
            new_cols = _tn(hi, place) + _tn(mid, place) + _tn(lo, place)
            for blk in range(2 * DSW_GW // LANES):
                rows = slice(blk * LANES, (blk + 1) * LANES)
                rolled = pltpu.roll(c_ref[b, rows, :], win - seq, 1)
                if win > LANES:
                    co_ref[b, rows, 0:win - LANES] = rolled[:, 0:win - LANES]
                co_ref[b, rows, win - LANES:win] = jnp.where(lane_sq < LANES - seq, rolled[:, win - LANES:win],
                                                             new_cols[rows])


def _dsw_sample_call(q32, k32, v32, cache, g, dil, seq, steps=None):
    n_seq, win = cache.shape[0], cache.shape[1]
    per8 = 8 // seq
    if steps is None:
        groups = max(1, min(8, 4 * 512 // win))
    else:
        groups = n_seq // (per8 * steps)
    n_blk = per8 * groups
    view = jnp.transpose(cache, (0, 2, 3, 4, 1)).reshape(n_seq, 2 * DSW_GW, win)
    tok = pl.BlockSpec((None, 8 * groups, DSW_GW), lambda i: (0, i, g))
    tok_out = pl.BlockSpec((None, 8 * groups, DSW_GW), lambda i: (0, i, 0))
    cspec = pl.BlockSpec((n_blk, 2 * DSW_GW, win), lambda i: (i, 0, 0))
    return dict(
        name=f"dsw_sample_d{dil}", steps=n_seq // n_blk,
        body=functools.partial(_dsw_sample_kernel, seq=seq, dil=dil), args=(q32, k32, v32, view),
        in_specs=[tok, tok, tok, cspec],
        out_specs=[tok_out, tok_out, cspec],
        out_shape=[jax.ShapeDtypeStruct((1, n_seq * seq, DSW_GW), F32)] * 2
                  + [jax.ShapeDtypeStruct(view.shape, F32)])


def _dsw_sample_finish(o, lse, new, cache_shape):
    n_seq, win = cache_shape[0], cache_shape[1]
    new = jnp.transpose(new.reshape(n_seq, 2, DSW_HEADS, DSW_HEAD_DIM, win), (0, 4, 1, 2, 3))
    n_slab = DSW_GW // LANES
    slabs = lambda a: jnp.transpose(a.reshape(1, -1, n_slab, LANES), (0, 2, 1, 3))
    return slabs(o), slabs(lse), new


def _merge_kernel(oa_ref, o0_ref, o1_ref, o2_ref, l0_ref, l1_ref, l2_ref, ga_ref, gb_ref, x_ref,
                  g1_ref, sc_ref, sh_ref, n2_ref, wpa_ref, wpb_ref, wo_ref, x1_ref, h2_ref):
    ob = []
    for slab in range(DSW_GW // LANES):
        l0, l1, l2 = l0_ref[slab], l1_ref[slab], l2_ref[slab]
        m = jnp.maximum(jnp.maximum(l0, l1), l2)
        w0, w1, w2 = jnp.exp(l0 - m), jnp.exp(l1 - m), jnp.exp(l2 - m)
        den = w0 + w1 + w2
        ob.append((w0 / den) * o0_ref[slab] + (w1 / den) * o1_ref[slab] + (w2 / den) * o2_ref[slab])
    ob = jnp.concatenate(ob, axis=1).astype(BF16)
    merged = (jax.nn.sigmoid(ga_ref[...].astype(F32)) * _nn(oa_ref[...], wpa_ref[...])
              + jax.nn.sigmoid(gb_ref[...].astype(F32)) * _nn(ob, wpb_ref[...]))
    x1 = x_ref[...] + g1_ref[...] * _nn(merged.astype(BF16), wo_ref[...])
    x1_ref[...] = x1
    h2_ref[...] = (_rms(x1, n2_ref[...]) * (1.0 + sc_ref[...]) + sh_ref[...]).astype(BF16)


def _flat_tok_spec(tm, tiles, n):
    return pl.BlockSpec((None, tm, n), lambda i: (i // tiles, i % tiles, 0))


def _flat_mod_spec(m, tm, tiles, d):
    arr, k = m
    if arr.ndim == 4:
        return pl.BlockSpec((None, None, 1, d), lambda i: (i // tiles, k, 0, 0))
    return pl.BlockSpec((tm, d), lambda i: (i % tiles, k))


def _merge_call(oa, og, lg, ga, gb, x, g1, sc2, sh2, n2, wpa, wpb, wo, tm):
    nb, t, d = x.shape
    tiles = t // tm
    tok = functools.partial(_flat_tok_spec, tm, tiles)
    mod = lambda a: _flat_mod_spec(a, tm, tiles, d)
    slab = pl.BlockSpec((None, DSW_GW // LANES, tm, LANES), lambda i: (i // tiles, 0, i % tiles, 0))
    return dict(
        name="merge_outproj", steps=nb * tiles, body=_merge_kernel,
        args=(oa, *og, *lg, ga, gb, x, g1[0], sc2[0], sh2[0], n2, wpa, wpb, wo),
        in_specs=[tok(1024)] + [slab] * 6 + [tok(d), tok(d), tok(d), mod(g1), mod(sc2), mod(sh2),
                  _resident((1, d)), _resident(wpa.shape), _resident(wpb.shape), _resident(wo.shape)],
        out_specs=[tok(d), tok(d)],
        out_shape=[jax.ShapeDtypeStruct((nb, t, d), F32), jax.ShapeDtypeStruct((nb, t, d), BF16)])


def _ffn_kernel(h_ref, x_ref, g2_ref, nf_ref, wu_ref, wd_ref, y_ref, *, final_norm, n_split):
    h = h_ref[...]
    d_ff = wd_ref.shape[0]
    step = d_ff // n_split
    acc = None
    for j in range(n_split):
        u1 = _nn(h, wu_ref[:, j * step:(j + 1) * step])
        u2 = _nn(h, wu_ref[:, d_ff + j * step:d_ff + (j + 1) * step])
        a = (u1 * jax.nn.sigmoid(u1) * u2).astype(BF16)
        part = _nn(a, wd_ref[j * step:(j + 1) * step, :])
        acc = part if acc is None else acc + part
    x2 = x_ref[...] + g2_ref[...] * acc
    y_ref[...] = _rms(x2, nf_ref[...]) if final_norm else x2


def _ffn_call(h2, x1, g2, nf, wu, wd, tm, final_norm):
    nb, t, d = x1.shape
    tiles = t // tm
    tok = functools.partial(_flat_tok_spec, tm, tiles)
    return dict(
        name="ffn", steps=nb * tiles,
        body=functools.partial(_ffn_kernel, final_norm=final_norm, n_split=wd.shape[0] // (2 * LANES)),
        args=(h2, x1, g2[0], nf, wu, wd),
        in_specs=[tok(d), tok(d), _flat_mod_spec(g2, tm, tiles, d), _resident((1, d)),
                  _resident(wu.shape), _resident(wd.shape)],
        out_specs=[tok(d)],
        out_shape=[jax.ShapeDtypeStruct((nb, t, d), F32)])


def _rope_tables(pos):
    half = DSW_HEAD_DIM // 2
    inv = ROPE_THETA ** (-np.arange(half, dtype=np.float64) / half)
    ang = np.asarray(pos, np.float64)[:, None] * inv[None, :]
    reps = LANES // half
    sign = np.tile(np.concatenate([-np.ones(half), np.ones(half)]), LANES // DSW_HEAD_DIM)
    cosf = np.tile(np.cos(ang), (1, reps))
    sins = np.tile(np.sin(ang), (1, reps)) * sign[None, :]
    return jnp.asarray(cosf, F32), jnp.asarray(sins, F32)


def _layer_weights(w_in, b_in, w_alpha2, b_alpha2):
    bf = lambda a: a.astype(BF16)
    row = lambda a: a.reshape(1, -1)
    o_glr, o_dq, o_ga = 3072, 3088, 5392
    pad_r = LANES - GLA_RANK
    wt = w_in.T
    return dict(
        wa=bf(wt[:o_glr]), ba=row(b_in[:o_glr]),
        wg=bf(jnp.pad(wt[o_glr:o_dq], ((0, pad_r), (0, 0)))), bg=row(jnp.pad(b_in[o_glr:o_dq], (0, pad_r))),
        w2=bf(jnp.pad(w_alpha2, ((0, pad_r), (0, 0)))), b2=row(b_alpha2),
        wb=bf(wt[o_dq:o_ga]), bb=row(b_in[o_dq:o_ga]),
        wc=bf(wt[o_ga:]), bc=row(b_in[o_ga:]))


def _kv_unstack(kvt, keep):
    nb, _, width = kvt.shape
    kv = kvt[:, :, width - keep:].reshape(nb, 2, DSW_HEADS, DSW_HEAD_DIM, keep)
    return jnp.transpose(kv, (0, 4, 1, 2, 3))


def kernel(x_prompt, x_sample, state_gla, cache_kv_w128, cache_kv_w512, cache_kv_w2048, c_prompt, c_sample,
           norm1_g, norm2_g, w_ada, b_ada, w_in, b_in, w_alpha2, b_alpha2, gla_norm_g, w_proj_a, w_proj_b,
           w_out, w_up, w_down, normf_g):
    depth = w_ada.shape[0]
    nb, t, d = x_prompt.shape
    n_seq, seq, _ = x_sample.shape
    assert 8 % seq == 0 and seq >= 3, "sample kernels pack whole sequences into 8-row groups"
    past = PAST_LEN
    caches = (cache_kv_w128, cache_kv_w512, cache_kv_w2048)

    cos_p, sin_p = _rope_tables(np.arange(t))
    cos_s, sin_s = _rope_tables(np.tile(past + np.arange(seq), n_seq))

    n_tok_s = n_seq * seq
    pad_c = (-(n_tok_s + nb)) % 8
    c_all = jnp.pad(jnp.concatenate([jnp.repeat(c_sample, seq, axis=0), c_prompt], axis=0), ((0, pad_c), (0, 0)))

    xp = x_prompt
    xs = x_sample.reshape(1, n_seq * seq, d)
    row = lambda a: a.reshape(1, -1)
    sp_l, kvp_l, ss_l, kvs_l = [], [], [], []
    for l in range(depth):
        w = _layer_weights(w_in[l], b_in[l], w_alpha2[l], b_alpha2[l])
        mod = _ada(c_all, w_ada[l], b_ada[l])
        mod_rows_p = mod[n_tok_s:n_tok_s + nb].reshape(nb, 6, 1, d)
        mod_p = [(mod_rows_p, k) for k in range(6)]
        mod_s = [(mod, k) for k in range(6)]
        last = l == depth - 1

        sh1_s, sc1_s, g1_s, sh2_s, sc2_s, g2_s = mod_s
        ((gq_s, gk_s, gv_s, gr_s, la_s, ga_s, gb_s, k32, v32, q32),) = _run_jobs(_inproj_call(
            xs, sc1_s, sh1_s, row(norm1_g[l]), cos_s, sin_s, w, tm=min(512, n_seq * seq), fold=False))
        sh1, sc1, g1, sh2, sc2, g2 = mod_p
        inproj_p = _inproj_call(xp, sc1, sh1, row(norm1_g[l]), cos_p, sin_p, w, tm=512, fold=True)
        late = (w_proj_a[l], w_proj_b[l], w_out[l], w_up[l], w_down[l])
        (gq, gk, gv, gr, la, ga, gb, *dsw_p), *cast = _run_jobs(
            inproj_p, *[_cast_call(a, inproj_p["steps"]) for a in late])
        w.update(zip(("wpa", "wpb", "wo", "wu", "wd"), (c[0] for c in cast)))
        qkv, kvt = dsw_p[:3 * len(DSW_GROUPS)], dsw_p[3 * len(DSW_GROUPS):]

        def with_sample_group(host, g):
            steps = host["steps"]
            fits = n_seq % ((8 // seq) * steps) == 0 and n_seq // ((8 // seq) * steps) <= 8
            job = _dsw_sample_call(q32, k32, v32, caches[g][l], g, DSW_GROUPS[g][1], seq, steps if fits else None)
            if fits:
                host_out, job_out = _run_jobs(host, job)
            else:
                (host_out,), (job_out,) = _run_jobs(host), _run_jobs(job)
            return host_out, _dsw_sample_finish(*job_out, caches[g][l].shape)

        sample_dsw = [None] * len(DSW_GROUPS)
        (oa, st), sample_dsw[2] = with_sample_group(
            _gla_prompt_call(gq, gk, gv, la, gr, row(gla_norm_g[l]), tt=256), 2)
        dsw_jobs = [_dsw_prompt_call(*qkv[3 * g:3 * g + 3], dil) for g, (_, dil) in enumerate(DSW_GROUPS)]
        gla_s_args = (gq_s, gk_s, gv_s, la_s, gr_s, row(gla_norm_g[l]), state_gla[l], seq)
        host = next((j for j in dsw_jobs if n_seq % j["steps"] == 0 and (n_seq // j["steps"]) * seq % 16 == 0), None)
        og, lg = [], []
        for job in dsw_jobs:
            if job is host:
                (o_g, l_g), (oa_s, s_new) = _run_jobs(job, _gla_sample_call(*gla_s_args, bb=n_seq // job["steps"]))
            else:
                ((o_g, l_g),) = _run_jobs(job)
            og.append(o_g)
            lg.append(l_g)
        if host is None:
            ((oa_s, s_new),) = _run_jobs(_gla_sample_call(*gla_s_args, bb=8))

        (x1, h2), sample_dsw[0] = with_sample_group(
            _merge_call(oa, og, lg, ga, gb, xp, g1, sc2, sh2, row(norm2_g[l]), w["wpa"], w["wpb"], w["wo"], tm=512), 0)
        (xp,), sample_dsw[1] = with_sample_group(
            _ffn_call(h2, x1, g2, row(normf_g), w["wu"], w["wd"], tm=512, final_norm=last), 1)
        og_s, lg_s, new_kv = zip(*sample_dsw)

        ((x1, h2),) = _run_jobs(_merge_call(oa_s, og_s, lg_s, ga_s, gb_s, xs, g1_s, sc2_s, sh2_s, row(norm2_g[l]),
                                            w["wpa"], w["wpb"], w["wo"], tm=min(512, n_seq * seq)))
        ((xs,),) = _run_jobs(_ffn_call(h2, x1, g2_s, row(normf_g), w["wu"], w["wd"], tm=min(512, n_seq * seq), final_norm=last))

        sp_l.append(jnp.swapaxes(st, 2, 3))
        kvp_l.append(tuple(_kv_unstack(kvt[g], min(win, t)) for g, (win, _) in enumerate(DSW_GROUPS)))
        ss_l.append(s_new)
        kvs_l.append(new_kv)

    y_prompt = xp
    y_sample = xs.reshape(n_seq, seq, d)
    stack = lambda items: jnp.stack(list(items))
    return (y_prompt, y_sample, stack(sp_l),
            stack(kv[0] for kv in kvp_l), stack(kv[1] for kv in kvp_l), stack(kv[2] for kv in kvp_l),
            stack(ss_l),
            stack(kv[0] for kv in kvs_l), stack(kv[1] for kv in kvs_l), stack(kv[2] for kv in kvs_l))
```

```python
import functools

import jax
import jax.numpy as jnp
import numpy as np
from jax import lax
from jax.experimental import pallas as pl
from jax.experimental.pallas import tpu as pltpu

F32 = jnp.float32
BF16 = jnp.bfloat16

EPS = 1e-6
GLA_HEADS = 4
GLA_DK = 128
GLA_DV = 256
GLA_RANK = 16
GLA_TAU = 16.0
GLA_CHUNK = 64
GLA_SCALE = GLA_DK ** -0.5
DSW_GROUPS = ((128, 1), (512, 4), (2048, 16))
DSW_HEADS = 4
DSW_HEAD_DIM = 64
DSW_SCALE = DSW_HEAD_DIM ** -0.5
DSW_GW = DSW_HEADS * DSW_HEAD_DIM
BAND = 128
DSW_STEP_ROWS = 2048
ROPE_THETA = 10000.0
PAST_LEN = 8192
LANES = 128
VMEM_LIMIT = 56 * 1024 * 1024


def _nn(a, b):
    return jnp.dot(a, b, preferred_element_type=F32)


def _nt(a, b):
    return lax.dot_general(a, b, (((1,), (1,)), ((), ())), preferred_element_type=F32)


def _tn(a, b):
    return lax.dot_general(a, b, (((0,), (0,)), ((), ())), preferred_element_type=F32)


def _split3(x):
    hi = x.astype(BF16)
    r1 = x - hi.astype(F32)
    mid = r1.astype(BF16)
    lo = (r1 - mid.astype(F32)).astype(BF16)
    return hi, mid, lo


def _iota(shape, dim):
    return lax.broadcasted_iota(jnp.int32, shape, dim)


def _rms(x, g):
    return x * lax.rsqrt(jnp.mean(x * x, axis=-1, keepdims=True) + EPS) * g


def _params(n_parallel, n_arbitrary=0):
    sem = ("parallel",) * n_parallel + ("arbitrary",) * n_arbitrary
    return pltpu.CompilerParams(dimension_semantics=sem, vmem_limit_bytes=VMEM_LIMIT)


def _resident(shape):
    nd = len(shape)
    return pl.BlockSpec(shape, lambda *_: (0,) * nd, pipeline_mode=pl.Buffered(1))


def _ada_kernel(c_ref, w_ref, b_ref, o_ref):
    c = c_ref[...]
    a = (c * jax.nn.sigmoid(c)).astype(BF16)
    o_ref[...] = _nn(a, w_ref[...].astype(BF16)) + b_ref[...]


def _ada(c_all, w_ada, b_ada):
    n, d = c_all.shape
    ncol = w_ada.shape[1]
    tn = 1536
    return pl.pallas_call(
        _ada_kernel,
        grid=(ncol // tn,),
        in_specs=[pl.BlockSpec((n, d), lambda j: (0, 0)),
                  pl.BlockSpec((d, tn), lambda j: (0, j)),
                  pl.BlockSpec((1, tn), lambda j: (0, j))],
        out_specs=pl.BlockSpec((n, tn), lambda j: (0, j)),
        out_shape=jax.ShapeDtypeStruct((n, ncol), F32),
        compiler_params=_params(1),
        name="ada_mod",
    )(c_all, w_ada, b_ada.reshape(1, ncol))


def _rope(x, cosf, sins, first_half):
    rot = jnp.where(first_half, pltpu.roll(x, LANES - 32, 1), pltpu.roll(x, 32, 1))
    return x * cosf + rot * sins


def _inproj_kernel(x_ref, sc_ref, sh_ref, g_ref, cos_ref, sin_ref,
                   wa_ref, ba_ref, wg_ref, bg_ref, w2_ref, b2_ref, wb_ref, bb_ref, wc_ref, bc_ref,
                   gq_ref, gk_ref, gv_ref, gr_ref, la_ref, ga_ref, gb_ref, *rest, fold):
    x = x_ref[...]
    h = (_rms(x, g_ref[...]) * (1.0 + sc_ref[...]) + sh_ref[...]).astype(BF16)

    cosf = cos_ref[...]
    sins = sin_ref[...]
    first_half = (_iota(cosf.shape, 1) % DSW_HEAD_DIM) < (DSW_HEAD_DIM // 2)
    tm = x.shape[0]
    per_group = DSW_GW // LANES
    width = 3 * DSW_GW

    def proj(off, g):
        cols = slice(off + g * DSW_GW, off + (g + 1) * DSW_GW)
        full = _nt(h, wb_ref[cols, :]) + bb_ref[:, cols]
        return [full[:, s * LANES:(s + 1) * LANES] for s in range(per_group)]

    for g, (_, dil) in enumerate(DSW_GROUPS):
        qs = [_rope(a, cosf, sins, first_half) * DSW_SCALE for a in proj(0, g)]
        ks = [_rope(a, cosf, sins, first_half) for a in proj(width, g)]
        vs = proj(2 * width, g)
        if fold:
            kvt_ref = rest[3 * len(DSW_GROUPS) + g]
            kvt_ref[0:DSW_GW, :] = jnp.concatenate(ks, axis=1).T
            kvt_ref[DSW_GW:2 * DSW_GW, :] = jnp.concatenate(vs, axis=1).T

        for slab in range(per_group):
            cols = slice(g * DSW_GW + slab * LANES, g * DSW_GW + (slab + 1) * LANES)
            lanes = slice(slab * LANES, (slab + 1) * LANES)
            if not fold:
                rest[0][:, cols] = ks[slab]
                rest[1][:, cols] = vs[slab]
                rest[2][:, cols] = qs[slab]
                continue
            scratch = rest[-1]
            for which, val in enumerate((qs[slab], ks[slab], vs[slab])):
                out_ref = rest[3 * g + which]
                if dil == 1:
                    out_ref[:, lanes] = val.astype(BF16)
                else:
                    scratch[which] = val
                    for r in range(dil):
                        out_ref[r, :, lanes] = scratch[which, pl.ds(r, tm // dil, stride=dil), :].astype(BF16)

    glr = (_nt(h, wg_ref[...]) + bg_ref[...]).astype(BF16)
    z = _nn(glr, w2_ref[...]) + b2_ref[...]
    la_ref[...] = jax.nn.log_sigmoid(z) * (1.0 / GLA_TAU)

    gq_ref[...] = (_nt(h, wa_ref[0:512, :]) + ba_ref[:, 0:512]).astype(gq_ref.dtype)
    gk_ref[...] = (_nt(h, wa_ref[512:1024, :]) + ba_ref[:, 512:1024]).astype(gk_ref.dtype)
    gv_ref[...] = (_nt(h, wa_ref[1024:2048, :]) + ba_ref[:, 1024:2048]).astype(gv_ref.dtype)
    gr_ref[...] = (_nt(h, wa_ref[2048:3072, :]) + ba_ref[:, 2048:3072]).astype(gr_ref.dtype)

    ga_ref[...] = (_nt(h, wc_ref[0:1024, :]) + bc_ref[:, 0:1024]).astype(ga_ref.dtype)
    gb_ref[...] = (_nt(h, wc_ref[1024:2048, :]) + bc_ref[:, 1024:2048]).astype(gb_ref.dtype)


def _inproj_call(x, sc, sh, g, cos_t, sin_t, w, tm, fold):
    nb, t, d = x.shape
    tiles = t // tm
    tok = functools.partial(_flat_tok_spec, tm, tiles)
    mod = lambda a: _flat_mod_spec(a, tm, tiles, d)
    out_cols = (512, 512, 1024, 1024, 512, 1024, 1024)
    out_dt = (BF16, BF16, BF16, BF16, F32, BF16, BF16)
    out_specs = [tok(n) for n in out_cols]
    out_shape = [jax.ShapeDtypeStruct((nb, t, n), dt) for n, dt in zip(out_cols, out_dt)]
    scratch = []
    if fold:
        for _, dil in DSW_GROUPS:
            for _ in range(3):
                if dil == 1:
                    out_specs.append(tok(DSW_GW))
                    out_shape.append(jax.ShapeDtypeStruct((nb, t, DSW_GW), BF16))
                else:
                    out_specs.append(pl.BlockSpec((None, dil, tm // dil, DSW_GW),
                                                  lambda i: (i // tiles, 0, i % tiles, 0)))
                    out_shape.append(jax.ShapeDtypeStruct((nb, dil, t // dil, DSW_GW), BF16))
        for win, _ in DSW_GROUPS:
            width = min(max(win, tm), t)
            first = (t - width) // tm
            out_specs.append(pl.BlockSpec((None, 2 * DSW_GW, tm),
                                          lambda i, first=first: (i // tiles, 0, jnp.maximum(i % tiles - first, 0))))
            out_shape.append(jax.ShapeDtypeStruct((nb, 2 * DSW_GW, width), F32))
        scratch = [pltpu.VMEM((3, tm, LANES), F32)]
    else:
        for _ in range(3):
            out_specs.append(tok(3 * DSW_GW))
            out_shape.append(jax.ShapeDtypeStruct((nb, t, 3 * DSW_GW), F32))
    weights = (w["wa"], w["ba"], w["wg"], w["bg"], w["w2"], w["b2"], w["wb"], w["bb"], w["wc"], w["bc"])
    table = pl.BlockSpec((tm, LANES), lambda i: (i % tiles, 0))
    return dict(
        name="inproj", steps=nb * tiles, body=functools.partial(_inproj_kernel, fold=fold),
        args=(x, sc[0], sh[0], g, cos_t, sin_t, *weights),
        in_specs=[tok(d), mod(sc), mod(sh), _resident((1, d)), table, table] + [_resident(a.shape) for a in weights],
        out_specs=out_specs, out_shape=out_shape, scratch=scratch)


def _cast_kernel(x_ref, o_ref):
    o_ref[...] = x_ref[...].astype(o_ref.dtype)


def _cast_call(w, steps):
    rows, cols = w.shape
    blk = next(b for b in range(16, rows + 1, 16) if rows % b == 0 and b * steps >= rows)
    n_blk = rows // blk
    spec = pl.BlockSpec((blk, cols), lambda i: (jnp.minimum(i, n_blk - 1), 0))
    return dict(name="cast", steps=steps, body=_cast_kernel, args=(w,), in_specs=[spec], out_specs=[spec],
                out_shape=[jax.ShapeDtypeStruct(w.shape, BF16)])


def _gla_local(gq_ref, gk_ref, la_ref, chunk):
    la = la_ref[...]
    tt = la.shape[0]
    r = _iota((tt, tt), 0)
    c = _iota((tt, tt), 1)
    same = (r // chunk) == (c // chunk)
    tri = jnp.where(same & (c <= r), 1.0, 0.0).astype(BF16)
    hi, mid, lo = _split3(la)
    b = _nn(tri, hi) + _nn(tri, mid) + _nn(tri, lo)
    if chunk % 8 == 0:
        bl = jnp.concatenate([jnp.broadcast_to(b[e - 1:e, :], (chunk, b.shape[1]))
                              for e in range(chunk, tt + 1, chunk)], axis=0)
    else:
        ones = jnp.where(same, 1.0, 0.0).astype(BF16)
        bl = _nn(ones, hi) + _nn(ones, mid) + _nn(ones, lo)
    gq = gq_ref[...].astype(F32)
    gk = gk_ref[...].astype(F32)
    qg = (gq * GLA_SCALE * jnp.exp(b)).astype(BF16)
    kd = (gk * jnp.exp(-b)).astype(BF16)
    kl = (gk * jnp.exp(bl - b)).astype(BF16)
    causal = same & (c <= r)
    return qg, kd, kl, jnp.exp(bl), causal


def _gla_finish(o, gr, g):
    return (_rms(o, g) * (gr * jax.nn.sigmoid(gr))).astype(BF16)


def _gla_prompt_body(first_tile, gq_ref, gk_ref, gv_ref, la_ref, gr_ref, g_ref, o_ref, st_ref):
    @pl.when(first_tile)
    def _():
        st_ref[...] = jnp.zeros_like(st_ref)

    qg, kd, kl, dec, causal = _gla_local(gq_ref, gk_ref, la_ref, GLA_CHUNK)
    tt = qg.shape[0]
    chunk_of_row = _iota((tt, GLA_DK), 0) // GLA_CHUNK
    for h in range(GLA_HEADS):
        kc = slice(h * GLA_DK, (h + 1) * GLA_DK)
        vc = slice(h * GLA_DV, (h + 1) * GLA_DV)
        v = gv_ref[:, vc].astype(BF16)
        att = jnp.where(causal, _nt(qg[:, kc], kd[:, kc]), 0.0).astype(BF16)
        intra = _nn(att, v)
        st = st_ref[h]
        n_chunks = tt // GLA_CHUNK
        kl_h = kl[:, kc]
        kl_bd = jnp.concatenate([jnp.where(chunk_of_row == ci, kl_h, jnp.zeros_like(kl_h))
                                 for ci in range(n_chunks)], axis=1)
        upd = _tn(v, kl_bd)
        inter = []
        for ci in range(n_chunks):
            rows = slice(ci * GLA_CHUNK, (ci + 1) * GLA_CHUNK)
            inter.append(_nt(qg[rows, kc], st.astype(BF16)))
            st = dec[ci * GLA_CHUNK:ci * GLA_CHUNK + 1, kc] * st + upd[:, ci * GLA_DK:(ci + 1) * GLA_DK]
        st_ref[h] = st
        o = intra + jnp.concatenate(inter, axis=0)
        o_ref[:, vc] = _gla_finish(o, gr_ref[:, vc].astype(F32), g_ref[...])


def _gla_prompt_call(gq, gk, gv, la, gr, g, tt):
    nb, t, _ = gq.shape
    tiles = t // tt
    tok = lambda n: pl.BlockSpec((None, tt, n), lambda i: (i // tiles, i % tiles, 0))
    def body(*refs):
        _gla_prompt_body(pl.program_id(0) % tiles == 0, *refs)

    return dict(
        name="gla_prompt", steps=nb * tiles, body=body, args=(gq, gk, gv, la, gr, g),
        in_specs=[tok(512), tok(512), tok(1024), tok(512), tok(1024), _resident((1, GLA_DV))],
        out_specs=[tok(1024),
                   pl.BlockSpec((None, GLA_HEADS, GLA_DV, GLA_DK), lambda i: (i // tiles, 0, 0, 0))],
        out_shape=[jax.ShapeDtypeStruct((nb, t, 1024), BF16),
                   jax.ShapeDtypeStruct((nb, GLA_HEADS, GLA_DV, GLA_DK), F32)])


def _run_jobs(*jobs):
    steps = jobs[0]["steps"]
    assert all(j["steps"] == steps for j in jobs)
    n_in = [len(j["in_specs"]) for j in jobs]
    n_out = [len(j["out_specs"]) for j in jobs]
    n_scr = [len(j.get("scratch", ())) for j in jobs]

    def kern(*refs):
        i_pos, o_pos, s_pos = 0, sum(n_in), sum(n_in) + sum(n_out)
        for j, ni, no, ns in zip(jobs, n_in, n_out, n_scr):
            j["body"](*refs[i_pos:i_pos + ni], *refs[o_pos:o_pos + no], *refs[s_pos:s_pos + ns])
            i_pos, o_pos, s_pos = i_pos + ni, o_pos + no, s_pos + ns

    outs = pl.pallas_call(
        kern,
        grid=(steps,),
        in_specs=[s for j in jobs for s in j["in_specs"]],
        out_specs=[s for j in jobs for s in j["out_specs"]],
        out_shape=[s for j in jobs for s in j["out_shape"]],
        scratch_shapes=[s for j in jobs for s in j.get("scratch", ())],
        compiler_params=_params(0, 1),
        name="__".join(j["name"] for j in jobs),
    )(*[a for j in jobs for a in j["args"]])
    split, pos = [], 0
    for no in n_out:
        split.append(list(outs[pos:pos + no]))
        pos += no
    return split


def _gla_sample_kernel(gq_ref, gk_ref, gv_ref, la_ref, gr_ref, g_ref, s_ref, o_ref, so_ref, *, seq):
    qg, kd, kl, dec, causal = _gla_local(gq_ref, gk_ref, la_ref, seq)
    rows_total = qg.shape[0]
    per8 = 8 // seq
    row8 = _iota((8, 1), 0)
    for h in range(GLA_HEADS):
        kc = slice(h * GLA_DK, (h + 1) * GLA_DK)
        vc = slice(h * GLA_DV, (h + 1) * GLA_DV)
        v = gv_ref[:, vc].astype(BF16)
        att = jnp.where(causal, _nt(qg[:, kc], kd[:, kc]), 0.0).astype(BF16)
        intra = _nn(att, v)
        inter = []
        for p in range(rows_total // 8):
            rows = slice(p * 8, (p + 1) * 8)
            d_hi, d_mid, d_lo = _split3(dec[rows, kc])
            inter_p = jnp.zeros((8, GLA_DV), F32)
            for j in range(per8):
                b = p * per8 + j
                r0 = j * seq
                s0 = s_ref[b, h]
                mine = (row8 >= r0) & (row8 < r0 + seq)
                inter_p = jnp.where(mine, _nn(qg[rows, kc], s0.astype(BF16)), inter_p)
                dl = jnp.where(row8 == r0, d_hi, jnp.where(row8 == r0 + 1, d_mid,
                               jnp.where(row8 == r0 + 2, d_lo, jnp.zeros_like(d_lo))))
                e = jnp.where((row8 >= r0) & (row8 < r0 + 3), 1.0, 0.0).astype(BF16)
                dec_b = _tn(dl, jnp.broadcast_to(e, (8, GLA_DV)))
                upd = _tn(jnp.where(mine, kl[rows, kc], jnp.zeros_like(kl[rows, kc])), v[rows])
                so_ref[b, h] = dec_b * s0 + upd
            inter.append(inter_p)
        o = intra + jnp.concatenate(inter, axis=0)
        o_ref[:, vc] = _gla_finish(o, gr_ref[:, vc].astype(F32), g_ref[...])


def _gla_sample_call(gq, gk, gv, la, gr, g, s0, seq, bb):
    n_seq = s0.shape[0]
    rows = bb * seq
    tok = lambda n: pl.BlockSpec((None, rows, n), lambda i: (0, i, 0))
    st = pl.BlockSpec((bb, GLA_HEADS, GLA_DK, GLA_DV), lambda i: (i, 0, 0, 0))
    return dict(
        name="gla_sample", steps=n_seq // bb, body=functools.partial(_gla_sample_kernel, seq=seq),
        args=(gq, gk, gv, la, gr, g, s0),
        in_specs=[tok(512), tok(512), tok(1024), tok(512), tok(1024), _resident((1, GLA_DV)), st],
        out_specs=[tok(1024), st],
        out_shape=[jax.ShapeDtypeStruct((1, n_seq * seq, 1024), BF16),
                   jax.ShapeDtypeStruct(s0.shape, F32)])


def _dsw_prompt_kernel(q_ref, kp_ref, kc_ref, vp_ref, vc_ref, o_ref, lse_ref, *, dil, tile, res):
    qb = q_ref.shape[0]
    first_key = jnp.where(tile == 0, BAND, 0)
    qi = _iota((BAND, 2 * BAND), 0) + BAND
    ki = _iota((BAND, 2 * BAND), 1)
    band = (qi - ki >= 0) & (qi - ki <= BAND)
    lane = _iota((BAND, LANES), 1)
    for s in range(qb // BAND):
        rows = slice(s * BAND, (s + 1) * BAND)
        if s == 0:
            valid = band & (ki >= first_key)
        else:
            valid = band
        if dil == 1:
            tok_rows = rows
        else:
            tok_rows = pl.ds(s * BAND * dil + res, BAND, stride=dil)
        for hp in range(DSW_GW // LANES):
            cols = slice(hp * LANES, (hp + 1) * LANES)
            qp = q_ref[rows, cols]
            if s == 0:
                kcat = jnp.concatenate([kp_ref[:, cols], kc_ref[0:BAND, cols]], axis=0)
                vcat = jnp.concatenate([vp_ref[:, cols], vc_ref[0:BAND, cols]], axis=0)
            else:
                kcat = kc_ref[(s - 1) * BAND:(s + 1) * BAND, cols]
                vcat = vc_ref[(s - 1) * BAND:(s + 1) * BAND, cols]
            outs, lses = [], []
            for hh in range(LANES // DSW_HEAD_DIM):
                in_head = (lane // DSW_HEAD_DIM) == hh
                sc = _nt(jnp.where(in_head, qp, jnp.zeros_like(qp)), kcat)
                sc = jnp.where(valid, sc, -jnp.inf)
                m = jnp.max(sc, axis=-1, keepdims=True)
                e = jnp.exp(sc - m)
                den = jnp.sum(e, axis=-1, keepdims=True)
                outs.append(_nn((e / den).astype(BF16), vcat))
                lses.append(m + jnp.log(den))
            first = lane < DSW_HEAD_DIM
            o_ref[hp, tok_rows, :] = jnp.where(first, outs[0], outs[1])
            lse_ref[hp, tok_rows, :] = jnp.where(first, lses[0], jnp.broadcast_to(lses[1], (BAND, LANES)))


def _dsw_prompt_call(q, k, v, dil):
    nb = q.shape[0]
    seq_len = q.shape[-2]
    t = seq_len * dil
    tq = min(DSW_STEP_ROWS, seq_len)
    sub = tq // BAND
    n_tiles = seq_len // tq
    rpb = max(1, min(dil, DSW_STEP_ROWS // tq))
    assert dil % rpb == 0
    n_res = dil // rpb
    bat = lambda i: i // (n_tiles * n_res)
    til = lambda i: (i // n_res) % n_tiles
    grp = lambda i: i % n_res
    prev_blk = lambda i: jnp.maximum(til(i) * sub - 1, 0)
    if dil == 1:
        cur = pl.BlockSpec((None, tq, DSW_GW), lambda i: (bat(i), til(i), 0))
        prev = pl.BlockSpec((None, BAND, DSW_GW), lambda i: (bat(i), prev_blk(i), 0))
    else:
        cur = pl.BlockSpec((None, rpb, tq, DSW_GW), lambda i: (bat(i), grp(i), til(i), 0))
        prev = pl.BlockSpec((None, rpb, BAND, DSW_GW), lambda i: (bat(i), grp(i), prev_blk(i), 0))
    n_slab = DSW_GW // LANES
    out = pl.BlockSpec((None, n_slab, tq * dil, LANES), lambda i: (bat(i), 0, til(i), 0))

    def body(*refs):
        i = pl.program_id(0)
        if dil == 1:
            _dsw_prompt_kernel(*refs, dil=dil, tile=til(i), res=0)
        else:
            for rr in range(rpb):
                _dsw_prompt_kernel(*[r.at[rr] for r in refs[:5]], *refs[5:], dil=dil, tile=til(i),
                                   res=grp(i) * rpb + rr)

    return dict(
        name=f"dsw_prompt_d{dil}", steps=nb * n_tiles * n_res, body=body, args=(q, k, k, v, v),
        in_specs=[cur, prev, cur, prev, cur],
        out_specs=[out, out],
        out_shape=[jax.ShapeDtypeStruct((nb, n_slab, t, LANES), F32)] * 2)


def _dsw_sample_kernel(q_ref, kn_ref, vn_ref, c_ref, o_ref, lse_ref, co_ref, *, seq, dil):
    per8 = 8 // seq
    win = c_ref.shape[2]
    n_rows = DSW_HEADS * 8
    lane = _iota((8, LANES), 1)
    head_of_lane = _iota((8, DSW_GW), 1) // DSW_HEAD_DIM
    row8 = _iota((8, 1), 0)
    r = _iota((n_rows, 1), 0)
    r_step = r % seq
    r_seq = (r % 8) // seq
    key = _iota((n_rows, win), 1)
    cache_ok = ((key % dil) == (r_step % dil)) & (key >= r_step)
    c128 = _iota((n_rows, LANES), 1)
    new_ok = ((c128 < 8) & ((c128 // seq) == r_seq) & ((c128 % seq) <= r_step)
              & (((r_step - c128 % seq) % dil) == 0))
    pad = jnp.zeros((LANES - 8, DSW_GW), BF16)

    def by_head(x):
        out = x[(DSW_HEADS - 1) * 8:DSW_HEADS * 8]
        for h in range(DSW_HEADS - 2, -1, -1):
            out = jnp.where(head_of_lane == h, x[h * 8:(h + 1) * 8], out)
        return out

    lane_sq = _iota((LANES, LANES), 1)
    p_row = _iota((8, LANES), 0)
    for grp in range(q_ref.shape[0] // 8):
        r8 = slice(grp * 8, (grp + 1) * 8)
        q8 = q_ref[r8, :]
        qrows = jnp.concatenate([jnp.where(head_of_lane == h, q8, 0.0) for h in range(DSW_HEADS)],
                                axis=0).astype(BF16)
        kn8 = kn_ref[r8, :]
        vn8 = vn_ref[r8, :]
        kn_t = jnp.concatenate([kn8.astype(BF16), pad], axis=0)
        vn_t = jnp.concatenate([vn8.astype(BF16), pad], axis=0)
        scn = jnp.where(new_ok, _nt(qrows, kn_t), -jnp.inf)
        m_new = jnp.max(scn, axis=-1, keepdims=True)
        o_p = jnp.zeros((8, DSW_GW), F32)
        l_p = jnp.zeros((8, DSW_GW), F32)
        for j in range(per8):
            b = grp * per8 + j
            kt = c_ref[b, 0:DSW_GW, :].astype(BF16)
            vt = c_ref[b, DSW_GW:2 * DSW_GW, :].astype(BF16)
            sc = jnp.where(cache_ok, _nn(qrows, kt), -jnp.inf)
            m = jnp.maximum(jnp.max(sc, axis=-1, keepdims=True), m_new)
            e = jnp.exp(sc - m)
            en = jnp.exp(scn - m)
            den = jnp.sum(e, axis=-1, keepdims=True) + jnp.sum(en, axis=-1, keepdims=True)
            o = _nt((e / den).astype(BF16), vt) + _nn((en / den).astype(BF16), vn_t)
            lse = jnp.broadcast_to(m + jnp.log(den), (n_rows, DSW_GW))
            mine = (row8 // seq) == j
            o_p = jnp.where(mine, by_head(o), o_p)
            l_p = jnp.where(mine, by_head(lse), l_p)
        o_ref[r8, :] = o_p
        lse_ref[r8, :] = l_p

        hi, mid, lo = _split3(jnp.concatenate([kn8, vn8], axis=1))
        for j in range(per8):
            b = grp * per8 + j
            place = jnp.where(((p_row // seq) == j) & (lane == LANES - seq + p_row % seq), 1.0, 0.0).astype(BF16)
            new_cols = _tn(hi, place) + _tn(mid, place) + _tn(lo, place)
            for blk in range(2 * DSW_GW // LANES):
                rows = slice(blk * LANES, (blk + 1) * LANES)
                rolled = pltpu.roll(c_ref[b, rows, :], win - seq, 1)
                if win > LANES:
                    co_ref[b, rows, 0:win - LANES] = rolled[:, 0:win - LANES]
                co_ref[b, rows, win - LANES:win] = jnp.where(lane_sq < LANES - seq, rolled[:, win - LANES:win],
                                                             new_cols[rows])


def _dsw_sample_call(q32, k32, v32, cache, g, dil, seq, steps=None):
    n_seq, win = cache.shape[0], cache.shape[1]
    per8 = 8 // seq
    if steps is None:
        groups = max(1, min(8, 4 * 512 // win))
    else:
        groups = n_seq // (per8 * steps)
    n_blk = per8 * groups
    view = jnp.transpose(cache, (0, 2, 3, 4, 1)).reshape(n_seq, 2 * DSW_GW, win)
    tok = pl.BlockSpec((None, 8 * groups, DSW_GW), lambda i: (0, i, g))
    tok_out = pl.BlockSpec((None, 8 * groups, DSW_GW), lambda i: (0, i, 0))
    cspec = pl.BlockSpec((n_blk, 2 * DSW_GW, win), lambda i: (i, 0, 0))
    return dict(
        name=f"dsw_sample_d{dil}", steps=n_seq // n_blk,
        body=functools.partial(_dsw_sample_kernel, seq=seq, dil=dil), args=(q32, k32, v32, view),
        in_specs=[tok, tok, tok, cspec],
        out_specs=[tok_out, tok_out, cspec],
        out_shape=[jax.ShapeDtypeStruct((1, n_seq * seq, DSW_GW), F32)] * 2
                  + [jax.ShapeDtypeStruct(view.shape, F32)])


def _dsw_sample_finish(o, lse, new, cache_shape):
    n_seq, win = cache_shape[0], cache_shape[1]
    new = jnp.transpose(new.reshape(n_seq, 2, DSW_HEADS, DSW_HEAD_DIM, win), (0, 4, 1, 2, 3))
    n_slab = DSW_GW // LANES
    slabs = lambda a: jnp.transpose(a.reshape(1, -1, n_slab, LANES), (0, 2, 1, 3))
    return slabs(o), slabs(lse), new


def _merge_kernel(oa_ref, o0_ref, o1_ref, o2_ref, l0_ref, l1_ref, l2_ref, ga_ref, gb_ref, x_ref,
                  g1_ref, sc_ref, sh_ref, n2_ref, wpa_ref, wpb_ref, wo_ref, x1_ref, h2_ref):
    ob = []
    for slab in range(DSW_GW // LANES):
        l0, l1, l2 = l0_ref[slab], l1_ref[slab], l2_ref[slab]
        m = jnp.maximum(jnp.maximum(l0, l1), l2)
        w0, w1, w2 = jnp.exp(l0 - m), jnp.exp(l1 - m), jnp.exp(l2 - m)
        den = w0 + w1 + w2
        ob.append((w0 / den) * o0_ref[slab] + (w1 / den) * o1_ref[slab] + (w2 / den) * o2_ref[slab])
    ob = jnp.concatenate(ob, axis=1).astype(BF16)
    merged = (jax.nn.sigmoid(ga_ref[...].astype(F32)) * _nn(oa_ref[...], wpa_ref[...])
              + jax.nn.sigmoid(gb_ref[...].astype(F32)) * _nn(ob, wpb_ref[...]))
    x1 = x_ref[...] + g1_ref[...] * _nn(merged.astype(BF16), wo_ref[...])
    x1_ref[...] = x1
    h2_ref[...] = (_rms(x1, n2_ref[...]) * (1.0 + sc_ref[...]) + sh_ref[...]).astype(BF16)


def _flat_tok_spec(tm, tiles, n):
    return pl.BlockSpec((None, tm, n), lambda i: (i // tiles, i % tiles, 0))


def _flat_mod_spec(m, tm, tiles, d):
    arr, k = m
    if arr.ndim == 4:
        return pl.BlockSpec((None, None, 1, d), lambda i: (i // tiles, k, 0, 0))
    return pl.BlockSpec((tm, d), lambda i: (i % tiles, k))


def _merge_call(oa, og, lg, ga, gb, x, g1, sc2, sh2, n2, wpa, wpb, wo, tm):
    nb, t, d = x.shape
    tiles = t // tm
    tok = functools.partial(_flat_tok_spec, tm, tiles)
    mod = lambda a: _flat_mod_spec(a, tm, tiles, d)
    slab = pl.BlockSpec((None, DSW_GW // LANES, tm, LANES), lambda i: (i // tiles, 0, i % tiles, 0))
    return dict(
        name="merge_outproj", steps=nb * tiles, body=_merge_kernel,
        args=(oa, *og, *lg, ga, gb, x, g1[0], sc2[0], sh2[0], n2, wpa, wpb, wo),
        in_specs=[tok(1024)] + [slab] * 6 + [tok(d), tok(d), tok(d), mod(g1), mod(sc2), mod(sh2),
                  _resident((1, d)), _resident(wpa.shape), _resident(wpb.shape), _resident(wo.shape)],
        out_specs=[tok(d), tok(d)],
        out_shape=[jax.ShapeDtypeStruct((nb, t, d), F32), jax.ShapeDtypeStruct((nb, t, d), BF16)])


def _ffn_kernel(h_ref, x_ref, g2_ref, nf_ref, wu_ref, wd_ref, y_ref, *, final_norm, n_split):
    h = h_ref[...]
    d_ff = wd_ref.shape[0]
    step = d_ff // n_split
    acc = None
    for j in range(n_split):
        u1 = _nn(h, wu_ref[:, j * step:(j + 1) * step])
        u2 = _nn(h, wu_ref[:, d_ff + j * step:d_ff + (j + 1) * step])
        a = (u1 * jax.nn.sigmoid(u1) * u2).astype(BF16)
        part = _nn(a, wd_ref[j * step:(j + 1) * step, :])
        acc = part if acc is None else acc + part
    x2 = x_ref[...] + g2_ref[...] * acc
    y_ref[...] = _rms(x2, nf_ref[...]) if final_norm else x2


def _ffn_call(h2, x1, g2, nf, wu, wd, tm, final_norm):
    nb, t, d = x1.shape
    tiles = t // tm
    tok = functools.partial(_flat_tok_spec, tm, tiles)
    return dict(
        name="ffn", steps=nb * tiles,
        body=functools.partial(_ffn_kernel, final_norm=final_norm, n_split=wd.shape[0] // (2 * LANES)),
        args=(h2, x1, g2[0], nf, wu, wd),
        in_specs=[tok(d), tok(d), _flat_mod_spec(g2, tm, tiles, d), _resident((1, d)),
                  _resident(wu.shape), _resident(wd.shape)],
        out_specs=[tok(d)],
        out_shape=[jax.ShapeDtypeStruct((nb, t, d), F32)])


def _rope_tables(pos):
    half = DSW_HEAD_DIM // 2
    inv = ROPE_THETA ** (-np.arange(half, dtype=np.float64) / half)
    ang = np.asarray(pos, np.float64)[:, None] * inv[None, :]
    reps = LANES // half
    sign = np.tile(np.concatenate([-np.ones(half), np.ones(half)]), LANES // DSW_HEAD_DIM)
    cosf = np.tile(np.cos(ang), (1, reps))
    sins = np.tile(np.sin(ang), (1, reps)) * sign[None, :]
    return jnp.asarray(cosf, F32), jnp.asarray(sins, F32)


def _layer_weights(w_in, b_in, w_alpha2, b_alpha2):
    bf = lambda a: a.astype(BF16)
    row = lambda a: a.reshape(1, -1)
    o_glr, o_dq, o_ga = 3072, 3088, 5392
    pad_r = LANES - GLA_RANK
    wt = w_in.T
    return dict(
        wa=bf(wt[:o_glr]), ba=row(b_in[:o_glr]),
        wg=bf(jnp.pad(wt[o_glr:o_dq], ((0, pad_r), (0, 0)))), bg=row(jnp.pad(b_in[o_glr:o_dq], (0, pad_r))),
        w2=bf(jnp.pad(w_alpha2, ((0, pad_r), (0, 0)))), b2=row(b_alpha2),
        wb=bf(wt[o_dq:o_ga]), bb=row(b_in[o_dq:o_ga]),
        wc=bf(wt[o_ga:]), bc=row(b_in[o_ga:]))


def _kv_unstack(kvt, keep):
    nb, _, width = kvt.shape
    kv = kvt[:, :, width - keep:].reshape(nb, 2, DSW_HEADS, DSW_HEAD_DIM, keep)
    return jnp.transpose(kv, (0, 4, 1, 2, 3))


def kernel(x_prompt, x_sample, state_gla, cache_kv_w128, cache_kv_w512, cache_kv_w2048, c_prompt, c_sample,
           norm1_g, norm2_g, w_ada, b_ada, w_in, b_in, w_alpha2, b_alpha2, gla_norm_g, w_proj_a, w_proj_b,
           w_out, w_up, w_down, normf_g):
    depth = w_ada.shape[0]
    nb, t, d = x_prompt.shape
    n_seq, seq, _ = x_sample.shape
    assert 8 % seq == 0 and seq >= 3, "sample kernels pack whole sequences into 8-row groups"
    past = PAST_LEN
    caches = (cache_kv_w128, cache_kv_w512, cache_kv_w2048)

    cos_p, sin_p = _rope_tables(np.arange(t))
    cos_s, sin_s = _rope_tables(np.tile(past + np.arange(seq), n_seq))

    n_tok_s = n_seq * seq
    pad_c = (-(n_tok_s + nb)) % 8
    c_all = jnp.pad(jnp.concatenate([jnp.repeat(c_sample, seq, axis=0), c_prompt], axis=0), ((0, pad_c), (0, 0)))

    xp = x_prompt
    xs = x_sample.reshape(1, n_seq * seq, d)
    row = lambda a: a.reshape(1, -1)
    sp_l, kvp_l, ss_l, kvs_l = [], [], [], []
    for l in range(depth):
        w = _layer_weights(w_in[l], b_in[l], w_alpha2[l], b_alpha2[l])
        mod = _ada(c_all, w_ada[l], b_ada[l])
        mod_rows_p = mod[n_tok_s:n_tok_s + nb].reshape(nb, 6, 1, d)
        mod_p = [(mod_rows_p, k) for k in range(6)]
        mod_s = [(mod, k) for k in range(6)]
        last = l == depth - 1

        sh1_s, sc1_s, g1_s, sh2_s, sc2_s, g2_s = mod_s
        ((gq_s, gk_s, gv_s, gr_s, la_s, ga_s, gb_s, k32, v32, q32),) = _run_jobs(_inproj_call(
            xs, sc1_s, sh1_s, row(norm1_g[l]), cos_s, sin_s, w, tm=min(512, n_seq * seq), fold=False))
        sh1, sc1, g1, sh2, sc2, g2 = mod_p
        inproj_p = _inproj_call(xp, sc1, sh1, row(norm1_g[l]), cos_p, sin_p, w, tm=512, fold=True)
        late = (w_proj_a[l], w_proj_b[l], w_out[l], w_up[l], w_down[l])
        (gq, gk, gv, gr, la, ga, gb, *dsw_p), *cast = _run_jobs(
            inproj_p, *[_cast_call(a, inproj_p["steps"]) for a in late])
        w.update(zip(("wpa", "wpb", "wo", "wu", "wd"), (c[0] for c in cast)))
        qkv, kvt = dsw_p[:3 * len(DSW_GROUPS)], dsw_p[3 * len(DSW_GROUPS):]

        def with_sample_group(host, g):
            steps = host["steps"]
            fits = n_seq % ((8 // seq) * steps) == 0 and n_seq // ((8 // seq) * steps) <= 8
            job = _dsw_sample_call(q32, k32, v32, caches[g][l], g, DSW_GROUPS[g][1], seq, steps if fits else None)
            if fits:
                host_out, job_out = _run_jobs(host, job)
            else:
                (host_out,), (job_out,) = _run_jobs(host), _run_jobs(job)
            return host_out, _dsw_sample_finish(*job_out, caches[g][l].shape)

        sample_dsw = [None] * len(DSW_GROUPS)
        (oa, st), sample_dsw[2] = with_sample_group(
            _gla_prompt_call(gq, gk, gv, la, gr, row(gla_norm_g[l]), tt=256), 2)
        dsw_jobs = [_dsw_prompt_call(*qkv[3 * g:3 * g + 3], dil) for g, (_, dil) in enumerate(DSW_GROUPS)]
        gla_s_args = (gq_s, gk_s, gv_s, la_s, gr_s, row(gla_norm_g[l]), state_gla[l], seq)
        host = next((j for j in dsw_jobs if n_seq % j["steps"] == 0 and (n_seq // j["steps"]) * seq % 16 == 0), None)
        og, lg = [], []
        for job in dsw_jobs:
            if job is host:
                (o_g, l_g), (oa_s, s_new) = _run_jobs(job, _gla_sample_call(*gla_s_args, bb=n_seq // job["steps"]))
            else:
                ((o_g, l_g),) = _run_jobs(job)
            og.append(o_g)
            lg.append(l_g)
        if host is None:
            ((oa_s, s_new),) = _run_jobs(_gla_sample_call(*gla_s_args, bb=8))

        (x1, h2), sample_dsw[0] = with_sample_group(
            _merge_call(oa, og, lg, ga, gb, xp, g1, sc2, sh2, row(norm2_g[l]), w["wpa"], w["wpb"], w["wo"], tm=512), 0)
        (xp,), sample_dsw[1] = with_sample_group(
            _ffn_call(h2, x1, g2, row(normf_g), w["wu"], w["wd"], tm=512, final_norm=last), 1)
        og_s, lg_s, new_kv = zip(*sample_dsw)

        ((x1, h2),) = _run_jobs(_merge_call(oa_s, og_s, lg_s, ga_s, gb_s, xs, g1_s, sc2_s, sh2_s, row(norm2_g[l]),
                                            w["wpa"], w["wpb"], w["wo"], tm=min(512, n_seq * seq)))
        ((xs,),) = _run_jobs(_ffn_call(h2, x1, g2_s, row(normf_g), w["wu"], w["wd"], tm=min(512, n_seq * seq), final_norm=last))

        sp_l.append(jnp.swapaxes(st, 2, 3))
        kvp_l.append(tuple(_kv_unstack(kvt[g], min(win, t)) for g, (win, _) in enumerate(DSW_GROUPS)))
        ss_l.append(s_new)
        kvs_l.append(new_kv)

    y_prompt = xp
    y_sample = xs.reshape(n_seq, seq, d)
    stack = lambda items: jnp.stack(list(items))
    return (y_prompt, y_sample, stack(sp_l),
            stack(kv[0] for kv in kvp_l), stack(kv[1] for kv in kvp_l), stack(kv[2] for kv in kvp_l),
            stack(ss_l),
            stack(kv[0] for kv in kvs_l), stack(kv[1] for kv in kvs_l), stack(kv[2] for kv in kvs_l))
```

```python
import functools

import jax
import jax.numpy as jnp
import numpy as np
from jax import lax
from jax.experimental import pallas as pl
from jax.experimental.pallas import tpu as pltpu

F32 = jnp.float32
BF16 = jnp.bfloat16

EPS = 1e-6
GLA_HEADS = 4
GLA_DK = 128
GLA_DV = 256
GLA_RANK = 16
GLA_TAU = 16.0
GLA_CHUNK = 64
GLA_SCALE = GLA_DK ** -0.5
DSW_GROUPS = ((128, 1), (512, 4), (2048, 16))
DSW_HEADS = 4
DSW_HEAD_DIM = 64
DSW_SCALE = DSW_HEAD_DIM ** -0.5
DSW_GW = DSW_HEADS * DSW_HEAD_DIM
BAND = 128
DSW_STEP_ROWS = 2048
ROPE_THETA = 10000.0
PAST_LEN = 8192
LANES = 128
VMEM_LIMIT = 60 * 1024 * 1024


def _nn(a, b):
    return jnp.dot(a, b, preferred_element_type=F32)


def _nt(a, b):
    return lax.dot_general(a, b, (((1,), (1,)), ((), ())), preferred_element_type=F32)


def _tn(a, b):
    return lax.dot_general(a, b, (((0,), (0,)), ((), ())), preferred_element_type=F32)


def _split3(x):
    hi = x.astype(BF16)
    r1 = x - hi.astype(F32)
    mid = r1.astype(BF16)
    lo = (r1 - mid.astype(F32)).astype(BF16)
    return hi, mid, lo


def _iota(shape, dim):
    return lax.broadcasted_iota(jnp.int32, shape, dim)


def _rms(x, g):
    return x * lax.rsqrt(jnp.mean(x * x, axis=-1, keepdims=True) + EPS) * g


def _params(n_parallel, n_arbitrary=0):
    sem = ("parallel",) * n_parallel + ("arbitrary",) * n_arbitrary
    return pltpu.CompilerParams(dimension_semantics=sem, vmem_limit_bytes=VMEM_LIMIT)


def _resident(shape):
    nd = len(shape)
    return pl.BlockSpec(shape, lambda *_: (0,) * nd, pipeline_mode=pl.Buffered(1))


def _ada_kernel(c_ref, w_ref, b_ref, o_ref):
    c = c_ref[...]
    a = (c * jax.nn.sigmoid(c)).astype(BF16)
    o_ref[...] = _nn(a, w_ref[...].astype(BF16)) + b_ref[...]


def _ada(c_all, w_ada, b_ada):
    n, d = c_all.shape
    ncol = w_ada.shape[1]
    tn = 1536
    return pl.pallas_call(
        _ada_kernel,
        grid=(ncol // tn,),
        in_specs=[pl.BlockSpec((n, d), lambda j: (0, 0)),
                  pl.BlockSpec((d, tn), lambda j: (0, j)),
                  pl.BlockSpec((1, tn), lambda j: (0, j))],
        out_specs=pl.BlockSpec((n, tn), lambda j: (0, j)),
        out_shape=jax.ShapeDtypeStruct((n, ncol), F32),
        compiler_params=_params(1),
        name="ada_mod",
    )(c_all, w_ada, b_ada.reshape(1, ncol))


def _rope(x, cosf, sins, first_half):
    rot = jnp.where(first_half, pltpu.roll(x, LANES - 32, 1), pltpu.roll(x, 32, 1))
    return x * cosf + rot * sins


def _inproj_kernel(x_ref, sc_ref, sh_ref, g_ref, cos_ref, sin_ref,
                   wa_ref, ba_ref, wg_ref, bg_ref, w2_ref, b2_ref, wb_ref, bb_ref, wc_ref, bc_ref,
                   gq_ref, gk_ref, gv_ref, gr_ref, la_ref, ga_ref, gb_ref, *rest, fold):
    x = x_ref[...]
    h = (_rms(x, g_ref[...]) * (1.0 + sc_ref[...]) + sh_ref[...]).astype(BF16)

    cosf = cos_ref[...]
    sins = sin_ref[...]
    first_half = (_iota(cosf.shape, 1) % DSW_HEAD_DIM) < (DSW_HEAD_DIM // 2)
    tm = x.shape[0]
    per_group = DSW_GW // LANES
    width = 3 * DSW_GW

    def proj(off, g):
        cols = slice(off + g * DSW_GW, off + (g + 1) * DSW_GW)
        full = _nt(h, wb_ref[cols, :]) + bb_ref[:, cols]
        return [full[:, s * LANES:(s + 1) * LANES] for s in range(per_group)]

    for g, (_, dil) in enumerate(DSW_GROUPS):
        qs = [_rope(a, cosf, sins, first_half) * DSW_SCALE for a in proj(0, g)]
        ks = [_rope(a, cosf, sins, first_half) for a in proj(width, g)]
        vs = proj(2 * width, g)
        if fold:
            kvt_ref = rest[3 * len(DSW_GROUPS) + g]
            kvt_ref[0:DSW_GW, :] = jnp.concatenate(ks, axis=1).T
            kvt_ref[DSW_GW:2 * DSW_GW, :] = jnp.concatenate(vs, axis=1).T

        for slab in range(per_group):
            cols = slice(g * DSW_GW + slab * LANES, g * DSW_GW + (slab + 1) * LANES)
            lanes = slice(slab * LANES, (slab + 1) * LANES)
            if not fold:
                rest[0][:, cols] = ks[slab]
                rest[1][:, cols] = vs[slab]
                rest[2][:, cols] = qs[slab]
                continue
            scratch = rest[-1]
            for which, val in enumerate((qs[slab], ks[slab], vs[slab])):
                out_ref = rest[3 * g + which]
                if dil == 1:
                    out_ref[:, lanes] = val.astype(BF16)
                else:
                    scratch[which] = val
                    for r in range(dil):
                        out_ref[r, :, lanes] = scratch[which, pl.ds(r, tm // dil, stride=dil), :].astype(BF16)

    glr = (_nt(h, wg_ref[...]) + bg_ref[...]).astype(BF16)
    z = _nn(glr, w2_ref[...]) + b2_ref[...]
    la_ref[...] = jax.nn.log_sigmoid(z) * (1.0 / GLA_TAU)

    gq_ref[...] = (_nt(h, wa_ref[0:512, :]) + ba_ref[:, 0:512]).astype(gq_ref.dtype)
    gk_ref[...] = (_nt(h, wa_ref[512:1024, :]) + ba_ref[:, 512:1024]).astype(gk_ref.dtype)
    gv_ref[...] = (_nt(h, wa_ref[1024:2048, :]) + ba_ref[:, 1024:2048]).astype(gv_ref.dtype)
    gr_ref[...] = (_nt(h, wa_ref[2048:3072, :]) + ba_ref[:, 2048:3072]).astype(gr_ref.dtype)

    ga_ref[...] = (_nt(h, wc_ref[0:1024, :]) + bc_ref[:, 0:1024]).astype(ga_ref.dtype)
    gb_ref[...] = (_nt(h, wc_ref[1024:2048, :]) + bc_ref[:, 1024:2048]).astype(gb_ref.dtype)


def _inproj_call(x, sc, sh, g, cos_t, sin_t, w, tm, fold):
    nb, t, d = x.shape
    tiles = t // tm
    tok = functools.partial(_flat_tok_spec, tm, tiles)
    mod = lambda a: _flat_mod_spec(a, tm, tiles, d)
    out_cols = (512, 512, 1024, 1024, 512, 1024, 1024)
    out_dt = (BF16, BF16, BF16, BF16, F32, BF16, BF16)
    out_specs = [tok(n) for n in out_cols]
    out_shape = [jax.ShapeDtypeStruct((nb, t, n), dt) for n, dt in zip(out_cols, out_dt)]
    scratch = []
    if fold:
        for _, dil in DSW_GROUPS:
            for _ in range(3):
                if dil == 1:
                    out_specs.append(tok(DSW_GW))
                    out_shape.append(jax.ShapeDtypeStruct((nb, t, DSW_GW), BF16))
                else:
                    out_specs.append(pl.BlockSpec((None, dil, tm // dil, DSW_GW),
                                                  lambda i: (i // tiles, 0, i % tiles, 0)))
                    out_shape.append(jax.ShapeDtypeStruct((nb, dil, t // dil, DSW_GW), BF16))
        for win, _ in DSW_GROUPS:
            width = min(max(win, tm), t)
            first = (t - width) // tm
            out_specs.append(pl.BlockSpec((None, 2 * DSW_GW, tm),
                                          lambda i, first=first: (i // tiles, 0, jnp.maximum(i % tiles - first, 0))))
            out_shape.append(jax.ShapeDtypeStruct((nb, 2 * DSW_GW, width), F32))
        scratch = [pltpu.VMEM((3, tm, LANES), F32)]
    else:
        for _ in range(3):
            out_specs.append(tok(3 * DSW_GW))
            out_shape.append(jax.ShapeDtypeStruct((nb, t, 3 * DSW_GW), F32))
    weights = (w["wa"], w["ba"], w["wg"], w["bg"], w["w2"], w["b2"], w["wb"], w["bb"], w["wc"], w["bc"])
    table = pl.BlockSpec((tm, LANES), lambda i: (i % tiles, 0))
    return dict(
        name="inproj", steps=nb * tiles, body=functools.partial(_inproj_kernel, fold=fold),
        args=(x, sc[0], sh[0], g, cos_t, sin_t, *weights),
        in_specs=[tok(d), mod(sc), mod(sh), _resident((1, d)), table, table] + [_resident(a.shape) for a in weights],
        out_specs=out_specs, out_shape=out_shape, scratch=scratch)


def _cast_kernel(x_ref, o_ref):
    o_ref[...] = x_ref[...].astype(o_ref.dtype)


def _cast_call(w, steps):
    rows, cols = w.shape
    blk = next(b for b in range(16, rows + 1, 16) if rows % b == 0 and b * steps >= rows)
    n_blk = rows // blk
    spec = pl.BlockSpec((blk, cols), lambda i: (jnp.minimum(i, n_blk - 1), 0))
    return dict(name="cast", steps=steps, body=_cast_kernel, args=(w,), in_specs=[spec], out_specs=[spec],
                out_shape=[jax.ShapeDtypeStruct(w.shape, BF16)])


def _gla_local(gq_ref, gk_ref, la_ref, chunk):
    la = la_ref[...]
    tt = la.shape[0]
    r = _iota((tt, tt), 0)
    c = _iota((tt, tt), 1)
    same = (r // chunk) == (c // chunk)
    tri = jnp.where(same & (c <= r), 1.0, 0.0).astype(BF16)
    hi, mid, lo = _split3(la)
    b = _nn(tri, hi) + _nn(tri, mid) + _nn(tri, lo)
    if chunk % 8 == 0:
        bl = jnp.concatenate([jnp.broadcast_to(b[e - 1:e, :], (chunk, b.shape[1]))
                              for e in range(chunk, tt + 1, chunk)], axis=0)
    else:
        ones = jnp.where(same, 1.0, 0.0).astype(BF16)
        bl = _nn(ones, hi) + _nn(ones, mid) + _nn(ones, lo)
    gq = gq_ref[...].astype(F32)
    gk = gk_ref[...].astype(F32)
    qg = (gq * GLA_SCALE * jnp.exp(b)).astype(BF16)
    kd = (gk * jnp.exp(-b)).astype(BF16)
    kl = (gk * jnp.exp(bl - b)).astype(BF16)
    causal = same & (c <= r)
    return qg, kd, kl, jnp.exp(bl), causal


def _gla_finish(o, gr, g):
    return (_rms(o, g) * (gr * jax.nn.sigmoid(gr))).astype(BF16)


def _gla_prompt_body(first_tile, gq_ref, gk_ref, gv_ref, la_ref, gr_ref, g_ref, o_ref, st_ref):
    @pl.when(first_tile)
    def _():
        st_ref[...] = jnp.zeros_like(st_ref)

    qg, kd, kl, dec, causal = _gla_local(gq_ref, gk_ref, la_ref, GLA_CHUNK)
    tt = qg.shape[0]
    chunk_of_row = _iota((tt, GLA_DK), 0) // GLA_CHUNK
    for h in range(GLA_HEADS):
        kc = slice(h * GLA_DK, (h + 1) * GLA_DK)
        vc = slice(h * GLA_DV, (h + 1) * GLA_DV)
        v = gv_ref[:, vc].astype(BF16)
        att = jnp.where(causal, _nt(qg[:, kc], kd[:, kc]), 0.0).astype(BF16)
        intra = _nn(att, v)
        st = st_ref[h]
        n_chunks = tt // GLA_CHUNK
        kl_h = kl[:, kc]
        kl_bd = jnp.concatenate([jnp.where(chunk_of_row == ci, kl_h, jnp.zeros_like(kl_h))
                                 for ci in range(n_chunks)], axis=1)
        upd = _tn(v, kl_bd)
        inter = []
        for ci in range(n_chunks):
            rows = slice(ci * GLA_CHUNK, (ci + 1) * GLA_CHUNK)
            inter.append(_nt(qg[rows, kc], st.astype(BF16)))
            st = dec[ci * GLA_CHUNK:ci * GLA_CHUNK + 1, kc] * st + upd[:, ci * GLA_DK:(ci + 1) * GLA_DK]
        st_ref[h] = st
        o = intra + jnp.concatenate(inter, axis=0)
        o_ref[:, vc] = _gla_finish(o, gr_ref[:, vc].astype(F32), g_ref[...])


def _gla_prompt_call(gq, gk, gv, la, gr, g, tt):
    nb, t, _ = gq.shape
    tiles = t // tt
    tok = lambda n: pl.BlockSpec((None, tt, n), lambda i: (i // tiles, i % tiles, 0))
    def body(*refs):
        _gla_prompt_body(pl.program_id(0) % tiles == 0, *refs)

    return dict(
        name="gla_prompt", steps=nb * tiles, body=body, args=(gq, gk, gv, la, gr, g),
        in_specs=[tok(512), tok(512), tok(1024), tok(512), tok(1024), _resident((1, GLA_DV))],
        out_specs=[tok(1024),
                   pl.BlockSpec((None, GLA_HEADS, GLA_DV, GLA_DK), lambda i: (i // tiles, 0, 0, 0))],
        out_shape=[jax.ShapeDtypeStruct((nb, t, 1024), BF16),
                   jax.ShapeDtypeStruct((nb, GLA_HEADS, GLA_DV, GLA_DK), F32)])


def _run_jobs(*jobs):
    steps = jobs[0]["steps"]
    assert all(j["steps"] == steps for j in jobs)
    n_in = [len(j["in_specs"]) for j in jobs]
    n_out = [len(j["out_specs"]) for j in jobs]
    n_scr = [len(j.get("scratch", ())) for j in jobs]

    def kern(*refs):
        i_pos, o_pos, s_pos = 0, sum(n_in), sum(n_in) + sum(n_out)
        for j, ni, no, ns in zip(jobs, n_in, n_out, n_scr):
            j["body"](*refs[i_pos:i_pos + ni], *refs[o_pos:o_pos + no], *refs[s_pos:s_pos + ns])
            i_pos, o_pos, s_pos = i_pos + ni, o_pos + no, s_pos + ns

    outs = pl.pallas_call(
        kern,
        grid=(steps,),
        in_specs=[s for j in jobs for s in j["in_specs"]],
        out_specs=[s for j in jobs for s in j["out_specs"]],
        out_shape=[s for j in jobs for s in j["out_shape"]],
        scratch_shapes=[s for j in jobs for s in j.get("scratch", ())],
        compiler_params=_params(0, 1),
        name="__".join(j["name"] for j in jobs),
    )(*[a for j in jobs for a in j["args"]])
    split, pos = [], 0
    for no in n_out:
        split.append(list(outs[pos:pos + no]))
        pos += no
    return split


def _gla_sample_kernel(gq_ref, gk_ref, gv_ref, la_ref, gr_ref, g_ref, s_ref, o_ref, so_ref, *, seq):
    qg, kd, kl, dec, causal = _gla_local(gq_ref, gk_ref, la_ref, seq)
    rows_total = qg.shape[0]
    per8 = 8 // seq
    row8 = _iota((8, 1), 0)
    for h in range(GLA_HEADS):
        kc = slice(h * GLA_DK, (h + 1) * GLA_DK)
        vc = slice(h * GLA_DV, (h + 1) * GLA_DV)
        v = gv_ref[:, vc].astype(BF16)
        att = jnp.where(causal, _nt(qg[:, kc], kd[:, kc]), 0.0).astype(BF16)
        intra = _nn(att, v)
        inter = []
        for p in range(rows_total // 8):
            rows = slice(p * 8, (p + 1) * 8)
            d_hi, d_mid, d_lo = _split3(dec[rows, kc])
            inter_p = jnp.zeros((8, GLA_DV), F32)
            for j in range(per8):
                b = p * per8 + j
                r0 = j * seq
                s0 = s_ref[b, h]
                mine = (row8 >= r0) & (row8 < r0 + seq)
                inter_p = jnp.where(mine, _nn(qg[rows, kc], s0.astype(BF16)), inter_p)
                dl = jnp.where(row8 == r0, d_hi, jnp.where(row8 == r0 + 1, d_mid,
                               jnp.where(row8 == r0 + 2, d_lo, jnp.zeros_like(d_lo))))
                e = jnp.where((row8 >= r0) & (row8 < r0 + 3), 1.0, 0.0).astype(BF16)
                dec_b = _tn(dl, jnp.broadcast_to(e, (8, GLA_DV)))
                upd = _tn(jnp.where(mine, kl[rows, kc], jnp.zeros_like(kl[rows, kc])), v[rows])
                so_ref[b, h] = dec_b * s0 + upd
            inter.append(inter_p)
        o = intra + jnp.concatenate(inter, axis=0)
        o_ref[:, vc] = _gla_finish(o, gr_ref[:, vc].astype(F32), g_ref[...])


def _gla_sample_call(gq, gk, gv, la, gr, g, s0, seq, bb):
    n_seq = s0.shape[0]
    rows = bb * seq
    tok = lambda n: pl.BlockSpec((None, rows, n), lambda i: (0, i, 0))
    st = pl.BlockSpec((bb, GLA_HEADS, GLA_DK, GLA_DV), lambda i: (i, 0, 0, 0))
    return dict(
        name="gla_sample", steps=n_seq // bb, body=functools.partial(_gla_sample_kernel, seq=seq),
        args=(gq, gk, gv, la, gr, g, s0),
        in_specs=[tok(512), tok(512), tok(1024), tok(512), tok(1024), _resident((1, GLA_DV)), st],
        out_specs=[tok(1024), st],
        out_shape=[jax.ShapeDtypeStruct((1, n_seq * seq, 1024), BF16),
                   jax.ShapeDtypeStruct(s0.shape, F32)])


def _dsw_prompt_kernel(q_ref, kp_ref, kc_ref, vp_ref, vc_ref, o_ref, lse_ref, *, dil, tile, res):
    qb = q_ref.shape[0]
    first_key = jnp.where(tile == 0, BAND, 0)
    qi = _iota((BAND, 2 * BAND), 0) + BAND
    ki = _iota((BAND, 2 * BAND), 1)
    band = (qi - ki >= 0) & (qi - ki <= BAND)
    lane = _iota((BAND, LANES), 1)
    lane_k = _iota((2 * BAND, LANES), 1)
    for s in range(qb // BAND):
        rows = slice(s * BAND, (s + 1) * BAND)
        if s == 0:
            valid = band & (ki >= first_key)
        else:
            valid = band
        if dil == 1:
            tok_rows = rows
        else:
            tok_rows = pl.ds(s * BAND * dil + res, BAND, stride=dil)
        for hp in range(DSW_GW // LANES):
            cols = slice(hp * LANES, (hp + 1) * LANES)
            qp = q_ref[rows, cols]
            if s == 0:
                kcat = jnp.concatenate([kp_ref[:, cols], kc_ref[0:BAND, cols]], axis=0)
                vcat = jnp.concatenate([vp_ref[:, cols], vc_ref[0:BAND, cols]], axis=0)
            else:
                kcat = kc_ref[(s - 1) * BAND:(s + 1) * BAND, cols]
                vcat = vc_ref[(s - 1) * BAND:(s + 1) * BAND, cols]
            outs, lses = [], []
            for hh in range(LANES // DSW_HEAD_DIM):
                in_head = (lane // DSW_HEAD_DIM) == hh
                sc = _nt(jnp.where(in_head, qp, jnp.zeros_like(qp)), kcat)
                sc = jnp.where(valid, sc, -jnp.inf)
                m = jnp.max(sc, axis=-1, keepdims=True)
                e = jnp.exp(sc - m).astype(BF16)
                v_one = jnp.where((lane_k // DSW_HEAD_DIM) == hh, vcat, jnp.ones_like(vcat))
                pv = _nn(e, v_one)
                den = pltpu.roll(pv, DSW_HEAD_DIM, 1)
                outs.append(pv / den)
                lses.append(m + jnp.log(den))
            first = lane < DSW_HEAD_DIM
            o_ref[hp, tok_rows, :] = jnp.where(first, outs[0], outs[1])
            lse_ref[hp, tok_rows, :] = jnp.where(first, lses[0], lses[1])


def _dsw_prompt_call(q, k, v, dil):
    nb = q.shape[0]
    seq_len = q.shape[-2]
    t = seq_len * dil
    tq = min(DSW_STEP_ROWS, seq_len)
    sub = tq // BAND
    n_tiles = seq_len // tq
    rpb = max(1, min(dil, DSW_STEP_ROWS // tq))
    assert dil % rpb == 0
    n_res = dil // rpb
    bat = lambda i: i // (n_tiles * n_res)
    til = lambda i: (i // n_res) % n_tiles
    grp = lambda i: i % n_res
    prev_blk = lambda i: jnp.maximum(til(i) * sub - 1, 0)
    if dil == 1:
        cur = pl.BlockSpec((None, tq, DSW_GW), lambda i: (bat(i), til(i), 0))
        prev = pl.BlockSpec((None, BAND, DSW_GW), lambda i: (bat(i), prev_blk(i), 0))
    else:
        cur = pl.BlockSpec((None, rpb, tq, DSW_GW), lambda i: (bat(i), grp(i), til(i), 0))
        prev = pl.BlockSpec((None, rpb, BAND, DSW_GW), lambda i: (bat(i), grp(i), prev_blk(i), 0))
    n_slab = DSW_GW // LANES
    out = pl.BlockSpec((None, n_slab, tq * dil, LANES), lambda i: (bat(i), 0, til(i), 0))

    def body(*refs):
        i = pl.program_id(0)
        if dil == 1:
            _dsw_prompt_kernel(*refs, dil=dil, tile=til(i), res=0)
        else:
            for rr in range(rpb):
                _dsw_prompt_kernel(*[r.at[rr] for r in refs[:5]], *refs[5:], dil=dil, tile=til(i),
                                   res=grp(i) * rpb + rr)

    return dict(
        name=f"dsw_prompt_d{dil}", steps=nb * n_tiles * n_res, body=body, args=(q, k, k, v, v),
        in_specs=[cur, prev, cur, prev, cur],
        out_specs=[out, out],
        out_shape=[jax.ShapeDtypeStruct((nb, n_slab, t, LANES), F32)] * 2)


def _dsw_sample_kernel(q_ref, kn_ref, vn_ref, c_ref, o_ref, lse_ref, co_ref, *, seq, dil):
    per8 = 8 // seq
    win = c_ref.shape[2]
    n_rows = DSW_HEADS * 8
    lane = _iota((8, LANES), 1)
    head_of_lane = _iota((8, DSW_GW), 1) // DSW_HEAD_DIM
    row8 = _iota((8, 1), 0)
    r = _iota((n_rows, 1), 0)
    r_step = r % seq
    r_seq = (r % 8) // seq
    key = _iota((n_rows, win), 1)
    cache_ok = ((key % dil) == (r_step % dil)) & (key >= r_step)
    c128 = _iota((n_rows, LANES), 1)
    new_ok = ((c128 < 8) & ((c128 // seq) == r_seq) & ((c128 % seq) <= r_step)
              & (((r_step - c128 % seq) % dil) == 0))
    pad = jnp.zeros((LANES - 8, DSW_GW), BF16)

    def by_head(x):
        out = x[(DSW_HEADS - 1) * 8:DSW_HEADS * 8]
        for h in range(DSW_HEADS - 2, -1, -1):
            out = jnp.where(head_of_lane == h, x[h * 8:(h + 1) * 8], out)
        return out

    lane_sq = _iota((LANES, LANES), 1)
    p_row = _iota((8, LANES), 0)
    for grp in range(q_ref.shape[0] // 8):
        r8 = slice(grp * 8, (grp + 1) * 8)
        q8 = q_ref[r8, :]
        qrows = jnp.concatenate([jnp.where(head_of_lane == h, q8, 0.0) for h in range(DSW_HEADS)],
                                axis=0).astype(BF16)
        kn8 = kn_ref[r8, :]
        vn8 = vn_ref[r8, :]
        kn_t = jnp.concatenate([kn8.astype(BF16), pad], axis=0)
        vn_t = jnp.concatenate([vn8.astype(BF16), pad], axis=0)
        scn = jnp.where(new_ok, _nt(qrows, kn_t), -jnp.inf)
        m_new = jnp.max(scn, axis=-1, keepdims=True)
        o_p = jnp.zeros((8, DSW_GW), F32)
        l_p = jnp.zeros((8, DSW_GW), F32)
        for j in range(per8):
            b = grp * per8 + j
            kt = c_ref[b, 0:DSW_GW, :].astype(BF16)
            vt = c_ref[b, DSW_GW:2 * DSW_GW, :].astype(BF16)
            sc = jnp.where(cache_ok, _nn(qrows, kt), -jnp.inf)
            m = jnp.maximum(jnp.max(sc, axis=-1, keepdims=True), m_new)
            e = jnp.exp(sc - m)
            en = jnp.exp(scn - m)
            den = jnp.sum(e, axis=-1, keepdims=True) + jnp.sum(en, axis=-1, keepdims=True)
            o = _nt((e / den).astype(BF16), vt) + _nn((en / den).astype(BF16), vn_t)
            lse = jnp.broadcast_to(m + jnp.log(den), (n_rows, DSW_GW))
            mine = (row8 // seq) == j
            o_p = jnp.where(mine, by_head(o), o_p)
            l_p = jnp.where(mine, by_head(lse), l_p)
        o_ref[r8, :] = o_p
        lse_ref[r8, :] = l_p

        hi, mid, lo = _split3(jnp.concatenate([kn8, vn8], axis=1))
        for j in range(per8):
            b = grp * per8 + j
            place = jnp.where(((p_row // seq) == j) & (lane == LANES - seq + p_row % seq), 1.0, 0.0).astype(BF16)
            new_cols = _tn(hi, place) + _tn(mid, place) + _tn(lo, place)
            for blk in range(2 * DSW_GW // LANES):
                rows = slice(blk * LANES, (blk + 1) * LANES)
                rolled = pltpu.roll(c_ref[b, rows, :], win - seq, 1)
                if win > LANES:
                    co_ref[b, rows, 0:win - LANES] = rolled[:, 0:win - LANES]
                co_ref[b, rows, win - LANES:win] = jnp.where(lane_sq < LANES - seq, rolled[:, win - LANES:win],
                                                             new_cols[rows])


def _dsw_sample_call(q32, k32, v32, cache, g, dil, seq, steps=None):
    n_seq, win = cache.shape[0], cache.shape[1]
    per8 = 8 // seq
    if steps is None:
        groups = max(1, min(8, 4 * 512 // win))
    else:
        groups = n_seq // (per8 * steps)
    n_blk = per8 * groups
    view = jnp.transpose(cache, (0, 2, 3, 4, 1)).reshape(n_seq, 2 * DSW_GW, win)
    tok = pl.BlockSpec((None, 8 * groups, DSW_GW), lambda i: (0, i, g))
    tok_out = pl.BlockSpec((None, 8 * groups, DSW_GW), lambda i: (0, i, 0))
    cspec = pl.BlockSpec((n_blk, 2 * DSW_GW, win), lambda i: (i, 0, 0))
    return dict(
        name=f"dsw_sample_d{dil}", steps=n_seq // n_blk,
        body=functools.partial(_dsw_sample_kernel, seq=seq, dil=dil), args=(q32, k32, v32, view),
        in_specs=[tok, tok, tok, cspec],
        out_specs=[tok_out, tok_out, cspec],
        out_shape=[jax.ShapeDtypeStruct((1, n_seq * seq, DSW_GW), F32)] * 2
                  + [jax.ShapeDtypeStruct(view.shape, F32)])


def _dsw_sample_finish(o, lse, new, cache_shape):
    n_seq, win = cache_shape[0], cache_shape[1]
    new = jnp.transpose(new.reshape(n_seq, 2, DSW_HEADS, DSW_HEAD_DIM, win), (0, 4, 1, 2, 3))
    n_slab = DSW_GW // LANES
    slabs = lambda a: jnp.transpose(a.reshape(1, -1, n_slab, LANES), (0, 2, 1, 3))
    return slabs(o), slabs(lse), new


def _merge_kernel(oa_ref, o0_ref, o1_ref, o2_ref, l0_ref, l1_ref, l2_ref, ga_ref, gb_ref, x_ref,
                  g1_ref, sc_ref, sh_ref, n2_ref, wpa_ref, wpb_ref, wo_ref, x1_ref, h2_ref):
    ob = []
    for slab in range(DSW_GW // LANES):
        l0, l1, l2 = l0_ref[slab], l1_ref[slab], l2_ref[slab]
        m = jnp.maximum(jnp.maximum(l0, l1), l2)
        w0, w1, w2 = jnp.exp(l0 - m), jnp.exp(l1 - m), jnp.exp(l2 - m)
        den = w0 + w1 + w2
        ob.append((w0 / den) * o0_ref[slab] + (w1 / den) * o1_ref[slab] + (w2 / den) * o2_ref[slab])
    ob = jnp.concatenate(ob, axis=1).astype(BF16)
    merged = (jax.nn.sigmoid(ga_ref[...].astype(F32)) * _nn(oa_ref[...], wpa_ref[...])
              + jax.nn.sigmoid(gb_ref[...].astype(F32)) * _nn(ob, wpb_ref[...]))
    x1 = x_ref[...] + g1_ref[...] * _nn(merged.astype(BF16), wo_ref[...])
    x1_ref[...] = x1
    h2_ref[...] = (_rms(x1, n2_ref[...]) * (1.0 + sc_ref[...]) + sh_ref[...]).astype(BF16)


def _flat_tok_spec(tm, tiles, n):
    return pl.BlockSpec((None, tm, n), lambda i: (i // tiles, i % tiles, 0))


def _flat_mod_spec(m, tm, tiles, d):
    arr, k = m
    if arr.ndim == 4:
        return pl.BlockSpec((None, None, 1, d), lambda i: (i // tiles, k, 0, 0))
    return pl.BlockSpec((tm, d), lambda i: (i % tiles, k))


def _merge_call(oa, og, lg, ga, gb, x, g1, sc2, sh2, n2, wpa, wpb, wo, tm):
    nb, t, d = x.shape
    tiles = t // tm
    tok = functools.partial(_flat_tok_spec, tm, tiles)
    mod = lambda a: _flat_mod_spec(a, tm, tiles, d)
    slab = pl.BlockSpec((None, DSW_GW // LANES, tm, LANES), lambda i: (i // tiles, 0, i % tiles, 0))
    return dict(
        name="merge_outproj", steps=nb * tiles, body=_merge_kernel,
        args=(oa, *og, *lg, ga, gb, x, g1[0], sc2[0], sh2[0], n2, wpa, wpb, wo),
        in_specs=[tok(1024)] + [slab] * 6 + [tok(d), tok(d), tok(d), mod(g1), mod(sc2), mod(sh2),
                  _resident((1, d)), _resident(wpa.shape), _resident(wpb.shape), _resident(wo.shape)],
        out_specs=[tok(d), tok(d)],
        out_shape=[jax.ShapeDtypeStruct((nb, t, d), F32), jax.ShapeDtypeStruct((nb, t, d), BF16)])


def _ffn_kernel(h_ref, x_ref, g2_ref, nf_ref, wu_ref, wd_ref, y_ref, *, final_norm, n_split):
    h = h_ref[...]
    d_ff = wd_ref.shape[0]
    step = d_ff // n_split
    acc = None
    for j in range(n_split):
        u1 = _nn(h, wu_ref[:, j * step:(j + 1) * step])
        u2 = _nn(h, wu_ref[:, d_ff + j * step:d_ff + (j + 1) * step])
        a = (u1 * jax.nn.sigmoid(u1) * u2).astype(BF16)
        part = _nn(a, wd_ref[j * step:(j + 1) * step, :])
        acc = part if acc is None else acc + part
    x2 = x_ref[...] + g2_ref[...] * acc
    y_ref[...] = _rms(x2, nf_ref[...]) if final_norm else x2


def _ffn_call(h2, x1, g2, nf, wu, wd, tm, final_norm):
    nb, t, d = x1.shape
    tiles = t // tm
    tok = functools.partial(_flat_tok_spec, tm, tiles)
    return dict(
        name="ffn", steps=nb * tiles,
        body=functools.partial(_ffn_kernel, final_norm=final_norm, n_split=wd.shape[0] // (2 * LANES)),
        args=(h2, x1, g2[0], nf, wu, wd),
        in_specs=[tok(d), tok(d), _flat_mod_spec(g2, tm, tiles, d), _resident((1, d)),
                  _resident(wu.shape), _resident(wd.shape)],
        out_specs=[tok(d)],
        out_shape=[jax.ShapeDtypeStruct((nb, t, d), F32)])


def _rope_tables(pos):
    half = DSW_HEAD_DIM // 2
    inv = ROPE_THETA ** (-np.arange(half, dtype=np.float64) / half)
    ang = np.asarray(pos, np.float64)[:, None] * inv[None, :]
    reps = LANES // half
    sign = np.tile(np.concatenate([-np.ones(half), np.ones(half)]), LANES // DSW_HEAD_DIM)
    cosf = np.tile(np.cos(ang), (1, reps))
    sins = np.tile(np.sin(ang), (1, reps)) * sign[None, :]
    return jnp.asarray(cosf, F32), jnp.asarray(sins, F32)


def _layer_weights(w_in, b_in, w_alpha2, b_alpha2):
    bf = lambda a: a.astype(BF16)
    row = lambda a: a.reshape(1, -1)
    o_glr, o_dq, o_ga = 3072, 3088, 5392
    pad_r = LANES - GLA_RANK
    wt = w_in.T
    return dict(
        wa=bf(wt[:o_glr]), ba=row(b_in[:o_glr]),
        wg=bf(jnp.pad(wt[o_glr:o_dq], ((0, pad_r), (0, 0)))), bg=row(jnp.pad(b_in[o_glr:o_dq], (0, pad_r))),
        w2=bf(jnp.pad(w_alpha2, ((0, pad_r), (0, 0)))), b2=row(b_alpha2),
        wb=bf(wt[o_dq:o_ga]), bb=row(b_in[o_dq:o_ga]),
        wc=bf(wt[o_ga:]), bc=row(b_in[o_ga:]))


def _kv_unstack(kvt, keep):
    nb, _, width = kvt.shape
    kv = kvt[:, :, width - keep:].reshape(nb, 2, DSW_HEADS, DSW_HEAD_DIM, keep)
    return jnp.transpose(kv, (0, 4, 1, 2, 3))


def kernel(x_prompt, x_sample, state_gla, cache_kv_w128, cache_kv_w512, cache_kv_w2048, c_prompt, c_sample,
           norm1_g, norm2_g, w_ada, b_ada, w_in, b_in, w_alpha2, b_alpha2, gla_norm_g, w_proj_a, w_proj_b,
           w_out, w_up, w_down, normf_g):
    depth = w_ada.shape[0]
    nb, t, d = x_prompt.shape
    n_seq, seq, _ = x_sample.shape
    assert 8 % seq == 0 and seq >= 3, "sample kernels pack whole sequences into 8-row groups"
    past = PAST_LEN
    caches = (cache_kv_w128, cache_kv_w512, cache_kv_w2048)

    cos_p, sin_p = _rope_tables(np.arange(t))
    cos_s, sin_s = _rope_tables(np.tile(past + np.arange(seq), n_seq))

    n_tok_s = n_seq * seq
    pad_c = (-(n_tok_s + nb)) % 8
    c_all = jnp.pad(jnp.concatenate([jnp.repeat(c_sample, seq, axis=0), c_prompt], axis=0), ((0, pad_c), (0, 0)))

    xp = x_prompt
    xs = x_sample.reshape(1, n_seq * seq, d)
    row = lambda a: a.reshape(1, -1)
    sp_l, kvp_l, ss_l, kvs_l = [], [], [], []
    for l in range(depth):
        w = _layer_weights(w_in[l], b_in[l], w_alpha2[l], b_alpha2[l])
        mod = _ada(c_all, w_ada[l], b_ada[l])
        mod_rows_p = mod[n_tok_s:n_tok_s + nb].reshape(nb, 6, 1, d)
        mod_p = [(mod_rows_p, k) for k in range(6)]
        mod_s = [(mod, k) for k in range(6)]
        last = l == depth - 1

        sh1_s, sc1_s, g1_s, sh2_s, sc2_s, g2_s = mod_s
        ((gq_s, gk_s, gv_s, gr_s, la_s, ga_s, gb_s, k32, v32, q32),) = _run_jobs(_inproj_call(
            xs, sc1_s, sh1_s, row(norm1_g[l]), cos_s, sin_s, w, tm=min(512, n_seq * seq), fold=False))
        sh1, sc1, g1, sh2, sc2, g2 = mod_p
        inproj_p = _inproj_call(xp, sc1, sh1, row(norm1_g[l]), cos_p, sin_p, w, tm=512, fold=True)
        late = (w_proj_a[l], w_proj_b[l], w_out[l], w_up[l], w_down[l])
        (gq, gk, gv, gr, la, ga, gb, *dsw_p), *cast = _run_jobs(
            inproj_p, *[_cast_call(a, inproj_p["steps"]) for a in late])
        w.update(zip(("wpa", "wpb", "wo", "wu", "wd"), (c[0] for c in cast)))
        qkv, kvt = dsw_p[:3 * len(DSW_GROUPS)], dsw_p[3 * len(DSW_GROUPS):]

        def with_sample_group(host, g):
            steps = host["steps"]
            fits = n_seq % ((8 // seq) * steps) == 0 and n_seq // ((8 // seq) * steps) <= 8
            job = _dsw_sample_call(q32, k32, v32, caches[g][l], g, DSW_GROUPS[g][1], seq, steps if fits else None)
            if fits:
                host_out, job_out = _run_jobs(host, job)
            else:
                (host_out,), (job_out,) = _run_jobs(host), _run_jobs(job)
            return host_out, _dsw_sample_finish(*job_out, caches[g][l].shape)

        sample_dsw = [None] * len(DSW_GROUPS)
        (oa, st), sample_dsw[2] = with_sample_group(
            _gla_prompt_call(gq, gk, gv, la, gr, row(gla_norm_g[l]), tt=256), 2)
        dsw_jobs = [_dsw_prompt_call(*qkv[3 * g:3 * g + 3], dil) for g, (_, dil) in enumerate(DSW_GROUPS)]
        gla_s_args = (gq_s, gk_s, gv_s, la_s, gr_s, row(gla_norm_g[l]), state_gla[l], seq)
        host = next((j for j in dsw_jobs if n_seq % j["steps"] == 0 and (n_seq // j["steps"]) * seq % 16 == 0), None)
        og, lg = [], []
        for job in dsw_jobs:
            if job is host:
                (o_g, l_g), (oa_s, s_new) = _run_jobs(job, _gla_sample_call(*gla_s_args, bb=n_seq // job["steps"]))
            else:
                ((o_g, l_g),) = _run_jobs(job)
            og.append(o_g)
            lg.append(l_g)
        if host is None:
            ((oa_s, s_new),) = _run_jobs(_gla_sample_call(*gla_s_args, bb=8))

        (x1, h2), sample_dsw[0] = with_sample_group(
            _merge_call(oa, og, lg, ga, gb, xp, g1, sc2, sh2, row(norm2_g[l]), w["wpa"], w["wpb"], w["wo"], tm=512), 0)
        (xp,), sample_dsw[1] = with_sample_group(
            _ffn_call(h2, x1, g2, row(normf_g), w["wu"], w["wd"], tm=512, final_norm=last), 1)
        og_s, lg_s, new_kv = zip(*sample_dsw)

        ((x1, h2),) = _run_jobs(_merge_call(oa_s, og_s, lg_s, ga_s, gb_s, xs, g1_s, sc2_s, sh2_s, row(norm2_g[l]),
                                            w["wpa"], w["wpb"], w["wo"], tm=min(512, n_seq * seq)))
        ((xs,),) = _run_jobs(_ffn_call(h2, x1, g2_s, row(normf_g), w["wu"], w["wd"], tm=min(512, n_seq * seq), final_norm=last))

        sp_l.append(jnp.swapaxes(st, 2, 3))
        kvp_l.append(tuple(_kv_unstack(kvt[g], min(win, t)) for g, (win, _) in enumerate(DSW_GROUPS)))
        ss_l.append(s_new)
        kvs_l.append(new_kv)

    y_prompt = xp
    y_sample = xs.reshape(n_seq, seq, d)
    stack = lambda items: jnp.stack(list(items))
    return (y_prompt, y_sample, stack(sp_l),
            stack(kv[0] for kv in kvp_l), stack(kv[1] for kv in kvp_l), stack(kv[2] for kv in kvp_l),
            stack(ss_l),
            stack(kv[0] for kv in kvs_l), stack(kv[1] for kv in kvs_l), stack(kv[2] for kv in kvs_l))
```
